```python
import jax, jax.numpy as jnp
from jax import lax
import numpy as np

D_MODEL = 4096
BATCH = 4
SEQ = 4096
DEPTH = 1
DEC_BATCH = 2
DEC_SEQ = 4096
PAST_LEN = 128

SSD_HEAD_DIM = 64
SSD_INNER = D_MODEL
SSD_HEADS = SSD_INNER // SSD_HEAD_DIM
SSD_GROUPS = 8
SSD_STATE = 128
SSD_CONV = 5
SSD_CHUNK = 128
SSD_CONV_DIM = SSD_INNER + 2 * SSD_GROUPS * SSD_STATE
SSD_NORM_GROUP = SSD_INNER // SSD_GROUPS
POOL_WINDOWS = (2, 4, 8, 16)
POOL_GROUPS = 4
POOL_GROUP_DIM = D_MODEL // 8
POOL_DIM = POOL_GROUPS * POOL_GROUP_DIM
D_FF = 4 * D_MODEL
PLE_DIM = 256
N_BRANCH = 2
EPS = 1e-6

IN_COLS = SSD_INNER + SSD_CONV_DIM + 2 * SSD_HEADS + POOL_DIM + N_BRANCH * D_MODEL
SPLITS = (
    SSD_INNER,
    SSD_INNER + SSD_CONV_DIM,
    SSD_INNER + SSD_CONV_DIM + SSD_HEADS,
    SSD_INNER + SSD_CONV_DIM + 2 * SSD_HEADS,
    SSD_INNER + SSD_CONV_DIM + 2 * SSD_HEADS + POOL_DIM,
    SSD_INNER + SSD_CONV_DIM + 2 * SSD_HEADS + POOL_DIM + D_MODEL,
)

kernel_name = "hybrid_ssd_pool_encoder"


def rms_norm(x, g):
    xf = x.astype(jnp.float32)
    xf = xf * lax.rsqrt(jnp.mean(xf * xf, axis=-1, keepdims=True) + EPS)
    return xf.astype(x.dtype) * g


def depthwise_centred_conv(u, w, b):
    out = lax.conv_general_dilated(
        u, w[:, None, :].astype(u.dtype), window_strides=(1,),
        padding=[(SSD_CONV // 2, SSD_CONV // 2)],
        dimension_numbers=('NWC', 'WIO', 'NWC'),
        feature_group_count=u.shape[-1])
    return out + b


def ssd_scan(x, dt, A, B, C):
    b, s, h, p = x.shape
    g, n = B.shape[2], B.shape[3]
    e = h // g
    nc = s // SSD_CHUNK
    L = SSD_CHUNK
    xc = (x * dt[..., None]).reshape(b, nc, L, g, e, p)
    a = (dt * A).reshape(b, nc, L, g, e)
    a_cum = jnp.cumsum(a, axis=2)
    Bc = B.reshape(b, nc, L, g, n)
    Cc = C.reshape(b, nc, L, g, n)
    seg = a_cum[:, :, :, None] - a_cum[:, :, None, :]
    mask = jnp.tril(jnp.ones((L, L), dtype=bool))[None, None, :, :, None, None]
    decay = jnp.exp(jnp.where(mask, seg, -jnp.inf))
    cb = jnp.einsum('bclgn,bcsgn->bclsg', Cc, Bc)
    y_diag = jnp.einsum('bclsg,bclsge,bcsgep->bclgep', cb, decay, xc)
    decay_to_end = jnp.exp(a_cum[:, :, -1:] - a_cum)
    states = jnp.einsum('bclgn,bclge,bclgep->bcgepn', Bc, decay_to_end, xc)
    chunk_decay = jnp.exp(a_cum[:, :, -1])

    def step(h_prev, inp):
        st, dec = inp
        h_new = dec[..., None, None] * h_prev + st
        return h_new, h_prev

    h0 = jnp.zeros((b, g, e, p, n), dtype=states.dtype)
    _, prev = lax.scan(step, h0, (jnp.swapaxes(states, 0, 1), jnp.swapaxes(chunk_decay, 0, 1)))
    prev = jnp.swapaxes(prev, 0, 1)
    y_off = jnp.einsum('bclgn,bcgepn,bclge->bclgep', Cc, prev, jnp.exp(a_cum))
    return (y_diag + y_off).reshape(b, s, h, p).astype(x.dtype)


def multiscale_pool(u, pool_w, pool_scale):
    b, s, _ = u.shape
    uf = u.astype(jnp.float32).reshape(b, s, POOL_GROUPS, POOL_GROUP_DIM)
    cs = jnp.concatenate([jnp.zeros((b, 1, POOL_GROUPS, POOL_GROUP_DIM), jnp.float32),
                          jnp.cumsum(uf, axis=1)], axis=1)
    t = jnp.arange(s)
    outs = []
    for gi, w in enumerate(POOL_WINDOWS):
        lo = jnp.clip(t - w // 2, 0, s)
        hi = jnp.clip(t + (w - w // 2), 0, s)
        csg = cs[:, :, gi]
        mean = (csg[:, hi] - csg[:, lo]) / (hi - lo).astype(jnp.float32)[None, :, None]
        outs.append(mean - uf[:, :, gi])
    pooled = jnp.stack(outs, axis=2).astype(u.dtype)
    y = jnp.einsum('bsgc,gcd->bsgd', pooled, pool_w).reshape(b, s, POOL_DIM)
    return y * pool_scale


def encoder_layer(x, p_l, norm_mix_g, w_in, conv_w, conv_b, dt_bias_f, dt_bias_b,
                  a_log_f, a_log_b, d_skip, ssd_norm_g, w_ssd_up, pool_w, pool_scale,
                  w_pool_up, w_out, norm_mlp_g, w_ff1, w_ff2, norm_ple_g, w_ple_gate,
                  w_ple_proj):
    b, s, _ = x.shape
    h = rms_norm(x, norm_mix_g)
    proj = h @ w_in
    z, xbc, dt_f, dt_b, u_pool, g_a, g_b = jnp.split(proj, SPLITS, axis=-1)

    xbc = jax.nn.silu(depthwise_centred_conv(xbc, conv_w, conv_b))
    xs, Bm, Cm = jnp.split(xbc, (SSD_INNER, SSD_INNER + SSD_GROUPS * SSD_STATE), axis=-1)
    xs = xs.reshape(b, s, SSD_HEADS, SSD_HEAD_DIM)
    Bm = Bm.reshape(b, s, SSD_GROUPS, SSD_STATE)
    Cm = Cm.reshape(b, s, SSD_GROUPS, SSD_STATE)
    dtf = jax.nn.softplus((dt_f + dt_bias_f).astype(jnp.float32))
    dtb = jax.nn.softplus((dt_b + dt_bias_b).astype(jnp.float32))
    A_f = -jnp.exp(a_log_f.astype(jnp.float32))
    A_b = -jnp.exp(a_log_b.astype(jnp.float32))
    y_f = ssd_scan(xs, dtf, A_f, Bm, Cm)
    y_b = jnp.flip(ssd_scan(jnp.flip(xs, 1), jnp.flip(dtb, 1), A_b,
                            jnp.flip(Bm, 1), jnp.flip(Cm, 1)), 1)
    y = y_f + y_b + d_skip[:, None] * xs
    y = y.reshape(b, s, SSD_INNER) * jax.nn.silu(z)
    y = rms_norm(y.reshape(b, s, SSD_GROUPS, SSD_NORM_GROUP), 1.0).reshape(b, s, SSD_INNER) * ssd_norm_g
    branch_a = y @ w_ssd_up

    branch_b = multiscale_pool(u_pool, pool_w, pool_scale) @ w_pool_up

    mixed = jax.nn.sigmoid(g_a) * branch_a + jax.nn.sigmoid(g_b) * branch_b
    x = x + mixed @ w_out

    h2 = rms_norm(x, norm_mlp_g)
    x = x + jnp.square(jax.nn.relu(h2 @ w_ff1)) @ w_ff2

    gate = jax.nn.sigmoid(rms_norm(x, norm_ple_g) @ w_ple_gate)
    x = x + gate * (p_l @ w_ple_proj)
    return x


def run_trunk(x, p, layer_params, norm_final_g):
    for i in range(DEPTH):
        x = encoder_layer(x, p[i], *[w[i] for w in layer_params])
    return rms_norm(x, norm_final_g)


def setup_inputs(seed: int = 0) -> dict:
    key = jax.random.key(seed)
    ks = jax.random.split(key, 32)
    f32 = jnp.float32

    def nrm(k, shape, scale):
        return jax.random.normal(k, shape, f32) * scale

    def gain(k, shape):
        return 1.0 + 0.02 * jax.random.normal(k, shape, f32)

    dt_init_f = jnp.exp(jax.random.uniform(ks[8], (DEPTH, SSD_HEADS), f32, np.log(1e-3), np.log(1e-1)))
    dt_init_b = jnp.exp(jax.random.uniform(ks[9], (DEPTH, SSD_HEADS), f32, np.log(1e-3), np.log(1e-1)))
    return {
        "x_prompt": nrm(ks[0], (BATCH, SEQ, D_MODEL), 1.0),
        "x_sample": nrm(ks[1], (DEC_BATCH, DEC_SEQ, D_MODEL), 1.0),
        "p_prompt": nrm(ks[2], (DEPTH, BATCH, SEQ, PLE_DIM), 1.0),
        "p_sample": nrm(ks[3], (DEPTH, DEC_BATCH, DEC_SEQ, PLE_DIM), 1.0),
        "norm_mix_g": gain(ks[4], (DEPTH, D_MODEL)),
        "w_in": nrm(ks[5], (DEPTH, D_MODEL, IN_COLS), D_MODEL ** -0.5),
        "conv_w": nrm(ks[6], (DEPTH, SSD_CONV, SSD_CONV_DIM), SSD_CONV ** -0.5),
        "conv_b": nrm(ks[7], (DEPTH, SSD_CONV_DIM), 0.02),
        "dt_bias_f": dt_init_f + jnp.log(-jnp.expm1(-dt_init_f)),
        "dt_bias_b": dt_init_b + jnp.log(-jnp.expm1(-dt_init_b)),
        "a_log_f": jnp.log(jax.random.uniform(ks[10], (DEPTH, SSD_HEADS), f32, 1.0, 16.0)),
        "a_log_b": jnp.log(jax.random.uniform(ks[11], (DEPTH, SSD_HEADS), f32, 1.0, 16.0)),
        "d_skip": gain(ks[12], (DEPTH, SSD_HEADS)),
        "ssd_norm_g": gain(ks[13], (DEPTH, SSD_INNER)),
        "w_ssd_up": nrm(ks[14], (DEPTH, SSD_INNER, D_MODEL), SSD_INNER ** -0.5),
        "pool_w": nrm(ks[15], (DEPTH, POOL_GROUPS, POOL_GROUP_DIM, POOL_GROUP_DIM), POOL_GROUP_DIM ** -0.5),
        "pool_scale": gain(ks[16], (DEPTH, POOL_DIM)),
        "w_pool_up": nrm(ks[17], (DEPTH, POOL_DIM, D_MODEL), POOL_DIM ** -0.5),
        "w_out": nrm(ks[18], (DEPTH, D_MODEL, D_MODEL), D_MODEL ** -0.5),
        "norm_mlp_g": gain(ks[19], (DEPTH, D_MODEL)),
        "w_ff1": nrm(ks[20], (DEPTH, D_MODEL, D_FF), D_MODEL ** -0.5),
        "w_ff2": nrm(ks[21], (DEPTH, D_FF, D_MODEL), D_FF ** -0.5),
        "norm_ple_g": gain(ks[22], (DEPTH, D_MODEL)),
        "w_ple_gate": nrm(ks[23], (DEPTH, D_MODEL, D_MODEL), D_MODEL ** -0.5),
        "w_ple_proj": nrm(ks[24], (DEPTH, PLE_DIM, D_MODEL), PLE_DIM ** -0.5),
        "norm_final_g": gain(ks[25], (D_MODEL,)),
    }


def reference(x_prompt, x_sample, p_prompt, p_sample, norm_mix_g, w_in, conv_w, conv_b,
              dt_bias_f, dt_bias_b, a_log_f, a_log_b, d_skip, ssd_norm_g, w_ssd_up,
              pool_w, pool_scale, w_pool_up, w_out, norm_mlp_g, w_ff1, w_ff2,
              norm_ple_g, w_ple_gate, w_ple_proj, norm_final_g):
    layer_params = (norm_mix_g, w_in, conv_w, conv_b, dt_bias_f, dt_bias_b, a_log_f, a_log_b,
                    d_skip, ssd_norm_g, w_ssd_up, pool_w, pool_scale, w_pool_up, w_out,
                    norm_mlp_g, w_ff1, w_ff2, norm_ple_g, w_ple_gate, w_ple_proj)
    y_prompt = run_trunk(x_prompt, p_prompt, layer_params, norm_final_g)
    y_sample = run_trunk(x_sample, p_sample, layer_params, norm_final_g)
    return (y_prompt, y_sample)
```

```python
import functools

import jax
import jax.numpy as jnp
from jax import lax
from jax.experimental import pallas as pl
from jax.experimental.pallas import tpu as pltpu

F32 = jnp.float32
BF16 = jnp.bfloat16

EPS = 1e-6
SSD_GROUPS = 8
SSD_STATE = 128
SSD_CHUNK = 128
POOL_WINDOWS = (2, 4, 8, 16)

V7X_VMEM_BYTES = 64 * 1024 * 1024
V7X_VMEM_RESERVE_BYTES = 8 * 1024 * 1024
BF16_SUBLANE_TILE = 16
HALO = BF16_SUBLANE_TILE


def _pick(dim, prefs):
    for p in prefs:
        if dim % p == 0:
            return p
    return dim


def _nbytes(shape, dtype):
    n = 1
    for s in shape:
        n *= s
    return n * jnp.dtype(dtype).itemsize


def _params(semantics, block_bytes, scratch_bytes=0):
    est = 2 * block_bytes + scratch_bytes
    limit = min(2 * est + (4 << 20), V7X_VMEM_BYTES - V7X_VMEM_RESERVE_BYTES)
    return pltpu.CompilerParams(dimension_semantics=semantics, vmem_limit_bytes=int(limit))


def _sigmoid(v):
    return 1.0 / (1.0 + jnp.exp(-v))


def _softplus(v):
    return jnp.maximum(v, 0.0) + jnp.log1p(jnp.exp(-jnp.abs(v)))


def _split3(v):
    a = v.astype(BF16)
    r = v - a.astype(F32)
    b = r.astype(BF16)
    c = (r - b.astype(F32)).astype(BF16)
    return a, b, c


def _split2(v):
    a = v.astype(BF16)
    b = (v - a.astype(F32)).astype(BF16)
    return a, b


def _dot(a, b):
    return jnp.dot(a, b, preferred_element_type=F32)


def _dot_nt(a, b):
    return lax.dot_general(a, b, (((1,), (1,)), ((), ())), preferred_element_type=F32)


def _dot_tn(a, b):
    return lax.dot_general(a, b, (((0,), (0,)), ((), ())), preferred_element_type=F32)


def _rms(x, g):
    ms = jnp.mean(x * x, axis=-1, keepdims=True)
    return x * lax.rsqrt(ms + EPS) * g


def _norm_dt_kernel(x_ref, g_ref, wdt_ref, wdtT_ref, h_ref, dt_ref, dtT_ref):
    h = _rms(x_ref[...], g_ref[...]).astype(BF16)
    h_ref[...] = h
    dt_ref[...] = _dot(h, wdt_ref[...])
    dtT_ref[...] = _dot_nt(wdtT_ref[...], h)


def _norm_dt(x, g, w_dt, w_dtT):
    M, D = x.shape
    H2 = w_dt.shape[1]
    tm = _pick(M, (512, 256, 128))
    blk = (_nbytes((tm, D), F32) + _nbytes((tm, D), BF16) + 2 * _nbytes((D, H2), BF16)
           + 2 * _nbytes((tm, H2), F32))
    return pl.pallas_call(
        _norm_dt_kernel,
        grid=(M // tm,),
        in_specs=[pl.BlockSpec((tm, D), lambda i: (i, 0)),
                  pl.BlockSpec((1, D), lambda i: (0, 0)),
                  pl.BlockSpec((D, H2), lambda i: (0, 0)),
                  pl.BlockSpec((H2, D), lambda i: (0, 0))],
        out_specs=[pl.BlockSpec((tm, D), lambda i: (i, 0)),
                   pl.BlockSpec((tm, H2), lambda i: (i, 0)),
                   pl.BlockSpec((H2, tm), lambda i: (0, i))],
        out_shape=[jax.ShapeDtypeStruct((M, D), BF16),
                   jax.ShapeDtypeStruct((M, H2), F32),
                   jax.ShapeDtypeStruct((H2, M), F32)],
        compiler_params=_params(("parallel",), blk),
        name="norm_dt",
    )(x, g.reshape(1, D), w_dt, w_dtT)


def _norm_kernel(x_ref, g_ref, o_ref):
    o_ref[...] = _rms(x_ref[...], g_ref[...]).astype(o_ref.dtype)


def _norm(x, g, out_dtype, name):
    M, D = x.shape
    tm = _pick(M, (512, 256, 128))
    blk = _nbytes((tm, D), F32) + _nbytes((tm, D), out_dtype)
    return pl.pallas_call(
        _norm_kernel,
        grid=(M // tm,),
        in_specs=[pl.BlockSpec((tm, D), lambda i: (i, 0)),
                  pl.BlockSpec((1, D), lambda i: (0, 0))],
        out_specs=pl.BlockSpec((tm, D), lambda i: (i, 0)),
        out_shape=jax.ShapeDtypeStruct((M, D), out_dtype),
        compiler_params=_params(("parallel",), blk),
        name=name,
    )(x, g.reshape(1, D))


def _mm_kernel(lhs_ref, w_ref, *rest, epilogue):
    out_ref = rest[-1]
    acc = _dot(lhs_ref[...], w_ref[...])
    out_ref[...] = epilogue(acc, *rest[:-1]).astype(out_ref.dtype)


def _matmul(lhs, w, out_dtype, name, epilogue=None, extras=(), extra_specs=None, tm=None, tn=None):
    M, K = lhs.shape
    N = w.shape[1]
    tm = tm or _pick(M, (1024, 512, 256, 128))
    tn = tn or _pick(N, (512, 256, 128))
    if epilogue is None:
        epilogue = lambda acc: acc
    specs = extra_specs(tm, tn) if extra_specs else []
    blk = (_nbytes((tm, K), lhs.dtype) + _nbytes((K, tn), w.dtype) + _nbytes((tm, tn), out_dtype)
           + sum(_nbytes(s.block_shape, e.dtype) for s, e in zip(specs, extras)))
    return pl.pallas_call(
        functools.partial(_mm_kernel, epilogue=epilogue),
        grid=(M // tm, N // tn),
        in_specs=[pl.BlockSpec((tm, K), lambda i, j: (i, 0)),
                  pl.BlockSpec((K, tn), lambda i, j: (0, j))] + specs,
        out_specs=pl.BlockSpec((tm, tn), lambda i, j: (i, j)),
        out_shape=jax.ShapeDtypeStruct((M, N), out_dtype),
        compiler_params=_params(("parallel", "arbitrary"), blk, _nbytes((tm, tn), F32)),
        name=name,
    )(lhs, w, *extras)


def _tile_spec(col_offset):
    def make(tm, tn):
        assert col_offset % tn == 0
        off = col_offset // tn
        return pl.BlockSpec((tm, tn), lambda i, j: (i, off + j))
    return make


def _mm_kacc_kernel(lhs_ref, w_ref, res_ref, out_ref):
    @pl.when(pl.program_id(2) == 0)
    def _():
        out_ref[...] = res_ref[...]
    out_ref[...] += _dot(lhs_ref[...], w_ref[...])


def _matmul_kacc_residual(lhs, w, res, name):
    M, K = lhs.shape
    N = w.shape[1]
    tm = _pick(M, (1024, 512, 256, 128))
    tn = _pick(N, (1024, 512, 256, 128))
    tk = _pick(K, (2048, 1024, 512, 256, 128))
    blk = (_nbytes((tm, tk), lhs.dtype) + _nbytes((tk, tn), w.dtype) + 2 * _nbytes((tm, tn), F32))
    return pl.pallas_call(
        _mm_kacc_kernel,
        grid=(M // tm, N // tn, K // tk),
        in_specs=[pl.BlockSpec((tm, tk), lambda i, j, k: (i, k)),
                  pl.BlockSpec((tk, tn), lambda i, j, k: (k, j)),
                  pl.BlockSpec((tm, tn), lambda i, j, k: (i, j))],
        out_specs=pl.BlockSpec((tm, tn), lambda i, j, k: (i, j)),
        out_shape=jax.ShapeDtypeStruct((M, N), F32),
        compiler_params=_params(("parallel", "parallel", "arbitrary"), blk, _nbytes((tm, tn), F32)),
        name=name,
    )(lhs, w, res)


def _conv_kernel(prev_ref, cur_ref, next_ref, w_ref, b_ref, out_ref, buf_ref, *, tm, seq_tiles, width):
    i = pl.program_id(0)
    first = (i % seq_tiles) == 0
    last = (i % seq_tiles) == seq_tiles - 1
    buf_ref[0:HALO, :] = jnp.where(first, 0.0, prev_ref[...].astype(F32))
    buf_ref[HALO:HALO + tm, :] = cur_ref[...].astype(F32)
    buf_ref[HALO + tm:, :] = jnp.where(last, 0.0, next_ref[...].astype(F32))
    half = width // 2
    acc = b_ref[...] + w_ref[0:1, :] * buf_ref[HALO - half:HALO - half + tm, :]
    for k in range(1, width):
        acc = acc + w_ref[k:k + 1, :] * buf_ref[HALO - half + k:HALO - half + k + tm, :]
    out_ref[...] = (acc * _sigmoid(acc)).astype(out_ref.dtype)


def _conv_silu(proj, col_offset, conv_w, conv_b, seq):
    M = proj.shape[0]
    width, C = conv_w.shape
    tm = _pick(seq, (512, 256, 128))
    tc = _pick(C, (2048, 1024, 512, 256, 128))
    while col_offset % tc:
        tc //= 2
    off = col_offset // tc
    hb = tm // HALO
    n_halo = M // HALO
    blk = (2 * _nbytes((tm, tc), BF16) + 2 * _nbytes((HALO, tc), BF16) + _nbytes((width + 1, tc), F32))
    return pl.pallas_call(
        functools.partial(_conv_kernel, tm=tm, seq_tiles=seq // tm, width=width),
        grid=(M // tm, C // tc),
        in_specs=[pl.BlockSpec((HALO, tc), lambda i, j: (jnp.maximum(i * hb - 1, 0), off + j)),
                  pl.BlockSpec((tm, tc), lambda i, j: (i, off + j)),
                  pl.BlockSpec((HALO, tc), lambda i, j: (jnp.minimum((i + 1) * hb, n_halo - 1), off + j)),
                  pl.BlockSpec((width, tc), lambda i, j: (0, j)),
                  pl.BlockSpec((1, tc), lambda i, j: (0, j))],
        out_specs=pl.BlockSpec((tm, tc), lambda i, j: (i, j)),
        out_shape=jax.ShapeDtypeStruct((M, C), BF16),
        scratch_shapes=[pltpu.VMEM((tm + 2 * HALO, tc), F32)],
        compiler_params=_params(("parallel", "parallel"), blk, 4 * _nbytes((tm + 2 * HALO, tc), F32)),
        name="conv_silu",
    )(proj, proj, proj, conv_w, conv_b.reshape(1, C))


def _pool_kernel(win_ref, prev_ref, cur_ref, next_ref, pw_ref, ps_ref, out_ref, *, tm, seq):
    i = pl.program_id(0)
    g = pl.program_id(1)
    w = win_ref[g]
    lo_off = w // 2
    hi_off = w - lo_off
    t0 = (i * tm) % seq

    def band(rows, cols, col_pos0):
        r = lax.broadcasted_iota(jnp.int32, (rows, cols), 0) + t0
        c = lax.broadcasted_iota(jnp.int32, (rows, cols), 1) + (t0 + col_pos0)
        inside = (c >= r - lo_off) & (c < r + hi_off) & (c >= 0) & (c < seq)
        return jnp.where(inside, 1.0, 0.0).astype(BF16)

    cur = cur_ref[...]
    wsum = (_dot(band(tm, tm, 0), cur)
            + _dot(band(tm, HALO, -HALO), prev_ref[...])
            + _dot(band(tm, HALO, tm), next_ref[...]))
    t = lax.broadcasted_iota(jnp.int32, wsum.shape, 0) + t0
    cnt = jnp.minimum(t + hi_off, seq) - jnp.maximum(t - lo_off, 0)
    pooled = (wsum / cnt.astype(F32) - cur.astype(F32)).astype(BF16)
    out_ref[...] = (_dot(pooled, pw_ref[0]) * ps_ref[...]).astype(out_ref.dtype)


def _pool(proj, col_offset, pool_w, pool_scale, seq):
    M = proj.shape[0]
    PG, Wd, _ = pool_w.shape
    assert PG == len(POOL_WINDOWS) and max(POOL_WINDOWS) // 2 <= HALO
    assert col_offset % Wd == 0
    off = col_offset // Wd
    tm = _pick(seq, (256, 128))
    hb = tm // HALO
    n_halo = M // HALO
    windows = jnp.asarray(POOL_WINDOWS, jnp.int32)
    blk = (2 * _nbytes((tm, Wd), BF16) + 2 * _nbytes((HALO, Wd), BF16) + _nbytes((Wd, Wd), BF16)
           + _nbytes((tm, tm), F32))
    return pl.pallas_call(
        functools.partial(_pool_kernel, tm=tm, seq=seq),
        grid=(M // tm, PG),
        in_specs=[pl.BlockSpec(memory_space=pltpu.SMEM),
                  pl.BlockSpec((HALO, Wd), lambda i, g: (jnp.maximum(i * hb - 1, 0), off + g)),
                  pl.BlockSpec((tm, Wd), lambda i, g: (i, off + g)),
                  pl.BlockSpec((HALO, Wd), lambda i, g: (jnp.minimum((i + 1) * hb, n_halo - 1), off + g)),
                  pl.BlockSpec((1, Wd, Wd), lambda i, g: (g, 0, 0)),
                  pl.BlockSpec((1, Wd), lambda i, g: (0, g))],
        out_specs=pl.BlockSpec((tm, Wd), lambda i, g: (i, g)),
        out_shape=jax.ShapeDtypeStruct((M, PG * Wd), BF16),
        compiler_params=_params(("parallel", "arbitrary"), blk, 4 * _nbytes((tm, Wd), F32)),
        name="pool",
    )(windows, proj, proj, proj, pool_w, pool_scale.reshape(1, PG * Wd))


def _cumsum_rows(tri, v):
    a, b, c = _split3(v)
    return _dot(tri, a) + _dot(tri, b) + _dot(tri, c)


def _cumsum_lanes(v, triT):
    a, b, c = _split3(v)
    return _dot(a, triT) + _dot(b, triT) + _dot(c, triT)


def _expand(v, e):
    a, b = _split2(v)
    return _dot(a, e) + _dot(b, e)


def _ssd_state_kernel(xf_ref, bf_ref, dtf_ref, xb_ref, bb_ref, dtb_ref, bias_ref, alog_ref, e_ref,
                      pf_ref, pb_ref, h_ref, *, H, G, N, Q):
    L = SSD_CHUNK

    @pl.when(pl.program_id(1) == 0)
    def _():
        h_ref[...] = jnp.zeros_like(h_ref)

    row = lax.broadcasted_iota(jnp.int32, (L, L), 0)
    col = lax.broadcasted_iota(jnp.int32, (L, L), 1)
    tri = jnp.where(row >= col, 1.0, 0.0).astype(BF16)
    is_fwd = lax.broadcasted_iota(jnp.int32, (L, 2 * H), 1) < H
    A = -jnp.exp(alog_ref[...])

    def direction(d, x_ref, b_ref, dtraw_ref, p_ref):
        dt = _softplus(dtraw_ref[...] + bias_ref[...])
        a = dt * A
        cum = _cumsum_rows(tri, a)
        total = cum[L - 1:L, :]
        dte = jnp.exp(jnp.where(is_fwd, total - cum, cum - a))
        wexp = _expand(dt * dte, e_ref[d])
        xs = (x_ref[...].astype(F32) * wexp).astype(BF16)
        cdexp = _expand(jnp.broadcast_to(jnp.exp(total), (8, 2 * H)), e_ref[d])[0:1, :]
        bm = b_ref[...]
        for g in range(G):
            st = _dot_tn(bm[:, g * N:(g + 1) * N], xs[:, g * Q:(g + 1) * Q])
            h_old = h_ref[d, g]
            p_ref[0, 0, g] = h_old.astype(BF16)
            h_ref[d, g] = cdexp[:, g * Q:(g + 1) * Q] * h_old + st

    direction(0, xf_ref, bf_ref, dtf_ref, pf_ref)
    direction(1, xb_ref, bb_ref, dtb_ref, pb_ref)


def _ssd_states(xbc, dtraw, bias, alog, emat, batch, seq, D, H):
    L, G, N = SSD_CHUNK, SSD_GROUPS, SSD_STATE
    GN = G * N
    Q = D // G
    NC = seq // L
    assert D % GN == 0 or GN % D == 0
    b_off = D // GN
    H2 = 2 * H
    blk = 2 * (_nbytes((L, D), BF16) + _nbytes((L, GN), BF16) + _nbytes((L, H2), F32)
               + _nbytes((G, N, Q), BF16)) + _nbytes((2, H2, D), BF16)
    fwd = lambda b, t: b * NC + t
    bwd = lambda b, t: b * NC + (NC - 1 - t)
    out_sds = jax.ShapeDtypeStruct((batch, NC, G, N, Q), BF16)
    return pl.pallas_call(
        functools.partial(_ssd_state_kernel, H=H, G=G, N=N, Q=Q),
        grid=(batch, NC),
        in_specs=[pl.BlockSpec((L, D), lambda b, t: (fwd(b, t), 0)),
                  pl.BlockSpec((L, GN), lambda b, t: (fwd(b, t), b_off)),
                  pl.BlockSpec((L, H2), lambda b, t: (fwd(b, t), 0)),
                  pl.BlockSpec((L, D), lambda b, t: (bwd(b, t), 0)),
                  pl.BlockSpec((L, GN), lambda b, t: (bwd(b, t), b_off)),
                  pl.BlockSpec((L, H2), lambda b, t: (bwd(b, t), 0)),
                  pl.BlockSpec((1, H2), lambda b, t: (0, 0)),
                  pl.BlockSpec((1, H2), lambda b, t: (0, 0)),
                  pl.BlockSpec((2, H2, D), lambda b, t: (0, 0, 0))],
        out_specs=[pl.BlockSpec((1, 1, G, N, Q), lambda b, t: (b, t, 0, 0, 0)),
                   pl.BlockSpec((1, 1, G, N, Q), lambda b, t: (b, NC - 1 - t, 0, 0, 0))],
        out_shape=[out_sds, out_sds],
        scratch_shapes=[pltpu.VMEM((2, G, N, Q), F32)],
        compiler_params=_params(("parallel", "arbitrary"), blk,
                                _nbytes((2, G, N, Q), F32) + 4 * _nbytes((L, D), F32)),
        name="ssd_states",
    )(xbc, xbc, dtraw, xbc, xbc, dtraw, bias, alog, emat)


def _ssd_out_kernel(dsk_ref, x_ref, b_ref, c_ref, z_ref, dt_ref, dtT_ref, pf_ref, pb_ref,
                    bias_ref, alog_ref, biasT_ref, alogT_ref, ng_ref, e_ref, out_ref,
                    *, H, G, N, Q, P):
    L = SSD_CHUNK
    E = H // G
    row = lax.broadcasted_iota(jnp.int32, (L, L), 0)
    col = lax.broadcasted_iota(jnp.int32, (L, L), 1)
    lower = row >= col
    eye = row == col
    tri = jnp.where(lower, 1.0, 0.0).astype(BF16)
    triT = jnp.where(row <= col, 1.0, 0.0).astype(BF16)
    is_fwd = lax.broadcasted_iota(jnp.int32, (L, 2 * H), 1) < H

    dt = _softplus(dt_ref[...] + bias_ref[...])
    a = dt * (-jnp.exp(alog_ref[...]))
    cum = _cumsum_rows(tri, a)
    total = cum[L - 1:L, :]
    ref_col = jnp.where(is_fwd, cum, cum - a)
    sdec = jnp.exp(jnp.where(is_fwd, cum, total - (cum - a)))
    sf = _expand(sdec, e_ref[0])
    sb = _expand(sdec, e_ref[1])

    dtT = _softplus(dtT_ref[...] + biasT_ref[...])
    aT = dtT * (-jnp.exp(alogT_ref[...]))
    cumT = _cumsum_lanes(aT, triT)
    exclT = cumT - aT

    lane = lax.broadcasted_iota(jnp.int32, (L, 2 * P), 1)
    keep_left = jnp.where(lane < P, 1.0, 0.0).astype(BF16)
    keep_right = jnp.where(lane < P, 0.0, 1.0).astype(BF16)

    x = x_ref[...]
    bm = b_ref[...]
    cm = c_ref[...]
    for g in range(G):
        c_g = cm[:, g * N:(g + 1) * N]
        cb = _dot_nt(c_g, bm[:, g * N:(g + 1) * N])
        pieces = []
        for e2 in range(E // 2):
            ws = []
            for h in (g * E + 2 * e2, g * E + 2 * e2 + 1):
                arg = jnp.where(lower,
                                ref_col[:, h:h + 1] - cumT[h:h + 1, :],
                                exclT[H + h:H + h + 1, :] - ref_col[:, H + h:H + h + 1])
                dec = jnp.exp(arg)
                dtb_row = dtT[H + h:H + h + 1, :]
                dsel = jnp.where(lower, dtT[h:h + 1, :], dtb_row)
                wmat = cb * (dec * dsel + jnp.where(eye, dtb_row, 0.0)) + jnp.where(eye, dsk_ref[h], 0.0)
                ws.append(wmat.astype(BF16))
            h0 = g * E + 2 * e2
            xp = x[:, h0 * P:(h0 + 2) * P]
            rhs = jnp.concatenate([xp * keep_left, xp * keep_right], axis=0)
            pieces.append(_dot(jnp.concatenate(ws, axis=1), rhs))
        y = jnp.concatenate(pieces, axis=1) if len(pieces) > 1 else pieces[0]
        sl = slice(g * Q, (g + 1) * Q)
        y = y + _dot(c_g, pf_ref[0, 0, g]) * sf[:, sl] + _dot(c_g, pb_ref[0, 0, g]) * sb[:, sl]
        zg = z_ref[:, sl].astype(F32)
        y = y * (zg * _sigmoid(zg))
        ms = jnp.mean(y * y, axis=-1, keepdims=True)
        out_ref[:, sl] = (y * lax.rsqrt(ms + EPS) * ng_ref[:, sl]).astype(out_ref.dtype)


def _ssd_out(proj, xbc, dtraw, dtrawT, prev_f, prev_b, bias, alog, biasT, alogT, d_skip, norm_g, emat,
             batch, seq, D, H):
    L, G, N = SSD_CHUNK, SSD_GROUPS, SSD_STATE
    GN = G * N
    Q = D // G
    P = D // H
    NC = seq // L
    M = batch * seq
    H2 = 2 * H
    assert (H // G) % 2 == 0 and 2 * P == 128 and D % GN == 0
    b_off = D // GN
    blk = (2 * _nbytes((L, D), BF16) + 2 * _nbytes((L, GN), BF16) + 2 * _nbytes((L, H2), F32)
           + 2 * _nbytes((G, N, Q), BF16) + _nbytes((2, H2, D), BF16) + _nbytes((L, D), BF16))
    rowblk = lambda b, c: b * NC + c
    return pl.pallas_call(
        functools.partial(_ssd_out_kernel, H=H, G=G, N=N, Q=Q, P=P),
        grid=(batch, NC),
        in_specs=[pl.BlockSpec(memory_space=pltpu.SMEM),
                  pl.BlockSpec((L, D), lambda b, c: (rowblk(b, c), 0)),
                  pl.BlockSpec((L, GN), lambda b, c: (rowblk(b, c), b_off)),
                  pl.BlockSpec((L, GN), lambda b, c: (rowblk(b, c), b_off + 1)),
                  pl.BlockSpec((L, D), lambda b, c: (rowblk(b, c), 0)),
                  pl.BlockSpec((L, H2), lambda b, c: (rowblk(b, c), 0)),
                  pl.BlockSpec((H2, L), lambda b, c: (0, rowblk(b, c))),
                  pl.BlockSpec((1, 1, G, N, Q), lambda b, c: (b, c, 0, 0, 0)),
                  pl.BlockSpec((1, 1, G, N, Q), lambda b, c: (b, c, 0, 0, 0)),
                  pl.BlockSpec((1, H2), lambda b, c: (0, 0)),
                  pl.BlockSpec((1, H2), lambda b, c: (0, 0)),
                  pl.BlockSpec((H2, L), lambda b, c: (0, 0)),
                  pl.BlockSpec((H2, L), lambda b, c: (0, 0)),
                  pl.BlockSpec((1, D), lambda b, c: (0, 0)),
                  pl.BlockSpec((2, H2, D), lambda b, c: (0, 0, 0))],
        out_specs=pl.BlockSpec((L, D), lambda b, c: (rowblk(b, c), 0)),
        out_shape=jax.ShapeDtypeStruct((M, D), BF16),
        compiler_params=_params(("parallel", "parallel"), blk, 8 * _nbytes((L, D), F32)),
        name="ssd_out",
    )(d_skip, xbc, xbc, xbc, proj, dtraw, dtrawT, prev_f, prev_b, bias, alog, biasT, alogT,
      norm_g.reshape(1, D), emat)


def _prepare_layer(norm_mix_g, w_in, conv_w, conv_b, dt_bias_f, dt_bias_b, a_log_f, a_log_b, d_skip,
                   ssd_norm_g, w_ssd_up, pool_w, pool_scale, w_pool_up, w_out, norm_mlp_g, w_ff1, w_ff2,
                   norm_ple_g, w_ple_gate, w_ple_proj):
    D = w_in.shape[0]
    H = d_skip.shape[0]
    C = conv_w.shape[1]
    s_xbc = D + C
    s_dt = s_xbc + 2 * H
    w_dt = w_in[:, s_xbc:s_dt].astype(BF16)
    head_of_lane = jnp.arange(D, dtype=jnp.int32) // (D // H)
    onehot = (jnp.arange(H, dtype=jnp.int32)[:, None] == head_of_lane[None, :]).astype(BF16)
    zeros = jnp.zeros_like(onehot)
    L = SSD_CHUNK
    bias = jnp.concatenate([dt_bias_f, dt_bias_b])
    alog = jnp.concatenate([a_log_f, a_log_b])
    return dict(
        D=D, H=H, C=C,
        norm_mix_g=norm_mix_g,
        w_main=jnp.concatenate([w_in[:, :s_xbc], w_in[:, s_dt:]], axis=1).astype(BF16),
        w_dt=w_dt, w_dtT=w_dt.T,
        conv_w=conv_w, conv_b=conv_b,
        bias=bias.reshape(1, 2 * H), alog=alog.reshape(1, 2 * H),
        biasT=jnp.broadcast_to(bias[:, None], (2 * H, L)), alogT=jnp.broadcast_to(alog[:, None], (2 * H, L)),
        emat=jnp.stack([jnp.concatenate([onehot, zeros]), jnp.concatenate([zeros, onehot])]),
        d_skip=d_skip, ssd_norm_g=ssd_norm_g,
        w_ssd_up=w_ssd_up.astype(BF16),
        pool_w=pool_w.astype(BF16), pool_scale=pool_scale, w_pool_up=w_pool_up.astype(BF16),
        w_out=w_out.astype(BF16), norm_mlp_g=norm_mlp_g,
        w_ff1=w_ff1.astype(BF16), w_ff2=w_ff2.astype(BF16),
        norm_ple_g=norm_ple_g, w_ple_gate=w_ple_gate.astype(BF16), w_ple_proj=w_ple_proj.astype(BF16),
    )


def _layer(x, p_l, lp, batch, seq):
    D, H, C = lp["D"], lp["H"], lp["C"]
    PD = lp["w_pool_up"].shape[0]
    off_xbc, off_pool = D, D + C
    off_ga = off_pool + PD
    off_gb = off_ga + D

    h, dtraw, dtrawT = _norm_dt(x, lp["norm_mix_g"], lp["w_dt"], lp["w_dtT"])
    proj = _matmul(h, lp["w_main"], BF16, "in_proj")

    xbc = _conv_silu(proj, off_xbc, lp["conv_w"], lp["conv_b"], seq)
    prev_f, prev_b = _ssd_states(xbc, dtraw, lp["bias"], lp["alog"], lp["emat"], batch, seq, D, H)
    y = _ssd_out(proj, xbc, dtraw, dtrawT, prev_f, prev_b, lp["bias"], lp["alog"], lp["biasT"],
                 lp["alogT"], lp["d_skip"], lp["ssd_norm_g"], lp["emat"], batch, seq, D, H)
    gated_a = _matmul(y, lp["w_ssd_up"], BF16, "ssd_up",
                      epilogue=lambda acc, g_ref: _sigmoid(g_ref[...].astype(F32)) * acc,
                      extras=(proj,), extra_specs=lambda tm, tn: [_tile_spec(off_ga)(tm, tn)])

    pooled = _pool(proj, off_pool, lp["pool_w"], lp["pool_scale"], seq)
    mixed = _matmul(pooled, lp["w_pool_up"], BF16, "pool_up",
                    epilogue=lambda acc, g_ref, a_ref: (_sigmoid(g_ref[...].astype(F32)) * acc
                                                        + a_ref[...].astype(F32)),
                    extras=(proj, gated_a),
                    extra_specs=lambda tm, tn: [_tile_spec(off_gb)(tm, tn), _tile_spec(0)(tm, tn)])
    x = _matmul(mixed, lp["w_out"], F32, "out_proj",
                epilogue=lambda acc, x_ref: x_ref[...] + acc,
                extras=(x,), extra_specs=lambda tm, tn: [_tile_spec(0)(tm, tn)])

    h2 = _norm(x, lp["norm_mlp_g"], BF16, "norm_mlp")
    ff = _matmul(h2, lp["w_ff1"], BF16, "ff1", epilogue=lambda acc: jnp.square(jnp.maximum(acc, 0.0)))
    x = _matmul_kacc_residual(ff, lp["w_ff2"], x, "ff2")

    h3 = _norm(x, lp["norm_ple_g"], BF16, "norm_ple")
    ple = p_l.shape[1]
    x = _matmul(h3, lp["w_ple_gate"], F32, "ple",
                epilogue=lambda acc, x_ref, p_ref, wp_ref: (
                    x_ref[...] + _sigmoid(acc) * _dot(p_ref[...].astype(BF16), wp_ref[...])),
                extras=(x, p_l, lp["w_ple_proj"]),
                extra_specs=lambda tm, tn: [_tile_spec(0)(tm, tn),
                                            pl.BlockSpec((tm, ple), lambda i, j: (i, 0)),
                                            pl.BlockSpec((ple, tn), lambda i, j: (0, j))])
    return x


def _trunk(x, p, layers, norm_final_g):
    batch, seq, D = x.shape
    assert seq % SSD_CHUNK == 0
    xf = x.reshape(batch * seq, D)
    for i, lp in enumerate(layers):
        xf = _layer(xf, p[i].reshape(batch * seq, -1), lp, batch, seq)
    return _norm(xf, norm_final_g, F32, "norm_final").reshape(batch, seq, D)


def kernel(x_prompt, x_sample, p_prompt, p_sample, norm_mix_g, w_in, conv_w, conv_b, dt_bias_f, dt_bias_b, a_log_f, a_log_b, d_skip, ssd_norm_g, w_ssd_up, pool_w, pool_scale, w_pool_up, w_out, norm_mlp_g, w_ff1, w_ff2, norm_ple_g, w_ple_gate, w_ple_proj, norm_final_g):
    stacked = (norm_mix_g, w_in, conv_w, conv_b, dt_bias_f, dt_bias_b, a_log_f, a_log_b, d_skip,
               ssd_norm_g, w_ssd_up, pool_w, pool_scale, w_pool_up, w_out, norm_mlp_g, w_ff1, w_ff2,
               norm_ple_g, w_ple_gate, w_ple_proj)
    layers = [_prepare_layer(*[w[i] for w in stacked]) for i in range(w_in.shape[0])]
    y_prompt = _trunk(x_prompt, p_prompt, layers, norm_final_g)
    y_sample = _trunk(x_sample, p_sample, layers, norm_final_g)
    return (y_prompt, y_sample)
```

```python
import functools

import jax
import jax.numpy as jnp
from jax import lax
from jax.experimental import pallas as pl
from jax.experimental.pallas import tpu as pltpu

F32 = jnp.float32
BF16 = jnp.bfloat16

EPS = 1e-6
LOG2E = 1.4426950408889634
SSD_GROUPS = 8
SSD_STATE = 128
SSD_CHUNK = 128
POOL_WINDOWS = (2, 4, 8, 16)

V7X_VMEM_BYTES = 64 * 1024 * 1024
V7X_VMEM_RESERVE_BYTES = 8 * 1024 * 1024
BF16_SUBLANE_TILE = 16
LANES = 128
HALO = BF16_SUBLANE_TILE
_TN_WIDE = 1024


def _pick(dim, prefs):
    for p in prefs:
        if dim % p == 0:
            return p
    return dim


def _nbytes(shape, dtype):
    n = 1
    for s in shape:
        n *= s
    return n * jnp.dtype(dtype).itemsize


def _params(semantics, block_bytes, scratch_bytes=0):
    est = 2 * block_bytes + scratch_bytes
    limit = min(2 * est + (4 << 20), V7X_VMEM_BYTES - V7X_VMEM_RESERVE_BYTES)
    return pltpu.CompilerParams(dimension_semantics=semantics, vmem_limit_bytes=int(limit))


def _sigmoid(v):
    return 1.0 / (1.0 + jnp.exp(-v))


def _softplus(v):
    u = jnp.exp(-jnp.abs(v))
    w = 1.0 + u
    return jnp.maximum(v, 0.0) + jnp.where(w == 1.0, u, jnp.log(w) * (u / (w - 1.0)))


def _split3(v):
    a = v.astype(BF16)
    r = v - a.astype(F32)
    b = r.astype(BF16)
    c = (r - b.astype(F32)).astype(BF16)
    return a, b, c


def _split2(v):
    a = v.astype(BF16)
    b = (v - a.astype(F32)).astype(BF16)
    return a, b


def _dot(a, b):
    return jnp.dot(a, b, preferred_element_type=F32)


def _dot_nt(a, b):
    return lax.dot_general(a, b, (((1,), (1,)), ((), ())), preferred_element_type=F32)


def _dot_tn(a, b):
    return lax.dot_general(a, b, (((0,), (0,)), ((), ())), preferred_element_type=F32)


def _rms(x, g):
    ms = jnp.mean(x * x, axis=-1, keepdims=True)
    return x * lax.rsqrt(ms + EPS) * g


def _norm_dt_kernel(x_ref, g_ref, wdt_ref, wdtT_ref, h_ref, dt_ref, dtT_ref):
    h = _rms(x_ref[...], g_ref[...]).astype(BF16)
    h_ref[...] = h
    dt_ref[...] = _dot(h, wdt_ref[...])
    dtT_ref[...] = _dot_nt(wdtT_ref[...], h)


def _norm_dt(x, g, w_dt, w_dtT):
    M, D = x.shape
    H2 = w_dt.shape[1]
    tm = _pick(M, (512, 256, 128))
    blk = (_nbytes((tm, D), F32) + _nbytes((tm, D), BF16) + 2 * _nbytes((D, H2), BF16)
           + 2 * _nbytes((tm, H2), F32))
    return pl.pallas_call(
        _norm_dt_kernel,
        grid=(M // tm,),
        in_specs=[pl.BlockSpec((tm, D), lambda i: (i, 0)),
                  pl.BlockSpec((1, D), lambda i: (0, 0)),
                  pl.BlockSpec((D, H2), lambda i: (0, 0)),
                  pl.BlockSpec((H2, D), lambda i: (0, 0))],
        out_specs=[pl.BlockSpec((tm, D), lambda i: (i, 0)),
                   pl.BlockSpec((tm, H2), lambda i: (i, 0)),
                   pl.BlockSpec((H2, tm), lambda i: (0, i))],
        out_shape=[jax.ShapeDtypeStruct((M, D), BF16),
                   jax.ShapeDtypeStruct((M, H2), F32),
                   jax.ShapeDtypeStruct((H2, M), F32)],
        compiler_params=_params(("parallel",), blk),
        name="norm_dt",
    )(x, g.reshape(1, D), w_dt, w_dtT)


def _norm_kernel(x_ref, g_ref, o_ref):
    o_ref[...] = _rms(x_ref[...], g_ref[...]).astype(o_ref.dtype)


def _norm(x, g, out_dtype, name):
    M, D = x.shape
    tm = _pick(M, (512, 256, 128))
    blk = _nbytes((tm, D), F32) + _nbytes((tm, D), out_dtype)
    return pl.pallas_call(
        _norm_kernel,
        grid=(M // tm,),
        in_specs=[pl.BlockSpec((tm, D), lambda i: (i, 0)),
                  pl.BlockSpec((1, D), lambda i: (0, 0))],
        out_specs=pl.BlockSpec((tm, D), lambda i: (i, 0)),
        out_shape=jax.ShapeDtypeStruct((M, D), out_dtype),
        compiler_params=_params(("parallel",), blk),
        name=name,
    )(x, g.reshape(1, D))


def _row_rsqrt(ssq_ref, d):
    return lax.rsqrt(jnp.sum(ssq_ref[...], axis=-1, keepdims=True) * (1.0 / d) + EPS)


def _lane_partial_sumsq(v):
    sq = v * v
    part = sq[:, 0:LANES]
    for q in range(1, v.shape[1] // LANES):
        part = part + sq[:, q * LANES:(q + 1) * LANES]
    return part


def _mm_kernel(lhs_ref, w_ref, *rest, epilogue, norm_width):
    out_ref = rest[-1]
    acc = _dot(lhs_ref[...], w_ref[...])
    if norm_width:
        acc = acc * _row_rsqrt(rest[0], norm_width)
        rest = rest[1:]
    out_ref[...] = epilogue(acc, *rest[:-1]).astype(out_ref.dtype)


def _matmul(lhs, w, out_dtype, name, epilogue=None, extras=(), extra_specs=None, ssq=None, tm=None, tn=None):
    M, K = lhs.shape
    N = w.shape[1]
    tm = tm or _pick(M, (1024, 512, 256, 128))
    tn = _pick(N, ((tn,) if tn else ()) + (512, 256, 128))
    if epilogue is None:
        epilogue = lambda acc: acc
    specs = extra_specs(tm, tn) if extra_specs else []
    if ssq is not None:
        extras = (ssq,) + tuple(extras)
        specs = [pl.BlockSpec((tm, LANES), lambda i, j: (i, 0))] + specs
    blk = (_nbytes((tm, K), lhs.dtype) + _nbytes((K, tn), w.dtype) + _nbytes((tm, tn), out_dtype)
           + sum(_nbytes(s.block_shape, e.dtype) for s, e in zip(specs, extras)))
    return pl.pallas_call(
        functools.partial(_mm_kernel, epilogue=epilogue, norm_width=K if ssq is not None else 0),
        grid=(M // tm, N // tn),
        in_specs=[pl.BlockSpec((tm, K), lambda i, j: (i, 0)),
                  pl.BlockSpec((K, tn), lambda i, j: (0, j))] + specs,
        out_specs=pl.BlockSpec((tm, tn), lambda i, j: (i, j)),
        out_shape=jax.ShapeDtypeStruct((M, N), out_dtype),
        compiler_params=_params(("parallel", "arbitrary"), blk, _nbytes((tm, tn), F32)),
        name=name,
    )(lhs, w, *extras)


def _tile_spec(col_offset):
    def make(tm, tn):
        assert col_offset % tn == 0
        off = col_offset // tn
        return pl.BlockSpec((tm, tn), lambda i, j: (i, off + j))
    return make


def _emit_norm_operand(x_new, g_ref, xg_ref, ssq_ref, first):
    xg_ref[...] = (x_new * g_ref[...]).astype(xg_ref.dtype)
    part = _lane_partial_sumsq(x_new)

    @pl.when(first)
    def _():
        ssq_ref[...] = part

    @pl.when(jnp.logical_not(first))
    def _():
        ssq_ref[...] += part


def _mm_res_norm_kernel(lhs_ref, w_ref, res_ref, g_ref, x_ref, xg_ref, ssq_ref):
    x_new = res_ref[...] + _dot(lhs_ref[...], w_ref[...])
    x_ref[...] = x_new
    _emit_norm_operand(x_new, g_ref, xg_ref, ssq_ref, pl.program_id(1) == 0)


def _matmul_residual_norm(lhs, w, res, g_next, name):
    M, K = lhs.shape
    N = w.shape[1]
    tm = _pick(M, (1024, 512, 256, 128))
    tn = _pick(N, (512, 256, 128))
    blk = (_nbytes((tm, K), lhs.dtype) + _nbytes((K, tn), w.dtype) + 2 * _nbytes((tm, tn), F32)
           + _nbytes((tm, tn), BF16) + _nbytes((tm, LANES), F32))
    return pl.pallas_call(
        _mm_res_norm_kernel,
        grid=(M // tm, N // tn),
        in_specs=[pl.BlockSpec((tm, K), lambda i, j: (i, 0)),
                  pl.BlockSpec((K, tn), lambda i, j: (0, j)),
                  pl.BlockSpec((tm, tn), lambda i, j: (i, j)),
                  pl.BlockSpec((1, tn), lambda i, j: (0, j))],
        out_specs=[pl.BlockSpec((tm, tn), lambda i, j: (i, j)),
                   pl.BlockSpec((tm, tn), lambda i, j: (i, j)),
                   pl.BlockSpec((tm, LANES), lambda i, j: (i, 0))],
        out_shape=[jax.ShapeDtypeStruct((M, N), F32),
                   jax.ShapeDtypeStruct((M, N), BF16),
                   jax.ShapeDtypeStruct((M, LANES), F32)],
        compiler_params=_params(("parallel", "arbitrary"), blk, _nbytes((tm, tn), F32)),
        name=name,
    )(lhs, w, res, g_next.reshape(1, N))


def _mm_kacc_kernel(lhs_ref, w_ref, res_ref, g_ref, x_ref, xg_ref, ssq_ref):
    k = pl.program_id(2)

    @pl.when(k == 0)
    def _():
        x_ref[...] = res_ref[...]

    x_ref[...] += _dot(lhs_ref[...], w_ref[...])

    @pl.when(k == pl.num_programs(2) - 1)
    def _():
        _emit_norm_operand(x_ref[...], g_ref, xg_ref, ssq_ref, pl.program_id(1) == 0)


def _matmul_kacc_residual_norm(lhs, w, res, g_next, name):
    M, K = lhs.shape
    N = w.shape[1]
    tm = _pick(M, (1024, 512, 256, 128))
    tn = _pick(N, (1024, 512, 256, 128))
    tk = _pick(K, (2048, 1024, 512, 256, 128))
    blk = (_nbytes((tm, tk), lhs.dtype) + _nbytes((tk, tn), w.dtype) + 2 * _nbytes((tm, tn), F32)
           + _nbytes((tm, tn), BF16) + _nbytes((tm, LANES), F32))
    return pl.pallas_call(
        _mm_kacc_kernel,
        grid=(M // tm, N // tn, K // tk),
        in_specs=[pl.BlockSpec((tm, tk), lambda i, j, k: (i, k)),
                  pl.BlockSpec((tk, tn), lambda i, j, k: (k, j)),
                  pl.BlockSpec((tm, tn), lambda i, j, k: (i, j)),
                  pl.BlockSpec((1, tn), lambda i, j, k: (0, j))],
        out_specs=[pl.BlockSpec((tm, tn), lambda i, j, k: (i, j)),
                   pl.BlockSpec((tm, tn), lambda i, j, k: (i, j)),
                   pl.BlockSpec((tm, LANES), lambda i, j, k: (i, 0))],
        out_shape=[jax.ShapeDtypeStruct((M, N), F32),
                   jax.ShapeDtypeStruct((M, N), BF16),
                   jax.ShapeDtypeStruct((M, LANES), F32)],
        compiler_params=_params(("parallel", "arbitrary", "arbitrary"), blk, _nbytes((tm, tn), F32)),
        name=name,
    )(lhs, w, res, g_next.reshape(1, N))


def _conv_kernel(prev_ref, cur_ref, next_ref, w_ref, b_ref, out_ref, buf_ref, *, tm, seq_tiles, width):
    i = pl.program_id(0)
    first = (i % seq_tiles) == 0
    last = (i % seq_tiles) == seq_tiles - 1
    buf_ref[0:HALO, :] = jnp.where(first, 0.0, prev_ref[...].astype(F32))
    buf_ref[HALO:HALO + tm, :] = cur_ref[...].astype(F32)
    buf_ref[HALO + tm:, :] = jnp.where(last, 0.0, next_ref[...].astype(F32))
    half = width // 2
    acc = b_ref[...] + w_ref[0:1, :] * buf_ref[HALO - half:HALO - half + tm, :]
    for k in range(1, width):
        acc = acc + w_ref[k:k + 1, :] * buf_ref[HALO - half + k:HALO - half + k + tm, :]
    out_ref[...] = (acc * _sigmoid(acc)).astype(out_ref.dtype)


def _conv_silu(proj, col_offset, conv_w, conv_b, seq):
    M = proj.shape[0]
    width, C = conv_w.shape
    tm = _pick(seq, (512, 256, 128))
    tc = _pick(C, (2048, 1024, 512, 256, 128))
    while col_offset % tc:
        tc //= 2
    off = col_offset // tc
    hb = tm // HALO
    n_halo = M // HALO
    blk = (2 * _nbytes((tm, tc), BF16) + 2 * _nbytes((HALO, tc), BF16) + _nbytes((width + 1, tc), F32))
    return pl.pallas_call(
        functools.partial(_conv_kernel, tm=tm, seq_tiles=seq // tm, width=width),
        grid=(M // tm, C // tc),
        in_specs=[pl.BlockSpec((HALO, tc), lambda i, j: (jnp.maximum(i * hb - 1, 0), off + j)),
                  pl.BlockSpec((tm, tc), lambda i, j: (i, off + j)),
                  pl.BlockSpec((HALO, tc), lambda i, j: (jnp.minimum((i + 1) * hb, n_halo - 1), off + j)),
                  pl.BlockSpec((width, tc), lambda i, j: (0, j)),
                  pl.BlockSpec((1, tc), lambda i, j: (0, j))],
        out_specs=pl.BlockSpec((tm, tc), lambda i, j: (i, j)),
        out_shape=jax.ShapeDtypeStruct((M, C), BF16),
        scratch_shapes=[pltpu.VMEM((tm + 2 * HALO, tc), F32)],
        compiler_params=_params(("parallel", "parallel"), blk, 4 * _nbytes((tm + 2 * HALO, tc), F32)),
        name="conv_silu",
    )(proj, proj, proj, conv_w, conv_b.reshape(1, C))


def _pool_kernel(win_ref, prev_ref, cur_ref, next_ref, pw_ref, ps_ref, out_ref, *, tm, seq):
    i = pl.program_id(0)
    g = pl.program_id(1)
    w = win_ref[g]
    lo_off = w // 2
    hi_off = w - lo_off
    t0 = (i * tm) % seq

    def band(rows, cols, col_pos0):
        r = lax.broadcasted_iota(jnp.int32, (rows, cols), 0) + t0
        c = lax.broadcasted_iota(jnp.int32, (rows, cols), 1) + (t0 + col_pos0)
        inside = (c >= r - lo_off) & (c < r + hi_off) & (c >= 0) & (c < seq)
        return jnp.where(inside, 1.0, 0.0).astype(BF16)

    cur = cur_ref[...]
    wsum = (_dot(band(tm, tm, 0), cur)
            + _dot(band(tm, HALO, -HALO), prev_ref[...])
            + _dot(band(tm, HALO, tm), next_ref[...]))
    t = lax.broadcasted_iota(jnp.int32, wsum.shape, 0) + t0
    cnt = jnp.minimum(t + hi_off, seq) - jnp.maximum(t - lo_off, 0)
    pooled = (wsum / cnt.astype(F32) - cur.astype(F32)).astype(BF16)
    out_ref[...] = (_dot(pooled, pw_ref[0]) * ps_ref[...]).astype(out_ref.dtype)


def _pool(proj, col_offset, pool_w, pool_scale, seq):
    M = proj.shape[0]
    PG, Wd, _ = pool_w.shape
    assert PG == len(POOL_WINDOWS) and max(POOL_WINDOWS) // 2 <= HALO
    assert col_offset % Wd == 0
    off = col_offset // Wd
    tm = _pick(seq, (256, 128))
    hb = tm // HALO
    n_halo = M // HALO
    windows = jnp.asarray(POOL_WINDOWS, jnp.int32)
    blk = (2 * _nbytes((tm, Wd), BF16) + 2 * _nbytes((HALO, Wd), BF16) + _nbytes((Wd, Wd), BF16)
           + _nbytes((tm, tm), F32))
    return pl.pallas_call(
        functools.partial(_pool_kernel, tm=tm, seq=seq),
        grid=(M // tm, PG),
        in_specs=[pl.BlockSpec(memory_space=pltpu.SMEM),
                  pl.BlockSpec((HALO, Wd), lambda i, g: (jnp.maximum(i * hb - 1, 0), off + g)),
                  pl.BlockSpec((tm, Wd), lambda i, g: (i, off + g)),
                  pl.BlockSpec((HALO, Wd), lambda i, g: (jnp.minimum((i + 1) * hb, n_halo - 1), off + g)),
                  pl.BlockSpec((1, Wd, Wd), lambda i, g: (g, 0, 0)),
                  pl.BlockSpec((1, Wd), lambda i, g: (0, g))],
        out_specs=pl.BlockSpec((tm, Wd), lambda i, g: (i, g)),
        out_shape=jax.ShapeDtypeStruct((M, PG * Wd), BF16),
        compiler_params=_params(("parallel", "arbitrary"), blk, 4 * _nbytes((tm, Wd), F32)),
        name="pool",
    )(windows, proj, proj, proj, pool_w, pool_scale.reshape(1, PG * Wd))


def _cumsum_rows(tri, v):
    a, b, c = _split3(v)
    return _dot(tri, a) + _dot(tri, b) + _dot(tri, c)


def _cumsum_lanes(v, triT):
    a, b, c = _split3(v)
    return _dot(a, triT) + _dot(b, triT) + _dot(c, triT)


def _hilo(v):
    a, b = _split2(v)
    return jnp.concatenate([a, b], axis=1)


def _expand(v_hilo, e2):
    return _dot(v_hilo, e2)


def _ssd_state_kernel(xf_ref, bf_ref, dtf_ref, xb_ref, bb_ref, dtb_ref, bias_ref, alog_ref, e_ref,
                      pf_ref, pb_ref, h_ref, *, H, G, N, Q):
    L = SSD_CHUNK

    @pl.when(pl.program_id(1) == 0)
    def _():
        h_ref[...] = jnp.zeros_like(h_ref)

    row = lax.broadcasted_iota(jnp.int32, (L, L), 0)
    col = lax.broadcasted_iota(jnp.int32, (L, L), 1)
    tri = jnp.where(row >= col, 1.0, 0.0).astype(BF16)
    is_fwd = lax.broadcasted_iota(jnp.int32, (L, 2 * H), 1) < H
    A = -jnp.exp(alog_ref[...])

    def direction(d, x_ref, b_ref, dtraw_ref, p_ref):
        dt = _softplus(dtraw_ref[...] + bias_ref[...])
        a = dt * A
        cum = _cumsum_rows(tri, a)
        total = cum[L - 1:L, :]
        dte = jnp.exp(jnp.where(is_fwd, total - cum, cum - a))
        wexp = _expand(_hilo(dt * dte), e_ref[d])
        xs = (x_ref[...].astype(F32) * wexp).astype(BF16)
        cdexp = _expand(_hilo(jnp.broadcast_to(jnp.exp(total), (8, 2 * H))), e_ref[d])[0:1, :]
        bm = b_ref[...]
        for g in range(G):
            st = _dot_tn(bm[:, g * N:(g + 1) * N], xs[:, g * Q:(g + 1) * Q])
            h_old = h_ref[d, g]
            p_ref[0, 0, g] = h_old.astype(BF16)
            h_ref[d, g] = cdexp[:, g * Q:(g + 1) * Q] * h_old + st

    direction(0, xf_ref, bf_ref, dtf_ref, pf_ref)
    direction(1, xb_ref, bb_ref, dtb_ref, pb_ref)


def _ssd_states(xbc, dtraw, bias, alog, emat, batch, seq, D, H):
    L, G, N = SSD_CHUNK, SSD_GROUPS, SSD_STATE
    GN = G * N
    Q = D // G
    NC = seq // L
    assert D % GN == 0 or GN % D == 0
    b_off = D // GN
    H2 = 2 * H
    blk = 2 * (_nbytes((L, D), BF16) + _nbytes((L, GN), BF16) + _nbytes((L, H2), F32)
               + _nbytes((G, N, Q), BF16)) + _nbytes((2, 2 * H2, D), BF16)
    fwd = lambda b, t: b * NC + t
    bwd = lambda b, t: b * NC + (NC - 1 - t)
    out_sds = jax.ShapeDtypeStruct((batch, NC, G, N, Q), BF16)
    return pl.pallas_call(
        functools.partial(_ssd_state_kernel, H=H, G=G, N=N, Q=Q),
        grid=(batch, NC),
        in_specs=[pl.BlockSpec((L, D), lambda b, t: (fwd(b, t), 0)),
                  pl.BlockSpec((L, GN), lambda b, t: (fwd(b, t), b_off)),
                  pl.BlockSpec((L, H2), lambda b, t: (fwd(b, t), 0)),
                  pl.BlockSpec((L, D), lambda b, t: (bwd(b, t), 0)),
                  pl.BlockSpec((L, GN), lambda b, t: (bwd(b, t), b_off)),
                  pl.BlockSpec((L, H2), lambda b, t: (bwd(b, t), 0)),
                  pl.BlockSpec((1, H2), lambda b, t: (0, 0)),
                  pl.BlockSpec((1, H2), lambda b, t: (0, 0)),
                  pl.BlockSpec((2, 2 * H2, D), lambda b, t: (0, 0, 0))],
        out_specs=[pl.BlockSpec((1, 1, G, N, Q), lambda b, t: (b, t, 0, 0, 0)),
                   pl.BlockSpec((1, 1, G, N, Q), lambda b, t: (b, NC - 1 - t, 0, 0, 0))],
        out_shape=[out_sds, out_sds],
        scratch_shapes=[pltpu.VMEM((2, G, N, Q), F32)],
        compiler_params=_params(("parallel", "arbitrary"), blk,
                                _nbytes((2, G, N, Q), F32) + 4 * _nbytes((L, D), F32)),
        name="ssd_states",
    )(xbc, xbc, dtraw, xbc, xbc, dtraw, bias, alog, emat)


def _ssd_out_kernel(dsk_ref, x_ref, b_ref, c_ref, zs_ref, dt_ref, dtT_ref, pf_ref, pb_ref,
                    bias_ref, alog_ref, biasT_ref, alogT_ref, ng_ref, e_ref, out_ref,
                    *, H, G, N, Q, P):
    L = SSD_CHUNK
    E = H // G
    row = lax.broadcasted_iota(jnp.int32, (L, L), 0)
    col = lax.broadcasted_iota(jnp.int32, (L, L), 1)
    lower = row >= col
    eye = row == col
    tri = jnp.where(lower, 1.0, 0.0).astype(BF16)
    triT = jnp.where(row <= col, 1.0, 0.0).astype(BF16)
    is_fwd = lax.broadcasted_iota(jnp.int32, (L, 2 * H), 1) < H
    is_fwdT = lax.broadcasted_iota(jnp.int32, (2 * H, L), 0) < H

    dt = _softplus(dt_ref[...] + bias_ref[...])
    a = dt * (-jnp.exp(alog_ref[...]))
    cum = _cumsum_rows(tri, a)
    total = cum[L - 1:L, :]
    excl = cum - a
    col2 = jnp.where(is_fwd, cum, excl) * LOG2E
    sdec = _hilo(jnp.exp(jnp.where(is_fwd, cum, total - excl)))

    dtT = _softplus(dtT_ref[...] + biasT_ref[...])
    aT = dtT * (-jnp.exp(alogT_ref[...]))
    cumT = _cumsum_lanes(aT, triT)
    lg = jnp.log2(dtT)
    row2 = jnp.where(is_fwdT, cumT * LOG2E - lg, (cumT - aT) * LOG2E + lg)

    lane = lax.broadcasted_iota(jnp.int32, (L, 2 * P), 1)
    keep_left = jnp.where(lane < P, 1.0, 0.0).astype(BF16)
    keep_right = jnp.where(lane < P, 0.0, 1.0).astype(BF16)

    for g in range(G):
        c_g = c_ref[:, g * N:(g + 1) * N]
        cb = _dot_nt(c_g, b_ref[:, g * N:(g + 1) * N])
        cb_diag = jnp.sum(jnp.where(eye, cb, 0.0), axis=0, keepdims=True)
        pieces = []
        for e2 in range(E // 2):
            ws = []
            for h in (g * E + 2 * e2, g * E + 2 * e2 + 1):
                arg = jnp.where(lower,
                                col2[:, h:h + 1] - row2[h:h + 1, :],
                                row2[H + h:H + h + 1, :] - col2[:, H + h:H + h + 1])
                diag = cb_diag * dtT[H + h:H + h + 1, :] + dsk_ref[h]
                ws.append((cb * jnp.exp2(arg) + jnp.where(eye, diag, 0.0)).astype(BF16))
            h0 = g * E + 2 * e2
            xp = x_ref[:, h0 * P:(h0 + 2) * P]
            rhs = jnp.concatenate([xp * keep_left, xp * keep_right], axis=0)
            pieces.append(_dot(jnp.concatenate(ws, axis=1), rhs))
        y = jnp.concatenate(pieces, axis=1) if len(pieces) > 1 else pieces[0]
        sl = slice(g * Q, (g + 1) * Q)
        y = (y + _dot(c_g, pf_ref[0, 0, g]) * _expand(sdec, e_ref[0, :, sl])
             + _dot(c_g, pb_ref[0, 0, g]) * _expand(sdec, e_ref[1, :, sl]))
        y = y * zs_ref[:, sl].astype(F32)
        ms = jnp.mean(y * y, axis=-1, keepdims=True)
        out_ref[:, sl] = (y * lax.rsqrt(ms + EPS) * ng_ref[:, sl]).astype(out_ref.dtype)


def _ssd_out(zs, xbc, dtraw, dtrawT, prev_f, prev_b, bias, alog, biasT, alogT, d_skip, norm_g, emat,
             batch, seq, D, H):
    L, G, N = SSD_CHUNK, SSD_GROUPS, SSD_STATE
    GN = G * N
    Q = D // G
    P = D // H
    NC = seq // L
    M = batch * seq
    H2 = 2 * H
    assert (H // G) % 2 == 0 and 2 * P == 128 and D % GN == 0
    b_off = D // GN
    blk = (2 * _nbytes((L, D), BF16) + 2 * _nbytes((L, GN), BF16) + 2 * _nbytes((L, H2), F32)
           + 2 * _nbytes((G, N, Q), BF16) + _nbytes((2, 2 * H2, D), BF16) + _nbytes((L, D), BF16))
    rowblk = lambda b, c: b * NC + c
    return pl.pallas_call(
        functools.partial(_ssd_out_kernel, H=H, G=G, N=N, Q=Q, P=P),
        grid=(batch, NC),
        in_specs=[pl.BlockSpec(memory_space=pltpu.SMEM),
                  pl.BlockSpec((L, D), lambda b, c: (rowblk(b, c), 0)),
                  pl.BlockSpec((L, GN), lambda b, c: (rowblk(b, c), b_off)),
                  pl.BlockSpec((L, GN), lambda b, c: (rowblk(b, c), b_off + 1)),
                  pl.BlockSpec((L, D), lambda b, c: (rowblk(b, c), 0)),
                  pl.BlockSpec((L, H2), lambda b, c: (rowblk(b, c), 0)),
                  pl.BlockSpec((H2, L), lambda b, c: (0, rowblk(b, c))),
                  pl.BlockSpec((1, 1, G, N, Q), lambda b, c: (b, c, 0, 0, 0)),
                  pl.BlockSpec((1, 1, G, N, Q), lambda b, c: (b, c, 0, 0, 0)),
                  pl.BlockSpec((1, H2), lambda b, c: (0, 0)),
                  pl.BlockSpec((1, H2), lambda b, c: (0, 0)),
                  pl.BlockSpec((H2, L), lambda b, c: (0, 0)),
                  pl.BlockSpec((H2, L), lambda b, c: (0, 0)),
                  pl.BlockSpec((1, D), lambda b, c: (0, 0)),
                  pl.BlockSpec((2, 2 * H2, D), lambda b, c: (0, 0, 0))],
        out_specs=pl.BlockSpec((L, D), lambda b, c: (rowblk(b, c), 0)),
        out_shape=jax.ShapeDtypeStruct((M, D), BF16),
        compiler_params=_params(("parallel", "parallel"), blk, 8 * _nbytes((L, D), F32)),
        name="ssd_out",
    )(d_skip, xbc, xbc, xbc, zs, dtraw, dtrawT, prev_f, prev_b, bias, alog, biasT, alogT,
      norm_g.reshape(1, D), emat)


def _prepare_layer(norm_mix_g, w_in, conv_w, conv_b, dt_bias_f, dt_bias_b, a_log_f, a_log_b, d_skip,
                   ssd_norm_g, w_ssd_up, pool_w, pool_scale, w_pool_up, w_out, norm_mlp_g, w_ff1, w_ff2,
                   norm_ple_g, w_ple_gate, w_ple_proj):
    D = w_in.shape[0]
    H = d_skip.shape[0]
    C = conv_w.shape[1]
    s_xbc = D + C
    s_dt = s_xbc + 2 * H
    w_dt = w_in[:, s_xbc:s_dt].astype(BF16)
    head_of_lane = jnp.arange(D, dtype=jnp.int32) // (D // H)
    onehot = (jnp.arange(H, dtype=jnp.int32)[:, None] == head_of_lane[None, :]).astype(BF16)
    zeros = jnp.zeros_like(onehot)
    L = SSD_CHUNK
    bias = jnp.concatenate([dt_bias_f, dt_bias_b])
    alog = jnp.concatenate([a_log_f, a_log_b])
    return dict(
        D=D, H=H, C=C,
        norm_mix_g=norm_mix_g,
        w_z=w_in[:, :D].astype(BF16), w_xbc=w_in[:, D:s_xbc].astype(BF16),
        w_tail=w_in[:, s_dt:].astype(BF16),
        w_dt=w_dt, w_dtT=w_dt.T,
        conv_w=conv_w, conv_b=conv_b,
        bias=bias.reshape(1, 2 * H), alog=alog.reshape(1, 2 * H),
        biasT=jnp.broadcast_to(bias[:, None], (2 * H, L)), alogT=jnp.broadcast_to(alog[:, None], (2 * H, L)),
        emat=jnp.stack([jnp.concatenate([onehot, zeros, onehot, zeros]),
                        jnp.concatenate([zeros, onehot, zeros, onehot])]),
        d_skip=d_skip, ssd_norm_g=ssd_norm_g,
        w_ssd_up=w_ssd_up.astype(BF16),
        pool_w=pool_w.astype(BF16), pool_scale=pool_scale, w_pool_up=w_pool_up.astype(BF16),
        w_out=w_out.astype(BF16), norm_mlp_g=norm_mlp_g,
        w_ff1=w_ff1.astype(BF16), w_ff2=w_ff2.astype(BF16),
        norm_ple_g=norm_ple_g, w_ple_gate=w_ple_gate.astype(BF16), w_ple_proj=w_ple_proj.astype(BF16),
    )


def _layer(x, p_l, lp, batch, seq):
    D, H = lp["D"], lp["H"]
    PD = lp["w_pool_up"].shape[0]
    off_ga, off_gb = PD, PD + D

    h, dtraw, dtrawT = _norm_dt(x, lp["norm_mix_g"], lp["w_dt"], lp["w_dtT"])
    zs = _matmul(h, lp["w_z"], BF16, "in_proj_z", epilogue=lambda acc: acc * _sigmoid(acc), tn=_TN_WIDE)
    xbc_raw = _matmul(h, lp["w_xbc"], BF16, "in_proj_xbc", tn=_TN_WIDE)
    tail = _matmul(h, lp["w_tail"], BF16, "in_proj_tail", tn=_TN_WIDE)

    xbc = _conv_silu(xbc_raw, 0, lp["conv_w"], lp["conv_b"], seq)
    prev_f, prev_b = _ssd_states(xbc, dtraw, lp["bias"], lp["alog"], lp["emat"], batch, seq, D, H)
    y = _ssd_out(zs, xbc, dtraw, dtrawT, prev_f, prev_b, lp["bias"], lp["alog"], lp["biasT"],
                 lp["alogT"], lp["d_skip"], lp["ssd_norm_g"], lp["emat"], batch, seq, D, H)
    gated_a = _matmul(y, lp["w_ssd_up"], BF16, "ssd_up",
                      epilogue=lambda acc, g_ref: _sigmoid(g_ref[...].astype(F32)) * acc,
                      extras=(tail,), extra_specs=lambda tm, tn: [_tile_spec(off_ga)(tm, tn)])

    pooled = _pool(tail, 0, lp["pool_w"], lp["pool_scale"], seq)
    mixed = _matmul(pooled, lp["w_pool_up"], BF16, "pool_up",
                    epilogue=lambda acc, g_ref, a_ref: (_sigmoid(g_ref[...].astype(F32)) * acc
                                                        + a_ref[...].astype(F32)),
                    extras=(tail, gated_a),
                    extra_specs=lambda tm, tn: [_tile_spec(off_gb)(tm, tn), _tile_spec(0)(tm, tn)])
    x, xg, ssq = _matmul_residual_norm(mixed, lp["w_out"], x, lp["norm_mlp_g"], "out_proj")

    ff = _matmul(xg, lp["w_ff1"], BF16, "ff1", ssq=ssq, tn=_TN_WIDE,
                 epilogue=lambda acc: jnp.square(jnp.maximum(acc, 0.0)))
    x, xg, ssq = _matmul_kacc_residual_norm(ff, lp["w_ff2"], x, lp["norm_ple_g"], "ff2")

    ple = p_l.shape[1]
    x = _matmul(xg, lp["w_ple_gate"], F32, "ple", ssq=ssq,
                epilogue=lambda acc, x_ref, p_ref, wp_ref: (
                    x_ref[...] + _sigmoid(acc) * _dot(p_ref[...].astype(BF16), wp_ref[...])),
                extras=(x, p_l, lp["w_ple_proj"]),
                extra_specs=lambda tm, tn: [_tile_spec(0)(tm, tn),
                                            pl.BlockSpec((tm, ple), lambda i, j: (i, 0)),
                                            pl.BlockSpec((ple, tn), lambda i, j: (0, j))])
    return x


def _trunk(x, p, layers, norm_final_g):
    batch, seq, D = x.shape
    assert seq % SSD_CHUNK == 0
    xf = x.reshape(batch * seq, D)
    for i, lp in enumerate(layers):
        xf = _layer(xf, p[i].reshape(batch * seq, -1), lp, batch, seq)
    return _norm(xf, norm_final_g, F32, "norm_final").reshape(batch, seq, D)


def kernel(x_prompt, x_sample, p_prompt, p_sample, norm_mix_g, w_in, conv_w, conv_b, dt_bias_f, dt_bias_b, a_log_f, a_log_b, d_skip, ssd_norm_g, w_ssd_up, pool_w, pool_scale, w_pool_up, w_out, norm_mlp_g, w_ff1, w_ff2, norm_ple_g, w_ple_gate, w_ple_proj, norm_final_g):
    stacked = (norm_mix_g, w_in, conv_w, conv_b, dt_bias_f, dt_bias_b, a_log_f, a_log_b, d_skip,
               ssd_norm_g, w_ssd_up, pool_w, pool_scale, w_pool_up, w_out, norm_mlp_g, w_ff1, w_ff2,
               norm_ple_g, w_ple_gate, w_ple_proj)
    layers = [_prepare_layer(*[w[i] for w in stacked]) for i in range(w_in.shape[0])]
    y_prompt = _trunk(x_prompt, p_prompt, layers, norm_final_g)
    y_sample = _trunk(x_sample, p_sample, layers, norm_final_g)
    return (y_prompt, y_sample)
```

```python
import functools

import jax
import jax.numpy as jnp
from jax import lax
from jax.experimental import pallas as pl
from jax.experimental.pallas import tpu as pltpu

F32 = jnp.float32
BF16 = jnp.bfloat16

EPS = 1e-6
LOG2E = 1.4426950408889634
LN2 = 0.6931471805599453
SSD_GROUPS = 8
SSD_STATE = 128
SSD_CHUNK = 128
POOL_WINDOWS = (2, 4, 8, 16)

V7X_VMEM_BYTES = 64 * 1024 * 1024
V7X_VMEM_RESERVE_BYTES = 8 * 1024 * 1024
BF16_SUBLANE_TILE = 16
LANES = 128
HALO = BF16_SUBLANE_TILE
_TN_WIDE = 1024
CONV_STRIP_ROWS = 64


def _pick(dim, prefs):
    for p in prefs:
        if dim % p == 0:
            return p
    return dim


def _nbytes(shape, dtype):
    n = 1
    for s in shape:
        n *= s
    return n * jnp.dtype(dtype).itemsize


def _params(semantics, block_bytes, scratch_bytes=0):
    est = 2 * block_bytes + scratch_bytes
    limit = min(2 * est + (4 << 20), V7X_VMEM_BYTES - V7X_VMEM_RESERVE_BYTES)
    return pltpu.CompilerParams(dimension_semantics=semantics, vmem_limit_bytes=int(limit))


def _sigmoid(v):
    return 1.0 / (1.0 + jnp.exp(-v))


def _softplus(v):
    u = jnp.exp(-jnp.abs(v))
    w = 1.0 + u
    return jnp.maximum(v, 0.0) + jnp.where(w == 1.0, u, (jnp.log2(w) * LN2) * (u / (w - 1.0)))


def _split3(v):
    a = v.astype(BF16)
    r = v - a.astype(F32)
    b = r.astype(BF16)
    c = (r - b.astype(F32)).astype(BF16)
    return a, b, c


def _split2(v):
    a = v.astype(BF16)
    b = (v - a.astype(F32)).astype(BF16)
    return a, b


def _dot(a, b):
    return jnp.dot(a, b, preferred_element_type=F32)


def _dot_nt(a, b):
    return lax.dot_general(a, b, (((1,), (1,)), ((), ())), preferred_element_type=F32)


def _dot_tn(a, b):
    return lax.dot_general(a, b, (((0,), (0,)), ((), ())), preferred_element_type=F32)


def _rms(x, g):
    ms = jnp.mean(x * x, axis=-1, keepdims=True)
    return x * lax.rsqrt(ms + EPS) * g


def _norm_dt_kernel(x_ref, g_ref, wdt_ref, wdtT_ref, h_ref, dt_ref, dtT_ref):
    h = _rms(x_ref[...], g_ref[...]).astype(BF16)
    h_ref[...] = h
    dt_ref[...] = _dot(h, wdt_ref[...])
    dtT_ref[...] = _dot_nt(wdtT_ref[...], h)


def _norm_dt(x, g, w_dt, w_dtT):
    M, D = x.shape
    H2 = w_dt.shape[1]
    tm = _pick(M, (512, 256, 128))
    blk = (_nbytes((tm, D), F32) + _nbytes((tm, D), BF16) + 2 * _nbytes((D, H2), BF16)
           + 2 * _nbytes((tm, H2), F32))
    return pl.pallas_call(
        _norm_dt_kernel,
        grid=(M // tm,),
        in_specs=[pl.BlockSpec((tm, D), lambda i: (i, 0)),
                  pl.BlockSpec((1, D), lambda i: (0, 0)),
                  pl.BlockSpec((D, H2), lambda i: (0, 0)),
                  pl.BlockSpec((H2, D), lambda i: (0, 0))],
        out_specs=[pl.BlockSpec((tm, D), lambda i: (i, 0)),
                   pl.BlockSpec((tm, H2), lambda i: (i, 0)),
                   pl.BlockSpec((H2, tm), lambda i: (0, i))],
        out_shape=[jax.ShapeDtypeStruct((M, D), BF16),
                   jax.ShapeDtypeStruct((M, H2), F32),
                   jax.ShapeDtypeStruct((H2, M), F32)],
        compiler_params=_params(("parallel",), blk),
        name="norm_dt",
    )(x, g.reshape(1, D), w_dt, w_dtT)


def _norm_kernel(x_ref, g_ref, o_ref):
    o_ref[...] = _rms(x_ref[...], g_ref[...]).astype(o_ref.dtype)


def _norm(x, g, out_dtype, name):
    M, D = x.shape
    tm = _pick(M, (512, 256, 128))
    blk = _nbytes((tm, D), F32) + _nbytes((tm, D), out_dtype)
    return pl.pallas_call(
        _norm_kernel,
        grid=(M // tm,),
        in_specs=[pl.BlockSpec((tm, D), lambda i: (i, 0)),
                  pl.BlockSpec((1, D), lambda i: (0, 0))],
        out_specs=pl.BlockSpec((tm, D), lambda i: (i, 0)),
        out_shape=jax.ShapeDtypeStruct((M, D), out_dtype),
        compiler_params=_params(("parallel",), blk),
        name=name,
    )(x, g.reshape(1, D))


def _row_rsqrt(ssq_ref, d):
    return lax.rsqrt(jnp.sum(ssq_ref[...], axis=-1, keepdims=True) * (1.0 / d) + EPS)


def _lane_partial_sumsq(v):
    sq = v * v
    part = sq[:, 0:LANES]
    for q in range(1, v.shape[1] // LANES):
        part = part + sq[:, q * LANES:(q + 1) * LANES]
    return part


def _mm_kernel(lhs_ref, w_ref, *rest, epilogue, norm_width):
    out_ref = rest[-1]
    acc = _dot(lhs_ref[...], w_ref[...])
    if norm_width:
        acc = acc * _row_rsqrt(rest[0], norm_width)
        rest = rest[1:]
    out_ref[...] = epilogue(acc, *rest[:-1]).astype(out_ref.dtype)


def _matmul(lhs, w, out_dtype, name, epilogue=None, extras=(), extra_specs=None, ssq=None, tm=None, tn=None,
            col_offsets=()):
    M, K = lhs.shape
    N = w.shape[1]
    tm = tm or _pick(M, (1024, 512, 256, 128))
    tn = next(t for t in ((tn,) if tn else ()) + (512, 256, 128)
              if N % t == 0 and all(o % t == 0 for o in col_offsets))
    if epilogue is None:
        epilogue = lambda acc: acc
    specs = extra_specs(tm, tn) if extra_specs else []
    if ssq is not None:
        extras = (ssq,) + tuple(extras)
        specs = [pl.BlockSpec((tm, LANES), lambda i, j: (i, 0))] + specs
    blk = (_nbytes((tm, K), lhs.dtype) + _nbytes((K, tn), w.dtype) + _nbytes((tm, tn), out_dtype)
           + sum(_nbytes(s.block_shape, e.dtype) for s, e in zip(specs, extras)))
    return pl.pallas_call(
        functools.partial(_mm_kernel, epilogue=epilogue, norm_width=K if ssq is not None else 0),
        grid=(M // tm, N // tn),
        in_specs=[pl.BlockSpec((tm, K), lambda i, j: (i, 0)),
                  pl.BlockSpec((K, tn), lambda i, j: (0, j))] + specs,
        out_specs=pl.BlockSpec((tm, tn), lambda i, j: (i, j)),
        out_shape=jax.ShapeDtypeStruct((M, N), out_dtype),
        compiler_params=_params(("parallel", "arbitrary"), blk, _nbytes((tm, tn), F32)),
        name=name,
    )(lhs, w, *extras)


def _tile_spec(col_offset):
    def make(tm, tn):
        assert col_offset % tn == 0
        off = col_offset // tn
        return pl.BlockSpec((tm, tn), lambda i, j: (i, off + j))
    return make


def _emit_norm_operand(x_new, g_ref, xg_ref, ssq_ref, first):
    xg_ref[...] = (x_new * g_ref[...]).astype(xg_ref.dtype)
    part = _lane_partial_sumsq(x_new)

    @pl.when(first)
    def _():
        ssq_ref[...] = part

    @pl.when(jnp.logical_not(first))
    def _():
        ssq_ref[...] += part


def _mm_res_norm_kernel(lhs_ref, w_ref, res_ref, g_ref, x_ref, xg_ref, ssq_ref):
    x_new = res_ref[...] + _dot(lhs_ref[...], w_ref[...])
    x_ref[...] = x_new
    _emit_norm_operand(x_new, g_ref, xg_ref, ssq_ref, pl.program_id(1) == 0)


def _matmul_residual_norm(lhs, w, res, g_next, name):
    M, K = lhs.shape
    N = w.shape[1]
    tm = _pick(M, (1024, 512, 256, 128))
    tn = _pick(N, (512, 256, 128))
    blk = (_nbytes((tm, K), lhs.dtype) + _nbytes((K, tn), w.dtype) + 2 * _nbytes((tm, tn), F32)
           + _nbytes((tm, tn), BF16) + _nbytes((tm, LANES), F32))
    return pl.pallas_call(
        _mm_res_norm_kernel,
        grid=(M // tm, N // tn),
        in_specs=[pl.BlockSpec((tm, K), lambda i, j: (i, 0)),
                  pl.BlockSpec((K, tn), lambda i, j: (0, j)),
                  pl.BlockSpec((tm, tn), lambda i, j: (i, j)),
                  pl.BlockSpec((1, tn), lambda i, j: (0, j))],
        out_specs=[pl.BlockSpec((tm, tn), lambda i, j: (i, j)),
                   pl.BlockSpec((tm, tn), lambda i, j: (i, j)),
                   pl.BlockSpec((tm, LANES), lambda i, j: (i, 0))],
        out_shape=[jax.ShapeDtypeStruct((M, N), F32),
                   jax.ShapeDtypeStruct((M, N), BF16),
                   jax.ShapeDtypeStruct((M, LANES), F32)],
        compiler_params=_params(("parallel", "arbitrary"), blk, _nbytes((tm, tn), F32)),
        name=name,
    )(lhs, w, res, g_next.reshape(1, N))


def _mm_kacc_kernel(lhs_ref, w_ref, res_ref, g_ref, x_ref, xg_ref, ssq_ref):
    k = pl.program_id(2)

    @pl.when(k == 0)
    def _():
        x_ref[...] = res_ref[...]

    x_ref[...] += _dot(lhs_ref[...], w_ref[...])

    @pl.when(k == pl.num_programs(2) - 1)
    def _():
        _emit_norm_operand(x_ref[...], g_ref, xg_ref, ssq_ref, pl.program_id(1) == 0)


def _matmul_kacc_residual_norm(lhs, w, res, g_next, name):
    M, K = lhs.shape
    N = w.shape[1]
    tm = _pick(M, (1024, 512, 256, 128))
    tn = _pick(N, (1024, 512, 256, 128))
    tk = _pick(K, (2048, 1024, 512, 256, 128))
    blk = (_nbytes((tm, tk), lhs.dtype) + _nbytes((tk, tn), w.dtype) + 2 * _nbytes((tm, tn), F32)
           + _nbytes((tm, tn), BF16) + _nbytes((tm, LANES), F32))
    return pl.pallas_call(
        _mm_kacc_kernel,
        grid=(M // tm, N // tn, K // tk),
        in_specs=[pl.BlockSpec((tm, tk), lambda i, j, k: (i, k)),
                  pl.BlockSpec((tk, tn), lambda i, j, k: (k, j)),
                  pl.BlockSpec((tm, tn), lambda i, j, k: (i, j)),
                  pl.BlockSpec((1, tn), lambda i, j, k: (0, j))],
        out_specs=[pl.BlockSpec((tm, tn), lambda i, j, k: (i, j)),
                   pl.BlockSpec((tm, tn), lambda i, j, k: (i, j)),
                   pl.BlockSpec((tm, LANES), lambda i, j, k: (i, 0))],
        out_shape=[jax.ShapeDtypeStruct((M, N), F32),
                   jax.ShapeDtypeStruct((M, N), BF16),
                   jax.ShapeDtypeStruct((M, LANES), F32)],
        compiler_params=_params(("parallel", "arbitrary", "arbitrary"), blk, _nbytes((tm, tn), F32)),
        name=name,
    )(lhs, w, res, g_next.reshape(1, N))


def _proj_conv_kernel(prev_ref, cur_ref, next_ref, w_ref, cw_ref, cb_ref, out_ref, lhs_ref, buf0_ref, buf1_ref,
                      *, tm, seq_tiles, width, n_col):
    i = pl.program_id(0)
    j = pl.program_id(1)

    @pl.when(j == 0)
    def _():
        lhs_ref[0:HALO, :] = prev_ref[...]
        lhs_ref[HALO:HALO + tm, :] = cur_ref[...]
        lhs_ref[HALO + tm:, :] = next_ref[...]
        buf1_ref[...] = jnp.zeros_like(buf1_ref)

    first = (i % seq_tiles) == 0
    last = (i % seq_tiles) == seq_tiles - 1

    def project(buf_ref):
        ext = _dot(lhs_ref[...], w_ref[...])
        buf_ref[0:HALO, :] = jnp.where(first, 0.0, ext[0:HALO])
        buf_ref[HALO:HALO + tm, :] = ext[HALO:HALO + tm]
        buf_ref[HALO + tm:, :] = jnp.where(last, 0.0, ext[HALO + tm:])

    def conv(buf_ref):
        lo = HALO - width // 2
        strip = min(tm, CONV_STRIP_ROWS)
        for r0 in range(0, tm, strip):
            acc = cb_ref[...] + cw_ref[0:1, :] * buf_ref[lo + r0:lo + r0 + strip, :]
            for k in range(1, width):
                acc = acc + cw_ref[k:k + 1, :] * buf_ref[lo + r0 + k:lo + r0 + k + strip, :]
            out_ref[r0:r0 + strip, :] = (acc * _sigmoid(acc)).astype(out_ref.dtype)

    even = (j % 2) == 0
    steady = j < n_col

    @pl.when(even & steady)
    def _():
        conv(buf1_ref)
        project(buf0_ref)

    @pl.when(jnp.logical_not(even) & steady)
    def _():
        conv(buf0_ref)
        project(buf1_ref)

    @pl.when(even & jnp.logical_not(steady))
    def _():
        conv(buf1_ref)

    @pl.when(jnp.logical_not(even) & jnp.logical_not(steady))
    def _():
        conv(buf0_ref)


def _proj_conv_silu(h, w, conv_w, conv_b, seq):
    M, K = h.shape
    width, C = conv_w.shape
    assert width // 2 <= HALO
    tm = _pick(seq, (1024, 512, 256, 128))
    tn = _pick(C, (512, 256, 128))
    n_col = C // tn
    hb = tm // HALO
    n_halo = M // HALO
    rows = tm + 2 * HALO
    blk = (_nbytes((rows, K), BF16) + _nbytes((K, tn), BF16) + _nbytes((tm, tn), BF16)
           + _nbytes((width + 1, tn), F32))
    proj_col = lambda j: jnp.minimum(j, n_col - 1)
    conv_col = lambda j: jnp.maximum(j - 1, 0)
    return pl.pallas_call(
        functools.partial(_proj_conv_kernel, tm=tm, seq_tiles=seq // tm, width=width, n_col=n_col),
        grid=(M // tm, n_col + 1),
        in_specs=[pl.BlockSpec((HALO, K), lambda i, j: (jnp.maximum(i * hb - 1, 0), 0)),
                  pl.BlockSpec((tm, K), lambda i, j: (i, 0)),
                  pl.BlockSpec((HALO, K), lambda i, j: (jnp.minimum((i + 1) * hb, n_halo - 1), 0)),
                  pl.BlockSpec((K, tn), lambda i, j: (0, proj_col(j))),
                  pl.BlockSpec((width, tn), lambda i, j: (0, conv_col(j))),
                  pl.BlockSpec((1, tn), lambda i, j: (0, conv_col(j)))],
        out_specs=pl.BlockSpec((tm, tn), lambda i, j: (i, conv_col(j))),
        out_shape=jax.ShapeDtypeStruct((M, C), BF16),
        scratch_shapes=[pltpu.VMEM((rows, K), BF16), pltpu.VMEM((rows, tn), F32), pltpu.VMEM((rows, tn), F32)],
        compiler_params=_params(("parallel", "arbitrary"), blk,
                                _nbytes((rows, K), BF16) + 5 * _nbytes((rows, tn), F32)),
        name="in_proj_xbc_conv",
    )(h, h, h, w, conv_w, conv_b.reshape(1, C))


def _pool_kernel(prev_ref, cur_ref, next_ref, pw_ref, ps_ref, out_ref, *, tm, seq, Wd):
    i = pl.program_id(0)
    t0 = (i * tm) % seq
    has_prev = t0 > 0
    has_next = t0 + tm < seq
    r = lax.broadcasted_iota(jnp.int32, (tm, tm), 0)
    c = lax.broadcasted_iota(jnp.int32, (tm, tm), 1)
    rh = lax.broadcasted_iota(jnp.int32, (tm, HALO), 0)
    ch = lax.broadcasted_iota(jnp.int32, (tm, HALO), 1)
    t = lax.broadcasted_iota(jnp.int32, (tm, Wd), 0) + t0
    for g, w in enumerate(POOL_WINDOWS):
        lo_off = w // 2
        hi_off = w - lo_off
        sl = slice(g * Wd, (g + 1) * Wd)
        band = jnp.where((c >= r - lo_off) & (c < r + hi_off), 1.0, 0.0).astype(BF16)
        band_prev = jnp.where(has_prev & (ch - HALO >= rh - lo_off), 1.0, 0.0).astype(BF16)
        band_next = jnp.where(has_next & (ch + tm < rh + hi_off), 1.0, 0.0).astype(BF16)
        cur = cur_ref[:, sl]
        wsum = _dot(band, cur) + _dot(band_prev, prev_ref[:, sl]) + _dot(band_next, next_ref[:, sl])
        cnt = jnp.minimum(t + hi_off, seq) - jnp.maximum(t - lo_off, 0)
        pooled = (wsum / cnt.astype(F32) - cur.astype(F32)).astype(BF16)
        out_ref[:, sl] = (_dot(pooled, pw_ref[g]) * ps_ref[:, sl]).astype(out_ref.dtype)


def _pool(proj, col_offset, pool_w, pool_scale, seq):
    M = proj.shape[0]
    PG, Wd, _ = pool_w.shape
    PD = PG * Wd
    assert PG == len(POOL_WINDOWS) and max(POOL_WINDOWS) // 2 <= HALO
    assert col_offset % PD == 0
    off = col_offset // PD
    tm = _pick(seq, (256, 128))
    hb = tm // HALO
    n_halo = M // HALO
    blk = (2 * _nbytes((tm, PD), BF16) + 2 * _nbytes((HALO, PD), BF16) + _nbytes((PG, Wd, Wd), BF16))
    return pl.pallas_call(
        functools.partial(_pool_kernel, tm=tm, seq=seq, Wd=Wd),
        grid=(M // tm,),
        in_specs=[pl.BlockSpec((HALO, PD), lambda i: (jnp.maximum(i * hb - 1, 0), off)),
                  pl.BlockSpec((tm, PD), lambda i: (i, off)),
                  pl.BlockSpec((HALO, PD), lambda i: (jnp.minimum((i + 1) * hb, n_halo - 1), off)),
                  pl.BlockSpec((PG, Wd, Wd), lambda i: (0, 0, 0)),
                  pl.BlockSpec((1, PD), lambda i: (0, 0))],
        out_specs=pl.BlockSpec((tm, PD), lambda i: (i, 0)),
        out_shape=jax.ShapeDtypeStruct((M, PD), BF16),
        compiler_params=_params(("parallel",), blk, 4 * _nbytes((tm, PD), F32)),
        name="pool",
    )(proj, proj, proj, pool_w, pool_scale.reshape(1, PD))


def _cumsum_rows(tri, v):
    a, b, c = _split3(v)
    return _dot(tri, a) + _dot(tri, b) + _dot(tri, c)


def _cumsum_lanes(v, triT):
    a, b, c = _split3(v)
    return _dot(a, triT) + _dot(b, triT) + _dot(c, triT)


def _hilo(v):
    a, b = _split2(v)
    return jnp.concatenate([a, b], axis=1)


def _expand(v_hilo, e2):
    return _dot(v_hilo, e2)


def _ssd_state_kernel(xf_ref, bf_ref, dtf_ref, xb_ref, bb_ref, dtb_ref, bias_ref, alog_ref, e_ref,
                      pf_ref, pb_ref, h_ref, *, H, G, N, Q):
    L = SSD_CHUNK

    @pl.when(pl.program_id(1) == 0)
    def _():
        h_ref[...] = jnp.zeros_like(h_ref)

    row = lax.broadcasted_iota(jnp.int32, (L, L), 0)
    col = lax.broadcasted_iota(jnp.int32, (L, L), 1)
    tri = jnp.where(row >= col, 1.0, 0.0).astype(BF16)
    is_fwd = lax.broadcasted_iota(jnp.int32, (L, 2 * H), 1) < H
    A = -jnp.exp(alog_ref[...])

    def direction(d, x_ref, b_ref, dtraw_ref, p_ref):
        dt = _softplus(dtraw_ref[...] + bias_ref[...])
        a = dt * A
        cum = _cumsum_rows(tri, a)
        total = cum[L - 1:L, :]
        dte = jnp.exp(jnp.where(is_fwd, total - cum, cum - a))
        wexp = _expand(_hilo(dt * dte), e_ref[d])
        xs = (x_ref[...].astype(F32) * wexp).astype(BF16)
        cdexp = _expand(_hilo(jnp.broadcast_to(jnp.exp(total), (8, 2 * H))), e_ref[d])[0:1, :]
        bm = b_ref[...]
        for g in range(G):
            st = _dot_tn(bm[:, g * N:(g + 1) * N], xs[:, g * Q:(g + 1) * Q])
            h_old = h_ref[d, g]
            p_ref[0, 0, g] = h_old.astype(BF16)
            h_ref[d, g] = cdexp[:, g * Q:(g + 1) * Q] * h_old + st

    direction(0, xf_ref, bf_ref, dtf_ref, pf_ref)
    direction(1, xb_ref, bb_ref, dtb_ref, pb_ref)


def _ssd_states(xbc, dtraw, bias, alog, emat, batch, seq, D, H):
    L, G, N = SSD_CHUNK, SSD_GROUPS, SSD_STATE
    GN = G * N
    Q = D // G
    NC = seq // L
    assert D % GN == 0 or GN % D == 0
    b_off = D // GN
    H2 = 2 * H
    blk = 2 * (_nbytes((L, D), BF16) + _nbytes((L, GN), BF16) + _nbytes((L, H2), F32)
               + _nbytes((G, N, Q), BF16)) + _nbytes((2, 2 * H2, D), BF16)
    fwd = lambda b, t: b * NC + t
    bwd = lambda b, t: b * NC + (NC - 1 - t)
    out_sds = jax.ShapeDtypeStruct((batch, NC, G, N, Q), BF16)
    return pl.pallas_call(
        functools.partial(_ssd_state_kernel, H=H, G=G, N=N, Q=Q),
        grid=(batch, NC),
        in_specs=[pl.BlockSpec((L, D), lambda b, t: (fwd(b, t), 0)),
                  pl.BlockSpec((L, GN), lambda b, t: (fwd(b, t), b_off)),
                  pl.BlockSpec((L, H2), lambda b, t: (fwd(b, t), 0)),
                  pl.BlockSpec((L, D), lambda b, t: (bwd(b, t), 0)),
                  pl.BlockSpec((L, GN), lambda b, t: (bwd(b, t), b_off)),
                  pl.BlockSpec((L, H2), lambda b, t: (bwd(b, t), 0)),
                  pl.BlockSpec((1, H2), lambda b, t: (0, 0)),
                  pl.BlockSpec((1, H2), lambda b, t: (0, 0)),
                  pl.BlockSpec((2, 2 * H2, D), lambda b, t: (0, 0, 0))],
        out_specs=[pl.BlockSpec((1, 1, G, N, Q), lambda b, t: (b, t, 0, 0, 0)),
                   pl.BlockSpec((1, 1, G, N, Q), lambda b, t: (b, NC - 1 - t, 0, 0, 0))],
        out_shape=[out_sds, out_sds],
        scratch_shapes=[pltpu.VMEM((2, G, N, Q), F32)],
        compiler_params=_params(("parallel", "arbitrary"), blk,
                                _nbytes((2, G, N, Q), F32) + 4 * _nbytes((L, D), F32)),
        name="ssd_states",
    )(xbc, xbc, dtraw, xbc, xbc, dtraw, bias, alog, emat)


def _ssd_out_kernel(dsk_ref, x_ref, b_ref, c_ref, zs_ref, dt_ref, dtT_ref, pf_ref, pb_ref,
                    bias_ref, alog_ref, biasT_ref, alogT_ref, ng_ref, e_ref, out_ref,
                    *, H, G, N, Q, P):
    L = SSD_CHUNK
    E = H // G
    row = lax.broadcasted_iota(jnp.int32, (L, L), 0)
    col = lax.broadcasted_iota(jnp.int32, (L, L), 1)
    lower = row >= col
    eye = row == col
    tri = jnp.where(lower, 1.0, 0.0).astype(BF16)
    triT = jnp.where(row <= col, 1.0, 0.0).astype(BF16)
    is_fwd = lax.broadcasted_iota(jnp.int32, (L, 2 * H), 1) < H
    is_fwdT = lax.broadcasted_iota(jnp.int32, (2 * H, L), 0) < H

    dt = _softplus(dt_ref[...] + bias_ref[...])
    a = dt * (-jnp.exp(alog_ref[...]))
    cum = _cumsum_rows(tri, a)
    total = cum[L - 1:L, :]
    excl = cum - a
    col2 = jnp.where(is_fwd, cum, excl) * LOG2E
    sdec = _hilo(jnp.exp(jnp.where(is_fwd, cum, total - excl)))

    dtT = _softplus(dtT_ref[...] + biasT_ref[...])
    aT = dtT * (-jnp.exp(alogT_ref[...]))
    cumT = _cumsum_lanes(aT, triT)
    lg = jnp.log2(dtT)
    row2 = jnp.where(is_fwdT, cumT * LOG2E - lg, (cumT - aT) * LOG2E + lg)

    lane = lax.broadcasted_iota(jnp.int32, (L, 2 * P), 1)
    keep_left = jnp.where(lane < P, 1.0, 0.0).astype(BF16)
    keep_right = jnp.where(lane < P, 0.0, 1.0).astype(BF16)

    for g in range(G):
        c_g = c_ref[:, g * N:(g + 1) * N]
        cb = _dot_nt(c_g, b_ref[:, g * N:(g + 1) * N])
        cb_diag = jnp.sum(jnp.where(eye, cb, 0.0), axis=0, keepdims=True)
        pieces = []
        for e2 in range(E // 2):
            ws = []
            for h in (g * E + 2 * e2, g * E + 2 * e2 + 1):
                arg = jnp.where(lower,
                                col2[:, h:h + 1] - row2[h:h + 1, :],
                                row2[H + h:H + h + 1, :] - col2[:, H + h:H + h + 1])
                diag = cb_diag * dtT[H + h:H + h + 1, :] + dsk_ref[h]
                ws.append((cb * jnp.exp2(arg) + jnp.where(eye, diag, 0.0)).astype(BF16))
            h0 = g * E + 2 * e2
            xp = x_ref[:, h0 * P:(h0 + 2) * P]
            rhs = jnp.concatenate([xp * keep_left, xp * keep_right], axis=0)
            pieces.append(_dot(jnp.concatenate(ws, axis=1), rhs))
        y = jnp.concatenate(pieces, axis=1) if len(pieces) > 1 else pieces[0]
        sl = slice(g * Q, (g + 1) * Q)
        y = (y + _dot(c_g, pf_ref[0, 0, g]) * _expand(sdec, e_ref[0, :, sl])
             + _dot(c_g, pb_ref[0, 0, g]) * _expand(sdec, e_ref[1, :, sl]))
        y = y * zs_ref[:, sl].astype(F32)
        ms = jnp.mean(y * y, axis=-1, keepdims=True)
        out_ref[:, sl] = (y * lax.rsqrt(ms + EPS) * ng_ref[:, sl]).astype(out_ref.dtype)


def _ssd_out(zs, xbc, dtraw, dtrawT, prev_f, prev_b, bias, alog, biasT, alogT, d_skip, norm_g, emat,
             batch, seq, D, H):
    L, G, N = SSD_CHUNK, SSD_GROUPS, SSD_STATE
    GN = G * N
    Q = D // G
    P = D // H
    NC = seq // L
    M = batch * seq
    H2 = 2 * H
    assert (H // G) % 2 == 0 and 2 * P == 128 and D % GN == 0
    b_off = D // GN
    blk = (2 * _nbytes((L, D), BF16) + 2 * _nbytes((L, GN), BF16) + 2 * _nbytes((L, H2), F32)
           + 2 * _nbytes((G, N, Q), BF16) + _nbytes((2, 2 * H2, D), BF16) + _nbytes((L, D), BF16))
    rowblk = lambda b, c: b * NC + c
    return pl.pallas_call(
        functools.partial(_ssd_out_kernel, H=H, G=G, N=N, Q=Q, P=P),
        grid=(batch, NC),
        in_specs=[pl.BlockSpec(memory_space=pltpu.SMEM),
                  pl.BlockSpec((L, D), lambda b, c: (rowblk(b, c), 0)),
                  pl.BlockSpec((L, GN), lambda b, c: (rowblk(b, c), b_off)),
                  pl.BlockSpec((L, GN), lambda b, c: (rowblk(b, c), b_off + 1)),
                  pl.BlockSpec((L, D), lambda b, c: (rowblk(b, c), 0)),
                  pl.BlockSpec((L, H2), lambda b, c: (rowblk(b, c), 0)),
                  pl.BlockSpec((H2, L), lambda b, c: (0, rowblk(b, c))),
                  pl.BlockSpec((1, 1, G, N, Q), lambda b, c: (b, c, 0, 0, 0)),
                  pl.BlockSpec((1, 1, G, N, Q), lambda b, c: (b, c, 0, 0, 0)),
                  pl.BlockSpec((1, H2), lambda b, c: (0, 0)),
                  pl.BlockSpec((1, H2), lambda b, c: (0, 0)),
                  pl.BlockSpec((H2, L), lambda b, c: (0, 0)),
                  pl.BlockSpec((H2, L), lambda b, c: (0, 0)),
                  pl.BlockSpec((1, D), lambda b, c: (0, 0)),
                  pl.BlockSpec((2, 2 * H2, D), lambda b, c: (0, 0, 0))],
        out_specs=pl.BlockSpec((L, D), lambda b, c: (rowblk(b, c), 0)),
        out_shape=jax.ShapeDtypeStruct((M, D), BF16),
        compiler_params=_params(("parallel", "parallel"), blk, 8 * _nbytes((L, D), F32)),
        name="ssd_out",
    )(d_skip, xbc, xbc, xbc, zs, dtraw, dtrawT, prev_f, prev_b, bias, alog, biasT, alogT,
      norm_g.reshape(1, D), emat)


def _prepare_layer(norm_mix_g, w_in, conv_w, conv_b, dt_bias_f, dt_bias_b, a_log_f, a_log_b, d_skip,
                   ssd_norm_g, w_ssd_up, pool_w, pool_scale, w_pool_up, w_out, norm_mlp_g, w_ff1, w_ff2,
                   norm_ple_g, w_ple_gate, w_ple_proj):
    D = w_in.shape[0]
    H = d_skip.shape[0]
    C = conv_w.shape[1]
    s_xbc = D + C
    s_dt = s_xbc + 2 * H
    w_dt = w_in[:, s_xbc:s_dt].astype(BF16)
    head_of_lane = jnp.arange(D, dtype=jnp.int32) // (D // H)
    onehot = (jnp.arange(H, dtype=jnp.int32)[:, None] == head_of_lane[None, :]).astype(BF16)
    zeros = jnp.zeros_like(onehot)
    L = SSD_CHUNK
    bias = jnp.concatenate([dt_bias_f, dt_bias_b])
    alog = jnp.concatenate([a_log_f, a_log_b])
    return dict(
        D=D, H=H, C=C,
        norm_mix_g=norm_mix_g,
        w_z=w_in[:, :D].astype(BF16), w_xbc=w_in[:, D:s_xbc].astype(BF16),
        w_tail=w_in[:, s_dt:].astype(BF16),
        w_dt=w_dt, w_dtT=w_dt.T,
        conv_w=conv_w, conv_b=conv_b,
        bias=bias.reshape(1, 2 * H), alog=alog.reshape(1, 2 * H),
        biasT=jnp.broadcast_to(bias[:, None], (2 * H, L)), alogT=jnp.broadcast_to(alog[:, None], (2 * H, L)),
        emat=jnp.stack([jnp.concatenate([onehot, zeros, onehot, zeros]),
                        jnp.concatenate([zeros, onehot, zeros, onehot])]),
        d_skip=d_skip, ssd_norm_g=ssd_norm_g,
        w_ssd_up=w_ssd_up.astype(BF16),
        pool_w=pool_w.astype(BF16), pool_scale=pool_scale, w_pool_up=w_pool_up.astype(BF16),
        w_out=w_out.astype(BF16), norm_mlp_g=norm_mlp_g,
        w_ff1=w_ff1.astype(BF16), w_ff2=w_ff2.astype(BF16),
        norm_ple_g=norm_ple_g, w_ple_gate=w_ple_gate.astype(BF16), w_ple_proj=w_ple_proj.astype(BF16),
    )


def _layer(x, p_l, lp, batch, seq):
    D, H = lp["D"], lp["H"]
    PD = lp["w_pool_up"].shape[0]
    off_ga, off_gb = PD, PD + D

    h, dtraw, dtrawT = _norm_dt(x, lp["norm_mix_g"], lp["w_dt"], lp["w_dtT"])
    zs = _matmul(h, lp["w_z"], BF16, "in_proj_z", epilogue=lambda acc: acc * _sigmoid(acc), tn=_TN_WIDE)
    tail = _matmul(h, lp["w_tail"], BF16, "in_proj_tail", tn=_TN_WIDE)

    xbc = _proj_conv_silu(h, lp["w_xbc"], lp["conv_w"], lp["conv_b"], seq)
    prev_f, prev_b = _ssd_states(xbc, dtraw, lp["bias"], lp["alog"], lp["emat"], batch, seq, D, H)
    y = _ssd_out(zs, xbc, dtraw, dtrawT, prev_f, prev_b, lp["bias"], lp["alog"], lp["biasT"],
                 lp["alogT"], lp["d_skip"], lp["ssd_norm_g"], lp["emat"], batch, seq, D, H)
    gated_a = _matmul(y, lp["w_ssd_up"], BF16, "ssd_up", tn=_TN_WIDE,
                      epilogue=lambda acc, g_ref: _sigmoid(g_ref[...].astype(F32)) * acc,
                      extras=(tail,), col_offsets=(off_ga,),
                      extra_specs=lambda tm, tn: [_tile_spec(off_ga)(tm, tn)])

    pooled = _pool(tail, 0, lp["pool_w"], lp["pool_scale"], seq)
    mixed = _matmul(pooled, lp["w_pool_up"], BF16, "pool_up", tn=_TN_WIDE,
                    epilogue=lambda acc, g_ref, a_ref: (_sigmoid(g_ref[...].astype(F32)) * acc
                                                        + a_ref[...].astype(F32)),
                    extras=(tail, gated_a), col_offsets=(off_gb,),
                    extra_specs=lambda tm, tn: [_tile_spec(off_gb)(tm, tn), _tile_spec(0)(tm, tn)])
    x, xg, ssq = _matmul_residual_norm(mixed, lp["w_out"], x, lp["norm_mlp_g"], "out_proj")

    ff = _matmul(xg, lp["w_ff1"], BF16, "ff1", ssq=ssq, tn=_TN_WIDE,
                 epilogue=lambda acc: jnp.square(jnp.maximum(acc, 0.0)))
    x, xg, ssq = _matmul_kacc_residual_norm(ff, lp["w_ff2"], x, lp["norm_ple_g"], "ff2")

    ple = p_l.shape[1]
    x = _matmul(xg, lp["w_ple_gate"], F32, "ple", ssq=ssq,
                epilogue=lambda acc, x_ref, p_ref, wp_ref: (
                    x_ref[...] + _sigmoid(acc) * _dot(p_ref[...].astype(BF16), wp_ref[...])),
                extras=(x, p_l, lp["w_ple_proj"]),
                extra_specs=lambda tm, tn: [_tile_spec(0)(tm, tn),
                                            pl.BlockSpec((tm, ple), lambda i, j: (i, 0)),
                                            pl.BlockSpec((ple, tn), lambda i, j: (0, j))])
    return x


def _trunk(x, p, layers, norm_final_g):
    batch, seq, D = x.shape
    assert seq % SSD_CHUNK == 0
    xf = x.reshape(batch * seq, D)
    for i, lp in enumerate(layers):
        xf = _layer(xf, p[i].reshape(batch * seq, -1), lp, batch, seq)
    return _norm(xf, norm_final_g, F32, "norm_final").reshape(batch, seq, D)


def kernel(x_prompt, x_sample, p_prompt, p_sample, norm_mix_g, w_in, conv_w, conv_b, dt_bias_f, dt_bias_b, a_log_f, a_log_b, d_skip, ssd_norm_g, w_ssd_up, pool_w, pool_scale, w_pool_up, w_out, norm_mlp_g, w_ff1, w_ff2, norm_ple_g, w_ple_gate, w_ple_proj, norm_final_g):
    stacked = (norm_mix_g, w_in, conv_w, conv_b, dt_bias_f, dt_bias_b, a_log_f, a_log_b, d_skip,
               ssd_norm_g, w_ssd_up, pool_w, pool_scale, w_pool_up, w_out, norm_mlp_g, w_ff1, w_ff2,
               norm_ple_g, w_ple_gate, w_ple_proj)
    layers = [_prepare_layer(*[w[i] for w in stacked]) for i in range(w_in.shape[0])]
    y_prompt = _trunk(x_prompt, p_prompt, layers, norm_final_g)
    y_sample = _trunk(x_sample, p_sample, layers, norm_final_g)
    return (y_prompt, y_sample)
```

```python
import functools

import jax
import jax.numpy as jnp
from jax import lax
from jax.experimental import pallas as pl
from jax.experimental.pallas import tpu as pltpu

F32 = jnp.float32
BF16 = jnp.bfloat16

EPS = 1e-6
LOG2E = 1.4426950408889634
LN2 = 0.6931471805599453
SSD_GROUPS = 8
SSD_STATE = 128
SSD_CHUNK = 128
POOL_WINDOWS = (2, 4, 8, 16)

V7X_VMEM_BYTES = 64 * 1024 * 1024
V7X_VMEM_RESERVE_BYTES = 8 * 1024 * 1024
BF16_SUBLANE_TILE = 16
LANES = 128
HALO = BF16_SUBLANE_TILE
_TN_WIDE = 1024
CONV_SUB_ROWS = 128


def _pick(dim, prefs):
    for p in prefs:
        if dim % p == 0:
            return p
    return dim


def _nbytes(shape, dtype):
    n = 1
    for s in shape:
        n *= s
    return n * jnp.dtype(dtype).itemsize


def _params(semantics, block_bytes, scratch_bytes=0):
    est = 2 * block_bytes + scratch_bytes
    limit = min(2 * est + (4 << 20), V7X_VMEM_BYTES - V7X_VMEM_RESERVE_BYTES)
    return pltpu.CompilerParams(dimension_semantics=semantics, vmem_limit_bytes=int(limit))


def _sigmoid(v):
    return 1.0 / (1.0 + jnp.exp(-v))


def _softplus(v):
    u = jnp.exp(-jnp.abs(v))
    w = 1.0 + u
    return jnp.maximum(v, 0.0) + jnp.where(w == 1.0, u, (jnp.log2(w) * LN2) * (u / (w - 1.0)))


def _split3(v):
    a = v.astype(BF16)
    r = v - a.astype(F32)
    b = r.astype(BF16)
    c = (r - b.astype(F32)).astype(BF16)
    return a, b, c


def _split2(v):
    a = v.astype(BF16)
    b = (v - a.astype(F32)).astype(BF16)
    return a, b


def _dot(a, b):
    return jnp.dot(a, b, preferred_element_type=F32)


def _dot_nt(a, b):
    return lax.dot_general(a, b, (((1,), (1,)), ((), ())), preferred_element_type=F32)


def _dot_tn(a, b):
    return lax.dot_general(a, b, (((0,), (0,)), ((), ())), preferred_element_type=F32)


def _rms(x, g):
    ms = jnp.mean(x * x, axis=-1, keepdims=True)
    return x * lax.rsqrt(ms + EPS) * g


def _norm_dt_kernel(x_ref, g_ref, wdt_ref, wdtT_ref, h_ref, dt_ref, dtT_ref):
    h = _rms(x_ref[...], g_ref[...]).astype(BF16)
    h_ref[...] = h
    dt_ref[...] = _dot(h, wdt_ref[...])
    dtT_ref[...] = _dot_nt(wdtT_ref[...], h)


def _norm_dt(x, g, w_dt, w_dtT):
    M, D = x.shape
    H2 = w_dt.shape[1]
    tm = _pick(M, (512, 256, 128))
    blk = (_nbytes((tm, D), F32) + _nbytes((tm, D), BF16) + 2 * _nbytes((D, H2), BF16)
           + 2 * _nbytes((tm, H2), F32))
    return pl.pallas_call(
        _norm_dt_kernel,
        grid=(M // tm,),
        in_specs=[pl.BlockSpec((tm, D), lambda i: (i, 0)),
                  pl.BlockSpec((1, D), lambda i: (0, 0)),
                  pl.BlockSpec((D, H2), lambda i: (0, 0)),
                  pl.BlockSpec((H2, D), lambda i: (0, 0))],
        out_specs=[pl.BlockSpec((tm, D), lambda i: (i, 0)),
                   pl.BlockSpec((tm, H2), lambda i: (i, 0)),
                   pl.BlockSpec((H2, tm), lambda i: (0, i))],
        out_shape=[jax.ShapeDtypeStruct((M, D), BF16),
                   jax.ShapeDtypeStruct((M, H2), F32),
                   jax.ShapeDtypeStruct((H2, M), F32)],
        compiler_params=_params(("parallel",), blk),
        name="norm_dt",
    )(x, g.reshape(1, D), w_dt, w_dtT)


def _norm_kernel(x_ref, g_ref, o_ref):
    o_ref[...] = _rms(x_ref[...], g_ref[...]).astype(o_ref.dtype)


def _norm(x, g, out_dtype, name):
    M, D = x.shape
    tm = _pick(M, (512, 256, 128))
    blk = _nbytes((tm, D), F32) + _nbytes((tm, D), out_dtype)
    return pl.pallas_call(
        _norm_kernel,
        grid=(M // tm,),
        in_specs=[pl.BlockSpec((tm, D), lambda i: (i, 0)),
                  pl.BlockSpec((1, D), lambda i: (0, 0))],
        out_specs=pl.BlockSpec((tm, D), lambda i: (i, 0)),
        out_shape=jax.ShapeDtypeStruct((M, D), out_dtype),
        compiler_params=_params(("parallel",), blk),
        name=name,
    )(x, g.reshape(1, D))


def _row_rsqrt(ssq_ref, d):
    return lax.rsqrt(jnp.sum(ssq_ref[...], axis=-1, keepdims=True) * (1.0 / d) + EPS)


def _lane_partial_sumsq(v):
    sq = v * v
    part = sq[:, 0:LANES]
    for q in range(1, v.shape[1] // LANES):
        part = part + sq[:, q * LANES:(q + 1) * LANES]
    return part


def _mm_kernel(lhs_ref, w_ref, *rest, epilogue, norm_width):
    out_ref = rest[-1]
    acc = _dot(lhs_ref[...], w_ref[...])
    if norm_width:
        acc = acc * _row_rsqrt(rest[0], norm_width)
        rest = rest[1:]
    out_ref[...] = epilogue(acc, *rest[:-1]).astype(out_ref.dtype)


def _matmul(lhs, w, out_dtype, name, epilogue=None, extras=(), extra_specs=None, ssq=None, tm=None, tn=None,
            col_offsets=(), w_cols=None):
    M, K = lhs.shape
    w_off, N = w_cols if w_cols else (0, w.shape[1])
    col_offsets = tuple(col_offsets) + (w_off,)
    tm = tm or _pick(M, (1024, 512, 256, 128))
    tn = next(t for t in ((tn,) if tn else ()) + (512, 256, 128)
              if N % t == 0 and all(o % t == 0 for o in col_offsets))
    if epilogue is None:
        epilogue = lambda acc: acc
    specs = extra_specs(tm, tn) if extra_specs else []
    if ssq is not None:
        extras = (ssq,) + tuple(extras)
        specs = [pl.BlockSpec((tm, LANES), lambda i, j: (i, 0))] + specs
    blk = (_nbytes((tm, K), lhs.dtype) + _nbytes((K, tn), w.dtype) + _nbytes((tm, tn), out_dtype)
           + sum(_nbytes(s.block_shape, e.dtype) for s, e in zip(specs, extras)))
    w_blk0 = w_off // tn
    return pl.pallas_call(
        functools.partial(_mm_kernel, epilogue=epilogue, norm_width=K if ssq is not None else 0),
        grid=(M // tm, N // tn),
        in_specs=[pl.BlockSpec((tm, K), lambda i, j: (i, 0)),
                  pl.BlockSpec((K, tn), lambda i, j: (0, w_blk0 + j))] + specs,
        out_specs=pl.BlockSpec((tm, tn), lambda i, j: (i, j)),
        out_shape=jax.ShapeDtypeStruct((M, N), out_dtype),
        compiler_params=_params(("parallel", "arbitrary"), blk, _nbytes((tm, tn), F32)),
        name=name,
    )(lhs, w, *extras)


def _tile_spec(col_offset):
    def make(tm, tn):
        assert col_offset % tn == 0
        off = col_offset // tn
        return pl.BlockSpec((tm, tn), lambda i, j: (i, off + j))
    return make


def _emit_norm_operand(x_new, g_ref, xg_ref, ssq_ref, first):
    xg_ref[...] = (x_new * g_ref[...]).astype(xg_ref.dtype)
    part = _lane_partial_sumsq(x_new)

    @pl.when(first)
    def _():
        ssq_ref[...] = part

    @pl.when(jnp.logical_not(first))
    def _():
        ssq_ref[...] += part


def _mm_res_norm_kernel(lhs_ref, w_ref, res_ref, g_ref, x_ref, xg_ref, ssq_ref):
    x_new = res_ref[...] + _dot(lhs_ref[...], w_ref[...])
    x_ref[...] = x_new
    _emit_norm_operand(x_new, g_ref, xg_ref, ssq_ref, pl.program_id(1) == 0)


def _matmul_residual_norm(lhs, w, res, g_next, name):
    M, K = lhs.shape
    N = w.shape[1]
    tm = _pick(M, (1024, 512, 256, 128))
    tn = _pick(N, (512, 256, 128))
    blk = (_nbytes((tm, K), lhs.dtype) + _nbytes((K, tn), w.dtype) + 2 * _nbytes((tm, tn), F32)
           + _nbytes((tm, tn), BF16) + _nbytes((tm, LANES), F32))
    return pl.pallas_call(
        _mm_res_norm_kernel,
        grid=(M // tm, N // tn),
        in_specs=[pl.BlockSpec((tm, K), lambda i, j: (i, 0)),
                  pl.BlockSpec((K, tn), lambda i, j: (0, j)),
                  pl.BlockSpec((tm, tn), lambda i, j: (i, j)),
                  pl.BlockSpec((1, tn), lambda i, j: (0, j))],
        out_specs=[pl.BlockSpec((tm, tn), lambda i, j: (i, j)),
                   pl.BlockSpec((tm, tn), lambda i, j: (i, j)),
                   pl.BlockSpec((tm, LANES), lambda i, j: (i, 0))],
        out_shape=[jax.ShapeDtypeStruct((M, N), F32),
                   jax.ShapeDtypeStruct((M, N), BF16),
                   jax.ShapeDtypeStruct((M, LANES), F32)],
        compiler_params=_params(("parallel", "arbitrary"), blk, _nbytes((tm, tn), F32)),
        name=name,
    )(lhs, w, res, g_next.reshape(1, N))


def _mm_kacc_kernel(lhs_ref, w_ref, res_ref, g_ref, x_ref, xg_ref, ssq_ref):
    k = pl.program_id(2)

    @pl.when(k == 0)
    def _():
        x_ref[...] = res_ref[...]

    x_ref[...] += _dot(lhs_ref[...], w_ref[...])

    @pl.when(k == pl.num_programs(2) - 1)
    def _():
        _emit_norm_operand(x_ref[...], g_ref, xg_ref, ssq_ref, pl.program_id(1) == 0)


def _matmul_kacc_residual_norm(lhs, w, res, g_next, name):
    M, K = lhs.shape
    N = w.shape[1]
    tm = _pick(M, (1024, 512, 256, 128))
    tn = _pick(N, (1024, 512, 256, 128))
    tk = _pick(K, (2048, 1024, 512, 256, 128))
    blk = (_nbytes((tm, tk), lhs.dtype) + _nbytes((tk, tn), w.dtype) + 2 * _nbytes((tm, tn), F32)
           + _nbytes((tm, tn), BF16) + _nbytes((tm, LANES), F32))
    return pl.pallas_call(
        _mm_kacc_kernel,
        grid=(M // tm, N // tn, K // tk),
        in_specs=[pl.BlockSpec((tm, tk), lambda i, j, k: (i, k)),
                  pl.BlockSpec((tk, tn), lambda i, j, k: (k, j)),
                  pl.BlockSpec((tm, tn), lambda i, j, k: (i, j)),
                  pl.BlockSpec((1, tn), lambda i, j, k: (0, j))],
        out_specs=[pl.BlockSpec((tm, tn), lambda i, j, k: (i, j)),
                   pl.BlockSpec((tm, tn), lambda i, j, k: (i, j)),
                   pl.BlockSpec((tm, LANES), lambda i, j, k: (i, 0))],
        out_shape=[jax.ShapeDtypeStruct((M, N), F32),
                   jax.ShapeDtypeStruct((M, N), BF16),
                   jax.ShapeDtypeStruct((M, LANES), F32)],
        compiler_params=_params(("parallel", "arbitrary", "arbitrary"), blk, _nbytes((tm, tn), F32)),
        name=name,
    )(lhs, w, res, g_next.reshape(1, N))


def _conv_kernel(prev_ref, cur_ref, next_ref, w_ref, b_ref, shift_ref, out_ref, buf_ref, *, tm, seq_tiles, width):
    i = pl.program_id(0)
    first = (i % seq_tiles) == 0
    last = (i % seq_tiles) == seq_tiles - 1
    prev = prev_ref[...]
    nxt = next_ref[...]
    buf_ref[0:HALO, :] = jnp.where(first, jnp.zeros_like(prev), prev)
    buf_ref[HALO:HALO + tm, :] = cur_ref[...]
    buf_ref[HALO + tm:, :] = jnp.where(last, jnp.zeros_like(nxt), nxt)

    half = width // 2
    taps = [k for k in range(width) if k != half]
    sub = shift_ref.shape[0] // len(taps)
    win = sub + 2 * HALO
    shift = shift_ref[...]
    for r0 in range(0, tm, sub):
        window = buf_ref[r0:r0 + win, :]
        shifted = _dot(shift, window)
        acc = b_ref[...] + w_ref[half:half + 1, :] * window[HALO:HALO + sub].astype(F32)
        for n, k in enumerate(taps):
            acc = acc + w_ref[k:k + 1, :] * shifted[n * sub:(n + 1) * sub]
        out_ref[r0:r0 + sub, :] = (acc * _sigmoid(acc)).astype(out_ref.dtype)


def _conv_silu(proj, col_offset, conv_w, conv_b, seq):
    M = proj.shape[0]
    width, C = conv_w.shape
    assert width // 2 <= HALO
    tm = _pick(seq, (512, 256, 128))
    tc = _pick(C, (2048, 1024, 512, 256, 128))
    while col_offset % tc:
        tc //= 2
    off = col_offset // tc
    hb = tm // HALO
    n_halo = M // HALO
    half = width // 2
    sub = min(tm, CONV_SUB_ROWS)
    win = sub + 2 * HALO
    rows = jnp.arange(sub, dtype=jnp.int32)[:, None]
    cols = jnp.arange(win, dtype=jnp.int32)[None, :]
    shift = jnp.concatenate([(cols == rows + HALO + (k - half)) for k in range(width) if k != half],
                            axis=0).astype(BF16)
    blk = (2 * _nbytes((tm, tc), BF16) + 2 * _nbytes((HALO, tc), BF16) + _nbytes((width + 1, tc), F32)
           + _nbytes(shift.shape, BF16))
    return pl.pallas_call(
        functools.partial(_conv_kernel, tm=tm, seq_tiles=seq // tm, width=width),
        grid=(M // tm, C // tc),
        in_specs=[pl.BlockSpec((HALO, tc), lambda i, j: (jnp.maximum(i * hb - 1, 0), off + j)),
                  pl.BlockSpec((tm, tc), lambda i, j: (i, off + j)),
                  pl.BlockSpec((HALO, tc), lambda i, j: (jnp.minimum((i + 1) * hb, n_halo - 1), off + j)),
                  pl.BlockSpec((width, tc), lambda i, j: (0, j)),
                  pl.BlockSpec((1, tc), lambda i, j: (0, j)),
                  pl.BlockSpec(shift.shape, lambda i, j: (0, 0))],
        out_specs=pl.BlockSpec((tm, tc), lambda i, j: (i, j)),
        out_shape=jax.ShapeDtypeStruct((M, C), BF16),
        scratch_shapes=[pltpu.VMEM((tm + 2 * HALO, tc), BF16)],
        compiler_params=_params(("parallel", "parallel"), blk,
                                _nbytes((tm + 2 * HALO, tc), BF16) + 8 * _nbytes((CONV_SUB_ROWS, tc), F32)),
        name="conv_silu",
    )(proj, proj, proj, conv_w, conv_b.reshape(1, C), shift)


def _pool_kernel(prev_ref, cur_ref, next_ref, pw_ref, ps_ref, out_ref, *, tm, seq, Wd):
    i = pl.program_id(0)
    t0 = (i * tm) % seq
    has_prev = t0 > 0
    has_next = t0 + tm < seq
    r = lax.broadcasted_iota(jnp.int32, (tm, tm), 0)
    c = lax.broadcasted_iota(jnp.int32, (tm, tm), 1)
    rh = lax.broadcasted_iota(jnp.int32, (tm, HALO), 0)
    ch = lax.broadcasted_iota(jnp.int32, (tm, HALO), 1)
    t = lax.broadcasted_iota(jnp.int32, (tm, Wd), 0) + t0
    for g, w in enumerate(POOL_WINDOWS):
        lo_off = w // 2
        hi_off = w - lo_off
        sl = slice(g * Wd, (g + 1) * Wd)
        band = jnp.where((c >= r - lo_off) & (c < r + hi_off), 1.0, 0.0).astype(BF16)
        band_prev = jnp.where(has_prev & (ch - HALO >= rh - lo_off), 1.0, 0.0).astype(BF16)
        band_next = jnp.where(has_next & (ch + tm < rh + hi_off), 1.0, 0.0).astype(BF16)
        cur = cur_ref[:, sl]
        wsum = _dot(band, cur) + _dot(band_prev, prev_ref[:, sl]) + _dot(band_next, next_ref[:, sl])
        cnt = jnp.minimum(t + hi_off, seq) - jnp.maximum(t - lo_off, 0)
        pooled = (wsum / cnt.astype(F32) - cur.astype(F32)).astype(BF16)
        out_ref[:, sl] = (_dot(pooled, pw_ref[g]) * ps_ref[:, sl]).astype(out_ref.dtype)


def _pool(proj, col_offset, pool_w, pool_scale, seq):
    M = proj.shape[0]
    PG, Wd, _ = pool_w.shape
    PD = PG * Wd
    assert PG == len(POOL_WINDOWS) and max(POOL_WINDOWS) // 2 <= HALO
    assert col_offset % PD == 0
    off = col_offset // PD
    tm = _pick(seq, (256, 128))
    hb = tm // HALO
    n_halo = M // HALO
    blk = (2 * _nbytes((tm, PD), BF16) + 2 * _nbytes((HALO, PD), BF16) + _nbytes((PG, Wd, Wd), BF16))
    return pl.pallas_call(
        functools.partial(_pool_kernel, tm=tm, seq=seq, Wd=Wd),
        grid=(M // tm,),
        in_specs=[pl.BlockSpec((HALO, PD), lambda i: (jnp.maximum(i * hb - 1, 0), off)),
                  pl.BlockSpec((tm, PD), lambda i: (i, off)),
                  pl.BlockSpec((HALO, PD), lambda i: (jnp.minimum((i + 1) * hb, n_halo - 1), off)),
                  pl.BlockSpec((PG, Wd, Wd), lambda i: (0, 0, 0)),
                  pl.BlockSpec((1, PD), lambda i: (0, 0))],
        out_specs=pl.BlockSpec((tm, PD), lambda i: (i, 0)),
        out_shape=jax.ShapeDtypeStruct((M, PD), BF16),
        compiler_params=_params(("parallel",), blk, 4 * _nbytes((tm, PD), F32)),
        name="pool",
    )(proj, proj, proj, pool_w, pool_scale.reshape(1, PD))


def _cumsum_rows(tri, v):
    a, b, c = _split3(v)
    return _dot(tri, a) + _dot(tri, b) + _dot(tri, c)


def _cumsum_lanes(v, triT):
    a, b, c = _split3(v)
    return _dot(a, triT) + _dot(b, triT) + _dot(c, triT)


def _hilo(v):
    a, b = _split2(v)
    return jnp.concatenate([a, b], axis=1)


def _expand(v_hilo, e2):
    return _dot(v_hilo, e2)


def _ssd_state_kernel(xf_ref, bf_ref, dtf_ref, xb_ref, bb_ref, dtb_ref, bias_ref, alog_ref, e_ref,
                      pf_ref, pb_ref, h_ref, *, H, G, N, Q):
    L = SSD_CHUNK

    @pl.when(pl.program_id(1) == 0)
    def _():
        h_ref[...] = jnp.zeros_like(h_ref)

    row = lax.broadcasted_iota(jnp.int32, (L, L), 0)
    col = lax.broadcasted_iota(jnp.int32, (L, L), 1)
    tri = jnp.where(row >= col, 1.0, 0.0).astype(BF16)
    is_fwd = lax.broadcasted_iota(jnp.int32, (L, 2 * H), 1) < H
    A = -jnp.exp(alog_ref[...])

    def direction(d, x_ref, b_ref, dtraw_ref, p_ref):
        dt = _softplus(dtraw_ref[...] + bias_ref[...])
        a = dt * A
        cum = _cumsum_rows(tri, a)
        total = cum[L - 1:L, :]
        dte = jnp.exp(jnp.where(is_fwd, total - cum, cum - a))
        wexp = _expand(_hilo(dt * dte), e_ref[d])
        xs = (x_ref[...].astype(F32) * wexp).astype(BF16)
        cdexp = _expand(_hilo(jnp.broadcast_to(jnp.exp(total), (8, 2 * H))), e_ref[d])[0:1, :]
        bm = b_ref[...]
        for g in range(G):
            st = _dot_tn(bm[:, g * N:(g + 1) * N], xs[:, g * Q:(g + 1) * Q])
            h_old = h_ref[d, g]
            p_ref[0, 0, g] = h_old.astype(BF16)
            h_ref[d, g] = cdexp[:, g * Q:(g + 1) * Q] * h_old + st

    direction(0, xf_ref, bf_ref, dtf_ref, pf_ref)
    direction(1, xb_ref, bb_ref, dtb_ref, pb_ref)


def _ssd_states(xbc, dtraw, bias, alog, emat, batch, seq, D, H):
    L, G, N = SSD_CHUNK, SSD_GROUPS, SSD_STATE
    GN = G * N
    Q = D // G
    NC = seq // L
    assert D % GN == 0 or GN % D == 0
    b_off = D // GN
    H2 = 2 * H
    blk = 2 * (_nbytes((L, D), BF16) + _nbytes((L, GN), BF16) + _nbytes((L, H2), F32)
               + _nbytes((G, N, Q), BF16)) + _nbytes((2, 2 * H2, D), BF16)
    fwd = lambda b, t: b * NC + t
    bwd = lambda b, t: b * NC + (NC - 1 - t)
    out_sds = jax.ShapeDtypeStruct((batch, NC, G, N, Q), BF16)
    return pl.pallas_call(
        functools.partial(_ssd_state_kernel, H=H, G=G, N=N, Q=Q),
        grid=(batch, NC),
        in_specs=[pl.BlockSpec((L, D), lambda b, t: (fwd(b, t), 0)),
                  pl.BlockSpec((L, GN), lambda b, t: (fwd(b, t), b_off)),
                  pl.BlockSpec((L, H2), lambda b, t: (fwd(b, t), 0)),
                  pl.BlockSpec((L, D), lambda b, t: (bwd(b, t), 0)),
                  pl.BlockSpec((L, GN), lambda b, t: (bwd(b, t), b_off)),
                  pl.BlockSpec((L, H2), lambda b, t: (bwd(b, t), 0)),
                  pl.BlockSpec((1, H2), lambda b, t: (0, 0)),
                  pl.BlockSpec((1, H2), lambda b, t: (0, 0)),
                  pl.BlockSpec((2, 2 * H2, D), lambda b, t: (0, 0, 0))],
        out_specs=[pl.BlockSpec((1, 1, G, N, Q), lambda b, t: (b, t, 0, 0, 0)),
                   pl.BlockSpec((1, 1, G, N, Q), lambda b, t: (b, NC - 1 - t, 0, 0, 0))],
        out_shape=[out_sds, out_sds],
        scratch_shapes=[pltpu.VMEM((2, G, N, Q), F32)],
        compiler_params=_params(("parallel", "arbitrary"), blk,
                                _nbytes((2, G, N, Q), F32) + 4 * _nbytes((L, D), F32)),
        name="ssd_states",
    )(xbc, xbc, dtraw, xbc, xbc, dtraw, bias, alog, emat)


def _ssd_out_kernel(dsk_ref, x_ref, b_ref, c_ref, zs_ref, dt_ref, dtT_ref, pf_ref, pb_ref,
                    bias_ref, alog_ref, biasT_ref, alogT_ref, ng_ref, e_ref, out_ref,
                    *, H, G, N, Q, P):
    L = SSD_CHUNK
    E = H // G
    row = lax.broadcasted_iota(jnp.int32, (L, L), 0)
    col = lax.broadcasted_iota(jnp.int32, (L, L), 1)
    lower = row >= col
    eye = row == col
    tri = jnp.where(lower, 1.0, 0.0).astype(BF16)
    triT = jnp.where(row <= col, 1.0, 0.0).astype(BF16)
    is_fwd = lax.broadcasted_iota(jnp.int32, (L, 2 * H), 1) < H
    is_fwdT = lax.broadcasted_iota(jnp.int32, (2 * H, L), 0) < H

    dt = _softplus(dt_ref[...] + bias_ref[...])
    a = dt * (-jnp.exp(alog_ref[...]))
    cum = _cumsum_rows(tri, a)
    total = cum[L - 1:L, :]
    excl = cum - a
    col2 = jnp.where(is_fwd, cum, excl) * LOG2E
    sdec = _hilo(jnp.exp(jnp.where(is_fwd, cum, total - excl)))

    dtT = _softplus(dtT_ref[...] + biasT_ref[...])
    aT = dtT * (-jnp.exp(alogT_ref[...]))
    cumT = _cumsum_lanes(aT, triT)
    lg = jnp.log2(dtT)
    row2 = jnp.where(is_fwdT, cumT * LOG2E - lg, (cumT - aT) * LOG2E + lg)

    lane = lax.broadcasted_iota(jnp.int32, (L, 2 * P), 1)
    keep_left = jnp.where(lane < P, 1.0, 0.0).astype(BF16)
    keep_right = jnp.where(lane < P, 0.0, 1.0).astype(BF16)

    for g in range(G):
        c_g = c_ref[:, g * N:(g + 1) * N]
        cb = _dot_nt(c_g, b_ref[:, g * N:(g + 1) * N])
        cb_diag = jnp.sum(jnp.where(eye, cb, 0.0), axis=0, keepdims=True)
        pieces = []
        for e2 in range(E // 2):
            ws = []
            for h in (g * E + 2 * e2, g * E + 2 * e2 + 1):
                arg = jnp.where(lower,
                                col2[:, h:h + 1] - row2[h:h + 1, :],
                                row2[H + h:H + h + 1, :] - col2[:, H + h:H + h + 1])
                diag = cb_diag * dtT[H + h:H + h + 1, :] + dsk_ref[h]
                ws.append((cb * jnp.exp2(arg) + jnp.where(eye, diag, 0.0)).astype(BF16))
            h0 = g * E + 2 * e2
            xp = x_ref[:, h0 * P:(h0 + 2) * P]
            rhs = jnp.concatenate([xp * keep_left, xp * keep_right], axis=0)
            pieces.append(_dot(jnp.concatenate(ws, axis=1), rhs))
        y = jnp.concatenate(pieces, axis=1) if len(pieces) > 1 else pieces[0]
        sl = slice(g * Q, (g + 1) * Q)
        y = (y + _dot(c_g, pf_ref[0, 0, g]) * _expand(sdec, e_ref[0, :, sl])
             + _dot(c_g, pb_ref[0, 0, g]) * _expand(sdec, e_ref[1, :, sl]))
        y = y * zs_ref[:, sl].astype(F32)
        ms = jnp.mean(y * y, axis=-1, keepdims=True)
        out_ref[:, sl] = (y * lax.rsqrt(ms + EPS) * ng_ref[:, sl]).astype(out_ref.dtype)


def _ssd_out(zs, xbc, dtraw, dtrawT, prev_f, prev_b, bias, alog, biasT, alogT, d_skip, norm_g, emat,
             batch, seq, D, H):
    L, G, N = SSD_CHUNK, SSD_GROUPS, SSD_STATE
    GN = G * N
    Q = D // G
    P = D // H
    NC = seq // L
    M = batch * seq
    H2 = 2 * H
    assert (H // G) % 2 == 0 and 2 * P == 128 and D % GN == 0
    b_off = D // GN
    blk = (2 * _nbytes((L, D), BF16) + 2 * _nbytes((L, GN), BF16) + 2 * _nbytes((L, H2), F32)
           + 2 * _nbytes((G, N, Q), BF16) + _nbytes((2, 2 * H2, D), BF16) + _nbytes((L, D), BF16))
    rowblk = lambda b, c: b * NC + c
    return pl.pallas_call(
        functools.partial(_ssd_out_kernel, H=H, G=G, N=N, Q=Q, P=P),
        grid=(batch, NC),
        in_specs=[pl.BlockSpec(memory_space=pltpu.SMEM),
                  pl.BlockSpec((L, D), lambda b, c: (rowblk(b, c), 0)),
                  pl.BlockSpec((L, GN), lambda b, c: (rowblk(b, c), b_off)),
                  pl.BlockSpec((L, GN), lambda b, c: (rowblk(b, c), b_off + 1)),
                  pl.BlockSpec((L, D), lambda b, c: (rowblk(b, c), 0)),
                  pl.BlockSpec((L, H2), lambda b, c: (rowblk(b, c), 0)),
                  pl.BlockSpec((H2, L), lambda b, c: (0, rowblk(b, c))),
                  pl.BlockSpec((1, 1, G, N, Q), lambda b, c: (b, c, 0, 0, 0)),
                  pl.BlockSpec((1, 1, G, N, Q), lambda b, c: (b, c, 0, 0, 0)),
                  pl.BlockSpec((1, H2), lambda b, c: (0, 0)),
                  pl.BlockSpec((1, H2), lambda b, c: (0, 0)),
                  pl.BlockSpec((H2, L), lambda b, c: (0, 0)),
                  pl.BlockSpec((H2, L), lambda b, c: (0, 0)),
                  pl.BlockSpec((1, D), lambda b, c: (0, 0)),
                  pl.BlockSpec((2, 2 * H2, D), lambda b, c: (0, 0, 0))],
        out_specs=pl.BlockSpec((L, D), lambda b, c: (rowblk(b, c), 0)),
        out_shape=jax.ShapeDtypeStruct((M, D), BF16),
        compiler_params=_params(("parallel", "parallel"), blk, 8 * _nbytes((L, D), F32)),
        name="ssd_out",
    )(d_skip, xbc, xbc, xbc, zs, dtraw, dtrawT, prev_f, prev_b, bias, alog, biasT, alogT,
      norm_g.reshape(1, D), emat)


def _prepare_layer(norm_mix_g, w_in, conv_w, conv_b, dt_bias_f, dt_bias_b, a_log_f, a_log_b, d_skip,
                   ssd_norm_g, w_ssd_up, pool_w, pool_scale, w_pool_up, w_out, norm_mlp_g, w_ff1, w_ff2,
                   norm_ple_g, w_ple_gate, w_ple_proj):
    D = w_in.shape[0]
    H = d_skip.shape[0]
    C = conv_w.shape[1]
    s_xbc = D + C
    s_dt = s_xbc + 2 * H
    w_dt = w_in[:, s_xbc:s_dt].astype(BF16)
    head_of_lane = jnp.arange(D, dtype=jnp.int32) // (D // H)
    onehot = (jnp.arange(H, dtype=jnp.int32)[:, None] == head_of_lane[None, :]).astype(BF16)
    zeros = jnp.zeros_like(onehot)
    L = SSD_CHUNK
    bias = jnp.concatenate([dt_bias_f, dt_bias_b])
    alog = jnp.concatenate([a_log_f, a_log_b])
    return dict(
        D=D, H=H, C=C,
        norm_mix_g=norm_mix_g,
        w_in=w_in.astype(BF16),
        w_tail=w_in[:, s_dt:].astype(BF16),
        w_dt=w_dt, w_dtT=w_dt.T,
        conv_w=conv_w, conv_b=conv_b,
        bias=bias.reshape(1, 2 * H), alog=alog.reshape(1, 2 * H),
        biasT=jnp.broadcast_to(bias[:, None], (2 * H, L)), alogT=jnp.broadcast_to(alog[:, None], (2 * H, L)),
        emat=jnp.stack([jnp.concatenate([onehot, zeros, onehot, zeros]),
                        jnp.concatenate([zeros, onehot, zeros, onehot])]),
        d_skip=d_skip, ssd_norm_g=ssd_norm_g,
        w_ssd_up=w_ssd_up.astype(BF16),
        pool_w=pool_w.astype(BF16), pool_scale=pool_scale, w_pool_up=w_pool_up.astype(BF16),
        w_out=w_out.astype(BF16), norm_mlp_g=norm_mlp_g,
        w_ff1=w_ff1.astype(BF16), w_ff2=w_ff2.astype(BF16),
        norm_ple_g=norm_ple_g, w_ple_gate=w_ple_gate.astype(BF16), w_ple_proj=w_ple_proj.astype(BF16),
    )


def _layer(x, p_l, lp, batch, seq):
    D, H = lp["D"], lp["H"]
    PD = lp["w_pool_up"].shape[0]
    off_ga, off_gb = PD, PD + D

    h, dtraw, dtrawT = _norm_dt(x, lp["norm_mix_g"], lp["w_dt"], lp["w_dtT"])
    zs = _matmul(h, lp["w_in"], BF16, "in_proj_z", epilogue=lambda acc: acc * _sigmoid(acc), tn=_TN_WIDE,
                 w_cols=(0, D))
    tail = _matmul(h, lp["w_tail"], BF16, "in_proj_tail", tn=_TN_WIDE)

    xbc_raw = _matmul(h, lp["w_in"], BF16, "in_proj_xbc", tn=_TN_WIDE, w_cols=(D, lp["C"]))
    xbc = _conv_silu(xbc_raw, 0, lp["conv_w"], lp["conv_b"], seq)
    prev_f, prev_b = _ssd_states(xbc, dtraw, lp["bias"], lp["alog"], lp["emat"], batch, seq, D, H)
    y = _ssd_out(zs, xbc, dtraw, dtrawT, prev_f, prev_b, lp["bias"], lp["alog"], lp["biasT"],
                 lp["alogT"], lp["d_skip"], lp["ssd_norm_g"], lp["emat"], batch, seq, D, H)
    gated_a = _matmul(y, lp["w_ssd_up"], BF16, "ssd_up", tn=_TN_WIDE,
                      epilogue=lambda acc, g_ref: _sigmoid(g_ref[...].astype(F32)) * acc,
                      extras=(tail,), col_offsets=(off_ga,),
                      extra_specs=lambda tm, tn: [_tile_spec(off_ga)(tm, tn)])

    pooled = _pool(tail, 0, lp["pool_w"], lp["pool_scale"], seq)
    mixed = _matmul(pooled, lp["w_pool_up"], BF16, "pool_up", tn=_TN_WIDE,
                    epilogue=lambda acc, g_ref, a_ref: (_sigmoid(g_ref[...].astype(F32)) * acc
                                                        + a_ref[...].astype(F32)),
                    extras=(tail, gated_a), col_offsets=(off_gb,),
                    extra_specs=lambda tm, tn: [_tile_spec(off_gb)(tm, tn), _tile_spec(0)(tm, tn)])
    x, xg, ssq = _matmul_residual_norm(mixed, lp["w_out"], x, lp["norm_mlp_g"], "out_proj")

    ff = _matmul(xg, lp["w_ff1"], BF16, "ff1", ssq=ssq, tn=_TN_WIDE,
                 epilogue=lambda acc: jnp.square(jnp.maximum(acc, 0.0)))
    x, xg, ssq = _matmul_kacc_residual_norm(ff, lp["w_ff2"], x, lp["norm_ple_g"], "ff2")

    ple = p_l.shape[1]
    x = _matmul(xg, lp["w_ple_gate"], F32, "ple", ssq=ssq,
                epilogue=lambda acc, x_ref, p_ref, wp_ref: (
                    x_ref[...] + _sigmoid(acc) * _dot(p_ref[...].astype(BF16), wp_ref[...])),
                extras=(x, p_l, lp["w_ple_proj"]),
                extra_specs=lambda tm, tn: [_tile_spec(0)(tm, tn),
                                            pl.BlockSpec((tm, ple), lambda i, j: (i, 0)),
                                            pl.BlockSpec((ple, tn), lambda i, j: (0, j))])
    return x


def _trunk(x, p, layers, norm_final_g):
    batch, seq, D = x.shape
    assert seq % SSD_CHUNK == 0
    xf = x.reshape(batch * seq, D)
    for i, lp in enumerate(layers):
        xf = _layer(xf, p[i].reshape(batch * seq, -1), lp, batch, seq)
    return _norm(xf, norm_final_g, F32, "norm_final").reshape(batch, seq, D)


def kernel(x_prompt, x_sample, p_prompt, p_sample, norm_mix_g, w_in, conv_w, conv_b, dt_bias_f, dt_bias_b, a_log_f, a_log_b, d_skip, ssd_norm_g, w_ssd_up, pool_w, pool_scale, w_pool_up, w_out, norm_mlp_g, w_ff1, w_ff2, norm_ple_g, w_ple_gate, w_ple_proj, norm_final_g):
    stacked = (norm_mix_g, w_in, conv_w, conv_b, dt_bias_f, dt_bias_b, a_log_f, a_log_b, d_skip,
               ssd_norm_g, w_ssd_up, pool_w, pool_scale, w_pool_up, w_out, norm_mlp_g, w_ff1, w_ff2,
               norm_ple_g, w_ple_gate, w_ple_proj)
    layers = [_prepare_layer(*[w[i] for w in stacked]) for i in range(w_in.shape[0])]
    y_prompt = _trunk(x_prompt, p_prompt, layers, norm_final_g)
    y_sample = _trunk(x_sample, p_sample, layers, norm_final_g)
    return (y_prompt, y_sample)
```

```python
import functools
import math

import jax
import jax.numpy as jnp
from jax import lax
from jax.experimental import pallas as pl
from jax.experimental.pallas import tpu as pltpu

F32 = jnp.float32
BF16 = jnp.bfloat16

EPS = 1e-6
LOG2E = 1.4426950408889634
LN2 = 0.6931471805599453
SSD_GROUPS = 8
SSD_STATE = 128
SSD_CHUNK = 128
POOL_WINDOWS = (2, 4, 8, 16)

V7X_VMEM_BYTES = 64 * 1024 * 1024
V7X_VMEM_RESERVE_BYTES = 8 * 1024 * 1024
BF16_SUBLANE_TILE = 16
LANES = 128
HALO = BF16_SUBLANE_TILE
_TN_WIDE = 1024
CONV_SUB_ROWS = 128


def _pick(dim, prefs):
    for p in prefs:
        if dim % p == 0:
            return p
    return dim


def _nbytes(shape, dtype):
    n = 1
    for s in shape:
        n *= s
    return n * jnp.dtype(dtype).itemsize


def _params(semantics, block_bytes, scratch_bytes=0):
    est = 2 * block_bytes + scratch_bytes
    limit = min(2 * est + (4 << 20), V7X_VMEM_BYTES - V7X_VMEM_RESERVE_BYTES)
    return pltpu.CompilerParams(dimension_semantics=semantics, vmem_limit_bytes=int(limit))


def _sigmoid(v):
    return 1.0 / (1.0 + jnp.exp(-v))


def _softplus(v):
    u = jnp.exp(-jnp.abs(v))
    w = 1.0 + u
    return jnp.maximum(v, 0.0) + jnp.where(w == 1.0, u, (jnp.log2(w) * LN2) * (u / (w - 1.0)))


def _split3(v):
    a = v.astype(BF16)
    r = v - a.astype(F32)
    b = r.astype(BF16)
    c = (r - b.astype(F32)).astype(BF16)
    return a, b, c


def _split2(v):
    a = v.astype(BF16)
    b = (v - a.astype(F32)).astype(BF16)
    return a, b


def _dot(a, b):
    return jnp.dot(a, b, preferred_element_type=F32)


def _dot_nt(a, b):
    return lax.dot_general(a, b, (((1,), (1,)), ((), ())), preferred_element_type=F32)


def _dot_tn(a, b):
    return lax.dot_general(a, b, (((0,), (0,)), ((), ())), preferred_element_type=F32)


def _rms(x, g):
    ms = jnp.mean(x * x, axis=-1, keepdims=True)
    return x * lax.rsqrt(ms + EPS) * g


def _part_tiles(parts, tm):
    return [p.shape[0] // tm for p in parts]


def _part_specs(parts, tm, ncols, col_index=None):
    specs, start = [], 0
    for n in _part_tiles(parts, tm):
        def index(*idx, s=start, n=n):
            row = idx[0] - s
            if col_index is None:
                return jnp.clip(row, 0, n - 1), 0
            return jnp.clip(row, 0, n - 1), jnp.where((row >= 0) & (row < n), col_index(*idx), 0)
        specs.append(pl.BlockSpec((tm, ncols), index))
        start += n
    return specs


def _part_read(refs, tiles, i):
    val = refs[-1][...]
    end = sum(tiles[:-1])
    for ref, n in zip(reversed(refs[:-1]), reversed(tiles[:-1])):
        val = jnp.where(i < end, ref[...], val)
        end -= n
    return val


def _norm_dt_kernel(*refs, tiles):
    n = len(tiles)
    x_refs, (g_ref, wdt_ref, wdtT_ref, h_ref, dt_ref, dtT_ref) = refs[:n], refs[n:]
    h = _rms(_part_read(x_refs, tiles, pl.program_id(0)), g_ref[...]).astype(BF16)
    h_ref[...] = h
    dt_ref[...] = _dot(h, wdt_ref[...])
    dtT_ref[...] = _dot_nt(wdtT_ref[...], h)


def _norm_dt(x_parts, g, w_dt, w_dtT):
    D = x_parts[0].shape[1]
    M = sum(p.shape[0] for p in x_parts)
    H2 = w_dt.shape[1]
    tm = _pick(functools.reduce(math.gcd, [p.shape[0] for p in x_parts]), (512, 256, 128))
    blk = (len(x_parts) * _nbytes((tm, D), F32) + _nbytes((tm, D), BF16) + 2 * _nbytes((D, H2), BF16)
           + 2 * _nbytes((tm, H2), F32))
    return pl.pallas_call(
        functools.partial(_norm_dt_kernel, tiles=_part_tiles(x_parts, tm)),
        grid=(M // tm,),
        in_specs=_part_specs(x_parts, tm, D) + [
            pl.BlockSpec((1, D), lambda i: (0, 0)),
            pl.BlockSpec((D, H2), lambda i: (0, 0)),
            pl.BlockSpec((H2, D), lambda i: (0, 0))],
        out_specs=[pl.BlockSpec((tm, D), lambda i: (i, 0)),
                   pl.BlockSpec((tm, H2), lambda i: (i, 0)),
                   pl.BlockSpec((H2, tm), lambda i: (0, i))],
        out_shape=[jax.ShapeDtypeStruct((M, D), BF16),
                   jax.ShapeDtypeStruct((M, H2), F32),
                   jax.ShapeDtypeStruct((H2, M), F32)],
        compiler_params=_params(("arbitrary",), blk),
        name="norm_dt",
    )(*x_parts, g.reshape(1, D), w_dt, w_dtT)


def _norm_final_kernel(x_ref, g_ref, *o_refs, tiles):
    i = pl.program_id(0)
    y = _rms(x_ref[...], g_ref[...])
    start = 0
    for o_ref, n in zip(o_refs, tiles):
        @pl.when((i >= start) & (i < start + n))
        def _(o_ref=o_ref):
            o_ref[...] = y
        start += n


def _norm_final(x, g, part_rows):
    M, D = x.shape
    tm = _pick(functools.reduce(math.gcd, part_rows), (512, 256, 128))
    outs = [jax.ShapeDtypeStruct((r, D), F32) for r in part_rows]
    blk = (1 + len(part_rows)) * _nbytes((tm, D), F32)
    return pl.pallas_call(
        functools.partial(_norm_final_kernel, tiles=_part_tiles(outs, tm)),
        grid=(M // tm,),
        in_specs=[pl.BlockSpec((tm, D), lambda i: (i, 0)),
                  pl.BlockSpec((1, D), lambda i: (0, 0))],
        out_specs=_part_specs(outs, tm, D),
        out_shape=outs,
        compiler_params=_params(("arbitrary",), blk),
        name="norm_final",
    )(x, g.reshape(1, D))


def _row_rsqrt(ssq_ref, d):
    return lax.rsqrt(jnp.sum(ssq_ref[...], axis=-1, keepdims=True) * (1.0 / d) + EPS)


def _lane_partial_sumsq(v):
    sq = v * v
    part = sq[:, 0:LANES]
    for q in range(1, v.shape[1] // LANES):
        part = part + sq[:, q * LANES:(q + 1) * LANES]
    return part


def _mm_kernel(lhs_ref, w_ref, *rest, epilogue, norm_width):
    out_ref = rest[-1]
    acc = _dot(lhs_ref[...], w_ref[...])
    if norm_width:
        acc = acc * _row_rsqrt(rest[0], norm_width)
        rest = rest[1:]
    out_ref[...] = epilogue(acc, *rest[:-1]).astype(out_ref.dtype)


def _matmul(lhs, w, out_dtype, name, epilogue=None, extras=(), extra_specs=None, ssq=None, tm=None, tn=None,
            col_offsets=(), w_cols=None):
    M, K = lhs.shape
    w_off, N = w_cols if w_cols else (0, w.shape[1])
    tm = tm or _pick(M, (1024, 512, 256, 128))
    tn = next(t for t in ((tn,) if tn else ()) + (512, 256, 128)
              if N % t == 0 and all(o % t == 0 for o in col_offsets))
    if epilogue is None:
        epilogue = lambda acc: acc
    specs = extra_specs(tm, tn) if extra_specs else []
    if ssq is not None:
        extras = (ssq,) + tuple(extras)
        specs = [pl.BlockSpec((tm, LANES), lambda i, j: (i, 0))] + specs
    blk = (_nbytes((tm, K), lhs.dtype) + _nbytes((K, tn), w.dtype) + _nbytes((tm, tn), out_dtype)
           + sum(_nbytes(s.block_shape, e.dtype) for s, e in zip(specs, extras)))
    if w_off % tn == 0:
        w_blk0 = w_off // tn
        w_spec = pl.BlockSpec((K, tn), lambda i, j: (0, w_blk0 + j))
    elif w_off % LANES == 0:
        w_spec = pl.BlockSpec((pl.Element(K), pl.Element(tn)),
                              lambda i, j: (0, pl.multiple_of(w_off + j * tn, LANES)))
    else:
        w = w[:, w_off:w_off + N]
        w_spec = pl.BlockSpec((K, tn), lambda i, j: (0, j))
    return pl.pallas_call(
        functools.partial(_mm_kernel, epilogue=epilogue, norm_width=K if ssq is not None else 0),
        grid=(M // tm, N // tn),
        in_specs=[pl.BlockSpec((tm, K), lambda i, j: (i, 0)), w_spec] + specs,
        out_specs=pl.BlockSpec((tm, tn), lambda i, j: (i, j)),
        out_shape=jax.ShapeDtypeStruct((M, N), out_dtype),
        compiler_params=_params(("parallel", "arbitrary"), blk, _nbytes((tm, tn), F32)),
        name=name,
    )(lhs, w, *extras)


def _tile_spec(col_offset):
    def make(tm, tn):
        assert col_offset % tn == 0
        off = col_offset // tn
        return pl.BlockSpec((tm, tn), lambda i, j: (i, off + j))
    return make


def _emit_norm_operand(x_new, g_ref, xg_ref, ssq_ref, first):
    xg_ref[...] = (x_new * g_ref[...]).astype(xg_ref.dtype)
    part = _lane_partial_sumsq(x_new)

    @pl.when(first)
    def _():
        ssq_ref[...] = part

    @pl.when(jnp.logical_not(first))
    def _():
        ssq_ref[...] += part


def _mm_res_norm_kernel(lhs_ref, w_ref, *refs, tiles):
    n = len(tiles)
    res_refs, (g_ref, x_ref, xg_ref, ssq_ref) = refs[:n], refs[n:]
    x_new = _part_read(res_refs, tiles, pl.program_id(0)) + _dot(lhs_ref[...], w_ref[...])
    x_ref[...] = x_new
    _emit_norm_operand(x_new, g_ref, xg_ref, ssq_ref, pl.program_id(1) == 0)


def _matmul_residual_norm(lhs, w, res_parts, g_next, name):
    M, K = lhs.shape
    N = w.shape[1]
    tm = _pick(functools.reduce(math.gcd, [p.shape[0] for p in res_parts]), (1024, 512, 256, 128))
    tn = _pick(N, (512, 256, 128))
    blk = (_nbytes((tm, K), lhs.dtype) + _nbytes((K, tn), w.dtype) + (1 + len(res_parts)) * _nbytes((tm, tn), F32)
           + _nbytes((tm, tn), BF16) + _nbytes((tm, LANES), F32))
    return pl.pallas_call(
        functools.partial(_mm_res_norm_kernel, tiles=_part_tiles(res_parts, tm)),
        grid=(M // tm, N // tn),
        in_specs=[pl.BlockSpec((tm, K), lambda i, j: (i, 0)),
                  pl.BlockSpec((K, tn), lambda i, j: (0, j))]
        + _part_specs(res_parts, tm, tn, col_index=lambda i, j: j)
        + [pl.BlockSpec((1, tn), lambda i, j: (0, j))],
        out_specs=[pl.BlockSpec((tm, tn), lambda i, j: (i, j)),
                   pl.BlockSpec((tm, tn), lambda i, j: (i, j)),
                   pl.BlockSpec((tm, LANES), lambda i, j: (i, 0))],
        out_shape=[jax.ShapeDtypeStruct((M, N), F32),
                   jax.ShapeDtypeStruct((M, N), BF16),
                   jax.ShapeDtypeStruct((M, LANES), F32)],
        compiler_params=_params(("arbitrary", "arbitrary"), blk, _nbytes((tm, tn), F32)),
        name=name,
    )(lhs, w, *res_parts, g_next.reshape(1, N))


def _mm_kacc_kernel(lhs_ref, w_ref, res_ref, g_ref, x_ref, xg_ref, ssq_ref, *, nk):
    k = pl.program_id(2)

    def step(base_ref, finish):
        x_new = base_ref[...] + _dot(lhs_ref[...], w_ref[...])
        x_ref[...] = x_new
        if finish:
            _emit_norm_operand(x_new, g_ref, xg_ref, ssq_ref, pl.program_id(1) == 0)

    if nk == 1:
        step(res_ref, True)
        return
    pl.when(k == 0)(lambda: step(res_ref, False))
    if nk > 2:
        pl.when((k > 0) & (k < nk - 1))(lambda: step(x_ref, False))
    pl.when(k == nk - 1)(lambda: step(x_ref, True))


def _matmul_kacc_residual_norm(lhs, w, res, g_next, name):
    M, K = lhs.shape
    N = w.shape[1]
    tm = _pick(M, (1024, 512, 256, 128))
    tn = _pick(N, (1024, 512, 256, 128))
    tk = _pick(K, (2048, 1024, 512, 256, 128))
    blk = (_nbytes((tm, tk), lhs.dtype) + _nbytes((tk, tn), w.dtype) + 2 * _nbytes((tm, tn), F32)
           + _nbytes((tm, tn), BF16) + _nbytes((tm, LANES), F32))
    return pl.pallas_call(
        functools.partial(_mm_kacc_kernel, nk=K // tk),
        grid=(M // tm, N // tn, K // tk),
        in_specs=[pl.BlockSpec((tm, tk), lambda i, j, k: (i, k)),
                  pl.BlockSpec((tk, tn), lambda i, j, k: (k, j)),
                  pl.BlockSpec((tm, tn), lambda i, j, k: (i, j)),
                  pl.BlockSpec((1, tn), lambda i, j, k: (0, j))],
        out_specs=[pl.BlockSpec((tm, tn), lambda i, j, k: (i, j)),
                   pl.BlockSpec((tm, tn), lambda i, j, k: (i, j)),
                   pl.BlockSpec((tm, LANES), lambda i, j, k: (i, 0))],
        out_shape=[jax.ShapeDtypeStruct((M, N), F32),
                   jax.ShapeDtypeStruct((M, N), BF16),
                   jax.ShapeDtypeStruct((M, LANES), F32)],
        compiler_params=_params(("parallel", "arbitrary", "arbitrary"), blk, _nbytes((tm, tn), F32)),
        name=name,
    )(lhs, w, res, g_next.reshape(1, N))


def _conv_kernel(prev_ref, cur_ref, next_ref, w_ref, b_ref, shift_ref, out_ref, buf_ref, *, tm, seq_tiles, width):
    i = pl.program_id(0)
    first = (i % seq_tiles) == 0
    last = (i % seq_tiles) == seq_tiles - 1
    prev = prev_ref[...]
    nxt = next_ref[...]
    buf_ref[0:HALO, :] = jnp.where(first, jnp.zeros_like(prev), prev)
    buf_ref[HALO:HALO + tm, :] = cur_ref[...]
    buf_ref[HALO + tm:, :] = jnp.where(last, jnp.zeros_like(nxt), nxt)

    half = width // 2
    taps = [k for k in range(width) if k != half]
    sub = shift_ref.shape[0] // len(taps)
    win = sub + 2 * HALO
    shift = shift_ref[...]
    for r0 in range(0, tm, sub):
        window = buf_ref[r0:r0 + win, :]
        shifted = _dot(shift, window)
        acc = b_ref[...] + w_ref[half:half + 1, :] * window[HALO:HALO + sub].astype(F32)
        for n, k in enumerate(taps):
            acc = acc + w_ref[k:k + 1, :] * shifted[n * sub:(n + 1) * sub]
        out_ref[r0:r0 + sub, :] = (acc * _sigmoid(acc)).astype(out_ref.dtype)


def _conv_silu(proj, col_offset, conv_w, conv_b, seq):
    M = proj.shape[0]
    width, C = conv_w.shape
    assert width // 2 <= HALO
    tm = _pick(seq, (512, 256, 128))
    tc = _pick(C, (2048, 1024, 512, 256, 128))
    while col_offset % tc:
        tc //= 2
    off = col_offset // tc
    hb = tm // HALO
    n_halo = M // HALO
    half = width // 2
    sub = min(tm, CONV_SUB_ROWS)
    win = sub + 2 * HALO
    rows = jnp.arange(sub, dtype=jnp.int32)[:, None]
    cols = jnp.arange(win, dtype=jnp.int32)[None, :]
    shift = jnp.concatenate([(cols == rows + HALO + (k - half)) for k in range(width) if k != half],
                            axis=0).astype(BF16)
    blk = (2 * _nbytes((tm, tc), BF16) + 2 * _nbytes((HALO, tc), BF16) + _nbytes((width + 1, tc), F32)
           + _nbytes(shift.shape, BF16))
    return pl.pallas_call(
        functools.partial(_conv_kernel, tm=tm, seq_tiles=seq // tm, width=width),
        grid=(M // tm, C // tc),
        in_specs=[pl.BlockSpec((HALO, tc), lambda i, j: (jnp.maximum(i * hb - 1, 0), off + j)),
                  pl.BlockSpec((tm, tc), lambda i, j: (i, off + j)),
                  pl.BlockSpec((HALO, tc), lambda i, j: (jnp.minimum((i + 1) * hb, n_halo - 1), off + j)),
                  pl.BlockSpec((width, tc), lambda i, j: (0, j)),
                  pl.BlockSpec((1, tc), lambda i, j: (0, j)),
                  pl.BlockSpec(shift.shape, lambda i, j: (0, 0))],
        out_specs=pl.BlockSpec((tm, tc), lambda i, j: (i, j)),
        out_shape=jax.ShapeDtypeStruct((M, C), BF16),
        scratch_shapes=[pltpu.VMEM((tm + 2 * HALO, tc), BF16)],
        compiler_params=_params(("parallel", "parallel"), blk,
                                _nbytes((tm + 2 * HALO, tc), BF16) + 8 * _nbytes((CONV_SUB_ROWS, tc), F32)),
        name="conv_silu",
    )(proj, proj, proj, conv_w, conv_b.reshape(1, C), shift)


def _pool_kernel(prev_ref, cur_ref, next_ref, pw_ref, ps_ref, out_ref, *, tm, seq, Wd):
    i = pl.program_id(0)
    t0 = (i * tm) % seq
    has_prev = t0 > 0
    has_next = t0 + tm < seq
    r = lax.broadcasted_iota(jnp.int32, (tm, tm), 0)
    c = lax.broadcasted_iota(jnp.int32, (tm, tm), 1)
    rh = lax.broadcasted_iota(jnp.int32, (tm, HALO), 0)
    ch = lax.broadcasted_iota(jnp.int32, (tm, HALO), 1)
    t = lax.broadcasted_iota(jnp.int32, (tm, Wd), 0) + t0
    for g, w in enumerate(POOL_WINDOWS):
        lo_off = w // 2
        hi_off = w - lo_off
        sl = slice(g * Wd, (g + 1) * Wd)
        band = jnp.where((c >= r - lo_off) & (c < r + hi_off), 1.0, 0.0).astype(BF16)
        band_prev = jnp.where(has_prev & (ch - HALO >= rh - lo_off), 1.0, 0.0).astype(BF16)
        band_next = jnp.where(has_next & (ch + tm < rh + hi_off), 1.0, 0.0).astype(BF16)
        cur = cur_ref[:, sl]
        wsum = _dot(band, cur) + _dot(band_prev, prev_ref[:, sl]) + _dot(band_next, next_ref[:, sl])
        cnt = jnp.minimum(t + hi_off, seq) - jnp.maximum(t - lo_off, 0)
        pooled = (wsum / cnt.astype(F32) - cur.astype(F32)).astype(BF16)
        out_ref[:, sl] = (_dot(pooled, pw_ref[g]) * ps_ref[:, sl]).astype(out_ref.dtype)


def _pool(proj, col_offset, pool_w, pool_scale, seq):
    M = proj.shape[0]
    PG, Wd, _ = pool_w.shape
    PD = PG * Wd
    assert PG == len(POOL_WINDOWS) and max(POOL_WINDOWS) // 2 <= HALO
    assert col_offset % PD == 0
    off = col_offset // PD
    tm = _pick(seq, (256, 128))
    hb = tm // HALO
    n_halo = M // HALO
    blk = (2 * _nbytes((tm, PD), BF16) + 2 * _nbytes((HALO, PD), BF16) + _nbytes((PG, Wd, Wd), BF16))
    return pl.pallas_call(
        functools.partial(_pool_kernel, tm=tm, seq=seq, Wd=Wd),
        grid=(M // tm,),
        in_specs=[pl.BlockSpec((HALO, PD), lambda i: (jnp.maximum(i * hb - 1, 0), off)),
                  pl.BlockSpec((tm, PD), lambda i: (i, off)),
                  pl.BlockSpec((HALO, PD), lambda i: (jnp.minimum((i + 1) * hb, n_halo - 1), off)),
                  pl.BlockSpec((PG, Wd, Wd), lambda i: (0, 0, 0)),
                  pl.BlockSpec((1, PD), lambda i: (0, 0))],
        out_specs=pl.BlockSpec((tm, PD), lambda i: (i, 0)),
        out_shape=jax.ShapeDtypeStruct((M, PD), BF16),
        compiler_params=_params(("parallel",), blk, 4 * _nbytes((tm, PD), F32)),
        name="pool",
    )(proj, proj, proj, pool_w, pool_scale.reshape(1, PD))


def _cumsum_rows(tri, v):
    a, b, c = _split3(v)
    return _dot(tri, a) + _dot(tri, b) + _dot(tri, c)


def _cumsum_lanes(v, triT):
    a, b, c = _split3(v)
    return _dot(a, triT) + _dot(b, triT) + _dot(c, triT)


def _hilo(v):
    a, b = _split2(v)
    return jnp.concatenate([a, b], axis=1)


def _expand(v_hilo, e2):
    return _dot(v_hilo, e2)


def _ssd_state_kernel(xf_ref, bf_ref, dtf_ref, xb_ref, bb_ref, dtb_ref, bias_ref, alog_ref, e_ref,
                      pf_ref, pb_ref, h_ref, *, H, G, N, Q):
    L = SSD_CHUNK

    @pl.when(pl.program_id(1) == 0)
    def _():
        h_ref[...] = jnp.zeros_like(h_ref)

    row = lax.broadcasted_iota(jnp.int32, (L, L), 0)
    col = lax.broadcasted_iota(jnp.int32, (L, L), 1)
    tri = jnp.where(row >= col, 1.0, 0.0).astype(BF16)
    is_fwd = lax.broadcasted_iota(jnp.int32, (L, 2 * H), 1) < H
    A = -jnp.exp(alog_ref[...])

    def direction(d, x_ref, b_ref, dtraw_ref, p_ref):
        dt = _softplus(dtraw_ref[...] + bias_ref[...])
        a = dt * A
        cum = _cumsum_rows(tri, a)
        total = cum[L - 1:L, :]
        dte = jnp.exp(jnp.where(is_fwd, total - cum, cum - a))
        wexp = _expand(_hilo(dt * dte), e_ref[d])
        xs = (x_ref[...].astype(F32) * wexp).astype(BF16)
        cdexp = _expand(_hilo(jnp.broadcast_to(jnp.exp(total), (8, 2 * H))), e_ref[d])[0:1, :]
        bm = b_ref[...]
        for g in range(G):
            st = _dot_tn(bm[:, g * N:(g + 1) * N], xs[:, g * Q:(g + 1) * Q])
            h_old = h_ref[d, g]
            p_ref[0, 0, g] = h_old.astype(BF16)
            h_ref[d, g] = cdexp[:, g * Q:(g + 1) * Q] * h_old + st

    direction(0, xf_ref, bf_ref, dtf_ref, pf_ref)
    direction(1, xb_ref, bb_ref, dtb_ref, pb_ref)


def _ssd_states(xbc, dtraw, bias, alog, emat, batch, seq, D, H):
    L, G, N = SSD_CHUNK, SSD_GROUPS, SSD_STATE
    GN = G * N
    Q = D // G
    NC = seq // L
    assert D % GN == 0 or GN % D == 0
    b_off = D // GN
    H2 = 2 * H
    blk = 2 * (_nbytes((L, D), BF16) + _nbytes((L, GN), BF16) + _nbytes((L, H2), F32)
               + _nbytes((G, N, Q), BF16)) + _nbytes((2, 2 * H2, D), BF16)
    fwd = lambda b, t: b * NC + t
    bwd = lambda b, t: b * NC + (NC - 1 - t)
    out_sds = jax.ShapeDtypeStruct((batch, NC, G, N, Q), BF16)
    return pl.pallas_call(
        functools.partial(_ssd_state_kernel, H=H, G=G, N=N, Q=Q),
        grid=(batch, NC),
        in_specs=[pl.BlockSpec((L, D), lambda b, t: (fwd(b, t), 0)),
                  pl.BlockSpec((L, GN), lambda b, t: (fwd(b, t), b_off)),
                  pl.BlockSpec((L, H2), lambda b, t: (fwd(b, t), 0)),
                  pl.BlockSpec((L, D), lambda b, t: (bwd(b, t), 0)),
                  pl.BlockSpec((L, GN), lambda b, t: (bwd(b, t), b_off)),
                  pl.BlockSpec((L, H2), lambda b, t: (bwd(b, t), 0)),
                  pl.BlockSpec((1, H2), lambda b, t: (0, 0)),
                  pl.BlockSpec((1, H2), lambda b, t: (0, 0)),
                  pl.BlockSpec((2, 2 * H2, D), lambda b, t: (0, 0, 0))],
        out_specs=[pl.BlockSpec((1, 1, G, N, Q), lambda b, t: (b, t, 0, 0, 0)),
                   pl.BlockSpec((1, 1, G, N, Q), lambda b, t: (b, NC - 1 - t, 0, 0, 0))],
        out_shape=[out_sds, out_sds],
        scratch_shapes=[pltpu.VMEM((2, G, N, Q), F32)],
        compiler_params=_params(("parallel", "arbitrary"), blk,
                                _nbytes((2, G, N, Q), F32) + 4 * _nbytes((L, D), F32)),
        name="ssd_states",
    )(xbc, xbc, dtraw, xbc, xbc, dtraw, bias, alog, emat)


def _ssd_out_kernel(dsk_ref, x_ref, b_ref, c_ref, zs_ref, dt_ref, dtT_ref, pf_ref, pb_ref,
                    bias_ref, alog_ref, biasT_ref, alogT_ref, ng_ref, e_ref, out_ref,
                    *, H, G, N, Q, P):
    L = SSD_CHUNK
    E = H // G
    row = lax.broadcasted_iota(jnp.int32, (L, L), 0)
    col = lax.broadcasted_iota(jnp.int32, (L, L), 1)
    lower = row >= col
    eye = row == col
    tri = jnp.where(lower, 1.0, 0.0).astype(BF16)
    triT = jnp.where(row <= col, 1.0, 0.0).astype(BF16)
    is_fwd = lax.broadcasted_iota(jnp.int32, (L, 2 * H), 1) < H
    is_fwdT = lax.broadcasted_iota(jnp.int32, (2 * H, L), 0) < H

    dt = _softplus(dt_ref[...] + bias_ref[...])
    a = dt * (-jnp.exp(alog_ref[...]))
    cum = _cumsum_rows(tri, a)
    total = cum[L - 1:L, :]
    excl = cum - a
    col2 = jnp.where(is_fwd, cum, excl) * LOG2E
    sdec = _hilo(jnp.exp(jnp.where(is_fwd, cum, total - excl)))

    dtT = _softplus(dtT_ref[...] + biasT_ref[...])
    aT = dtT * (-jnp.exp(alogT_ref[...]))
    cumT = _cumsum_lanes(aT, triT)
    lg = jnp.log2(dtT)
    row2 = jnp.where(is_fwdT, cumT * LOG2E - lg, (cumT - aT) * LOG2E + lg)

    lane = lax.broadcasted_iota(jnp.int32, (L, 2 * P), 1)
    keep_left = jnp.where(lane < P, 1.0, 0.0).astype(BF16)
    keep_right = jnp.where(lane < P, 0.0, 1.0).astype(BF16)

    for g in range(G):
        c_g = c_ref[:, g * N:(g + 1) * N]
        cb = _dot_nt(c_g, b_ref[:, g * N:(g + 1) * N])
        cb_diag = jnp.sum(jnp.where(eye, cb, 0.0), axis=0, keepdims=True)
        pieces = []
        for e2 in range(E // 2):
            ws = []
            for h in (g * E + 2 * e2, g * E + 2 * e2 + 1):
                arg = jnp.where(lower,
                                col2[:, h:h + 1] - row2[h:h + 1, :],
                                row2[H + h:H + h + 1, :] - col2[:, H + h:H + h + 1])
                diag = cb_diag * dtT[H + h:H + h + 1, :] + dsk_ref[h]
                ws.append((cb * jnp.exp2(arg) + jnp.where(eye, diag, 0.0)).astype(BF16))
            h0 = g * E + 2 * e2
            xp = x_ref[:, h0 * P:(h0 + 2) * P]
            rhs = jnp.concatenate([xp * keep_left, xp * keep_right], axis=0)
            pieces.append(_dot(jnp.concatenate(ws, axis=1), rhs))
        y = jnp.concatenate(pieces, axis=1) if len(pieces) > 1 else pieces[0]
        sl = slice(g * Q, (g + 1) * Q)
        y = (y + _dot(c_g, pf_ref[0, 0, g]) * _expand(sdec, e_ref[0, :, sl])
             + _dot(c_g, pb_ref[0, 0, g]) * _expand(sdec, e_ref[1, :, sl]))
        y = y * zs_ref[:, sl].astype(F32)
        ms = jnp.mean(y * y, axis=-1, keepdims=True)
        out_ref[:, sl] = (y * lax.rsqrt(ms + EPS) * ng_ref[:, sl]).astype(out_ref.dtype)


def _ssd_out(zs, xbc, dtraw, dtrawT, prev_f, prev_b, bias, alog, biasT, alogT, d_skip, norm_g, emat,
             batch, seq, D, H):
    L, G, N = SSD_CHUNK, SSD_GROUPS, SSD_STATE
    GN = G * N
    Q = D // G
    P = D // H
    NC = seq // L
    M = batch * seq
    H2 = 2 * H
    assert (H // G) % 2 == 0 and 2 * P == 128 and D % GN == 0
    b_off = D // GN
    blk = (2 * _nbytes((L, D), BF16) + 2 * _nbytes((L, GN), BF16) + 2 * _nbytes((L, H2), F32)
           + 2 * _nbytes((G, N, Q), BF16) + _nbytes((2, 2 * H2, D), BF16) + _nbytes((L, D), BF16))
    rowblk = lambda b, c: b * NC + c
    return pl.pallas_call(
        functools.partial(_ssd_out_kernel, H=H, G=G, N=N, Q=Q, P=P),
        grid=(batch, NC),
        in_specs=[pl.BlockSpec(memory_space=pltpu.SMEM),
                  pl.BlockSpec((L, D), lambda b, c: (rowblk(b, c), 0)),
                  pl.BlockSpec((L, GN), lambda b, c: (rowblk(b, c), b_off)),
                  pl.BlockSpec((L, GN), lambda b, c: (rowblk(b, c), b_off + 1)),
                  pl.BlockSpec((L, D), lambda b, c: (rowblk(b, c), 0)),
                  pl.BlockSpec((L, H2), lambda b, c: (rowblk(b, c), 0)),
                  pl.BlockSpec((H2, L), lambda b, c: (0, rowblk(b, c))),
                  pl.BlockSpec((1, 1, G, N, Q), lambda b, c: (b, c, 0, 0, 0)),
                  pl.BlockSpec((1, 1, G, N, Q), lambda b, c: (b, c, 0, 0, 0)),
                  pl.BlockSpec((1, H2), lambda b, c: (0, 0)),
                  pl.BlockSpec((1, H2), lambda b, c: (0, 0)),
                  pl.BlockSpec((H2, L), lambda b, c: (0, 0)),
                  pl.BlockSpec((H2, L), lambda b, c: (0, 0)),
                  pl.BlockSpec((1, D), lambda b, c: (0, 0)),
                  pl.BlockSpec((2, 2 * H2, D), lambda b, c: (0, 0, 0))],
        out_specs=pl.BlockSpec((L, D), lambda b, c: (rowblk(b, c), 0)),
        out_shape=jax.ShapeDtypeStruct((M, D), BF16),
        compiler_params=_params(("parallel", "parallel"), blk, 8 * _nbytes((L, D), F32)),
        name="ssd_out",
    )(d_skip, xbc, xbc, xbc, zs, dtraw, dtrawT, prev_f, prev_b, bias, alog, biasT, alogT,
      norm_g.reshape(1, D), emat)


def _prepare_layer(norm_mix_g, w_in, conv_w, conv_b, dt_bias_f, dt_bias_b, a_log_f, a_log_b, d_skip,
                   ssd_norm_g, w_ssd_up, pool_w, pool_scale, w_pool_up, w_out, norm_mlp_g, w_ff1, w_ff2,
                   norm_ple_g, w_ple_gate, w_ple_proj):
    D = w_in.shape[0]
    H = d_skip.shape[0]
    C = conv_w.shape[1]
    s_xbc = D + C
    s_dt = s_xbc + 2 * H
    w_dt = w_in[:, s_xbc:s_dt].astype(BF16)
    head_of_lane = jnp.arange(D, dtype=jnp.int32) // (D // H)
    onehot = (jnp.arange(H, dtype=jnp.int32)[:, None] == head_of_lane[None, :]).astype(BF16)
    zeros = jnp.zeros_like(onehot)
    L = SSD_CHUNK
    bias = jnp.concatenate([dt_bias_f, dt_bias_b])
    alog = jnp.concatenate([a_log_f, a_log_b])
    return dict(
        D=D, H=H, C=C,
        norm_mix_g=norm_mix_g,
        w_in=w_in.astype(BF16), s_dt=s_dt,
        w_dt=w_dt, w_dtT=w_dt.T,
        conv_w=conv_w, conv_b=conv_b,
        bias=bias.reshape(1, 2 * H), alog=alog.reshape(1, 2 * H),
        biasT=jnp.broadcast_to(bias[:, None], (2 * H, L)), alogT=jnp.broadcast_to(alog[:, None], (2 * H, L)),
        emat=jnp.stack([jnp.concatenate([onehot, zeros, onehot, zeros]),
                        jnp.concatenate([zeros, onehot, zeros, onehot])]),
        d_skip=d_skip, ssd_norm_g=ssd_norm_g,
        w_ssd_up=w_ssd_up.astype(BF16),
        pool_w=pool_w.astype(BF16), pool_scale=pool_scale, w_pool_up=w_pool_up.astype(BF16),
        w_out=w_out.astype(BF16), norm_mlp_g=norm_mlp_g,
        w_ff1=w_ff1.astype(BF16), w_ff2=w_ff2.astype(BF16),
        norm_ple_g=norm_ple_g, w_ple_gate=w_ple_gate.astype(BF16), w_ple_proj=w_ple_proj.astype(BF16),
    )


def _layer(x_parts, p_parts, lp, batch, seq):
    D, H = lp["D"], lp["H"]
    PD = lp["w_pool_up"].shape[0]
    off_ga, off_gb = PD, PD + D

    h, dtraw, dtrawT = _norm_dt(x_parts, lp["norm_mix_g"], lp["w_dt"], lp["w_dtT"])
    zs = _matmul(h, lp["w_in"], BF16, "in_proj_z", epilogue=lambda acc: acc * _sigmoid(acc), tn=_TN_WIDE,
                 w_cols=(0, D))
    tail = _matmul(h, lp["w_in"], BF16, "in_proj_tail", tn=_TN_WIDE,
                   w_cols=(lp["s_dt"], PD + 2 * D))

    xbc_raw = _matmul(h, lp["w_in"], BF16, "in_proj_xbc", tn=_TN_WIDE, w_cols=(D, lp["C"]))
    xbc = _conv_silu(xbc_raw, 0, lp["conv_w"], lp["conv_b"], seq)
    prev_f, prev_b = _ssd_states(xbc, dtraw, lp["bias"], lp["alog"], lp["emat"], batch, seq, D, H)
    y = _ssd_out(zs, xbc, dtraw, dtrawT, prev_f, prev_b, lp["bias"], lp["alog"], lp["biasT"],
                 lp["alogT"], lp["d_skip"], lp["ssd_norm_g"], lp["emat"], batch, seq, D, H)
    gated_a = _matmul(y, lp["w_ssd_up"], BF16, "ssd_up", tn=_TN_WIDE,
                      epilogue=lambda acc, g_ref: _sigmoid(g_ref[...].astype(F32)) * acc,
                      extras=(tail,), col_offsets=(off_ga,),
                      extra_specs=lambda tm, tn: [_tile_spec(off_ga)(tm, tn)])

    pooled = _pool(tail, 0, lp["pool_w"], lp["pool_scale"], seq)
    mixed = _matmul(pooled, lp["w_pool_up"], BF16, "pool_up", tn=_TN_WIDE,
                    epilogue=lambda acc, g_ref, a_ref: (_sigmoid(g_ref[...].astype(F32)) * acc
                                                        + a_ref[...].astype(F32)),
                    extras=(tail, gated_a), col_offsets=(off_gb,),
                    extra_specs=lambda tm, tn: [_tile_spec(off_gb)(tm, tn), _tile_spec(0)(tm, tn)])
    x, xg, ssq = _matmul_residual_norm(mixed, lp["w_out"], x_parts, lp["norm_mlp_g"], "out_proj")

    ff = _matmul(xg, lp["w_ff1"], BF16, "ff1", ssq=ssq, tn=_TN_WIDE,
                 epilogue=lambda acc: jnp.square(jnp.maximum(acc, 0.0)))
    x, xg, ssq = _matmul_kacc_residual_norm(ff, lp["w_ff2"], x, lp["norm_ple_g"], "ff2")

    ple = p_parts[0].shape[1]
    tm = _pick(functools.reduce(math.gcd, [p.shape[0] for p in p_parts]), (1024, 512, 256, 128))
    tiles = _part_tiles(p_parts, tm)

    def ple_epilogue(acc, x_ref, *refs):
        p = _part_read(refs[:-1], tiles, pl.program_id(0))
        return x_ref[...] + _sigmoid(acc) * _dot(p.astype(BF16), refs[-1][...])

    return _matmul(xg, lp["w_ple_gate"], F32, "ple", ssq=ssq, tm=tm, epilogue=ple_epilogue,
                   extras=(x, *p_parts, lp["w_ple_proj"]),
                   extra_specs=lambda tm, tn: ([_tile_spec(0)(tm, tn)] + _part_specs(p_parts, tm, ple)
                                               + [pl.BlockSpec((ple, tn), lambda i, j: (0, j))]))


def _run_groups(xs, ps, layers, norm_final_g):
    seq, D = xs[0].shape[1:]
    assert seq % SSD_CHUNK == 0
    batch = sum(x.shape[0] for x in xs)
    x_parts = [x.reshape(-1, D) for x in xs]
    for i, lp in enumerate(layers):
        p_parts = [p[i].reshape(-1, p.shape[-1]) for p in ps]
        x_parts = [_layer(x_parts, p_parts, lp, batch, seq)]
    outs = _norm_final(x_parts[0] if len(x_parts) == 1 else jnp.concatenate(x_parts), norm_final_g,
                       [x.shape[0] * seq for x in xs])
    return [o.reshape(x.shape) for o, x in zip(outs, xs)]


def kernel(x_prompt, x_sample, p_prompt, p_sample, norm_mix_g, w_in, conv_w, conv_b, dt_bias_f, dt_bias_b, a_log_f, a_log_b, d_skip, ssd_norm_g, w_ssd_up, pool_w, pool_scale, w_pool_up, w_out, norm_mlp_g, w_ff1, w_ff2, norm_ple_g, w_ple_gate, w_ple_proj, norm_final_g):
    stacked = (norm_mix_g, w_in, conv_w, conv_b, dt_bias_f, dt_bias_b, a_log_f, a_log_b, d_skip,
               ssd_norm_g, w_ssd_up, pool_w, pool_scale, w_pool_up, w_out, norm_mlp_g, w_ff1, w_ff2,
               norm_ple_g, w_ple_gate, w_ple_proj)
    layers = [_prepare_layer(*[w[i] for w in stacked]) for i in range(w_in.shape[0])]
    if x_prompt.shape[1] == x_sample.shape[1]:
        y_prompt, y_sample = _run_groups([x_prompt, x_sample], [p_prompt, p_sample], layers, norm_final_g)
    else:
        (y_prompt,) = _run_groups([x_prompt], [p_prompt], layers, norm_final_g)
        (y_sample,) = _run_groups([x_sample], [p_sample], layers, norm_final_g)
    return (y_prompt, y_sample)
```

```python
import functools
import math

import jax
import jax.numpy as jnp
from jax import lax
from jax.experimental import pallas as pl
from jax.experimental.pallas import tpu as pltpu

F32 = jnp.float32
BF16 = jnp.bfloat16

EPS = 1e-6
LOG2E = 1.4426950408889634
LN2 = 0.6931471805599453
SSD_GROUPS = 8
SSD_STATE = 128
SSD_CHUNK = 128
POOL_WINDOWS = (2, 4, 8, 16)

V7X_VMEM_BYTES = 64 * 1024 * 1024
V7X_VMEM_RESERVE_BYTES = 8 * 1024 * 1024
BF16_SUBLANE_TILE = 16
LANES = 128
HALO = BF16_SUBLANE_TILE
_TN_WIDE = 1024
CONV_SUB_ROWS = 128
CAST_BLOCK_BYTES = 2 * 1024 * 1024


def _pick(dim, prefs):
    for p in prefs:
        if dim % p == 0:
            return p
    return dim


def _nbytes(shape, dtype):
    n = 1
    for s in shape:
        n *= s
    return n * jnp.dtype(dtype).itemsize


def _params(semantics, block_bytes, scratch_bytes=0):
    est = 2 * block_bytes + scratch_bytes
    limit = min(2 * est + (4 << 20), V7X_VMEM_BYTES - V7X_VMEM_RESERVE_BYTES)
    return pltpu.CompilerParams(dimension_semantics=semantics, vmem_limit_bytes=int(limit))


def _sigmoid(v):
    return 1.0 / (1.0 + jnp.exp(-v))


def _softplus(v):
    u = jnp.exp(-jnp.abs(v))
    w = 1.0 + u
    return jnp.maximum(v, 0.0) + jnp.where(w == 1.0, u, (jnp.log2(w) * LN2) * (u / (w - 1.0)))


def _split3(v):
    a = v.astype(BF16)
    r = v - a.astype(F32)
    b = r.astype(BF16)
    c = (r - b.astype(F32)).astype(BF16)
    return a, b, c


def _split2(v):
    a = v.astype(BF16)
    b = (v - a.astype(F32)).astype(BF16)
    return a, b


def _dot(a, b):
    return jnp.dot(a, b, preferred_element_type=F32)


def _dot_nt(a, b):
    return lax.dot_general(a, b, (((1,), (1,)), ((), ())), preferred_element_type=F32)


def _dot_tn(a, b):
    return lax.dot_general(a, b, (((0,), (0,)), ((), ())), preferred_element_type=F32)


def _rms(x, g):
    ms = jnp.mean(x * x, axis=-1, keepdims=True)
    return x * lax.rsqrt(ms + EPS) * g


def _part_tiles(parts, tm):
    return [p.shape[0] // tm for p in parts]


def _part_specs(parts, tm, ncols, col_index=None):
    specs, start = [], 0
    for n in _part_tiles(parts, tm):
        def index(*idx, s=start, n=n):
            row = idx[0] - s
            if col_index is None:
                return jnp.clip(row, 0, n - 1), 0
            return jnp.clip(row, 0, n - 1), jnp.where((row >= 0) & (row < n), col_index(*idx), 0)
        specs.append(pl.BlockSpec((tm, ncols), index))
        start += n
    return specs


def _part_read(refs, tiles, i):
    val = refs[-1][...]
    end = sum(tiles[:-1])
    for ref, n in zip(reversed(refs[:-1]), reversed(tiles[:-1])):
        val = jnp.where(i < end, ref[...], val)
        end -= n
    return val


def _norm_dt_kernel(*refs, tiles):
    n = len(tiles)
    x_refs, (g_ref, wdt_ref, wdtT_ref, h_ref, dt_ref, dtT_ref) = refs[:n], refs[n:]
    h = _rms(_part_read(x_refs, tiles, pl.program_id(0)), g_ref[...]).astype(BF16)
    h_ref[...] = h
    dt_ref[...] = _dot(h, wdt_ref[...])
    dtT_ref[...] = _dot_nt(wdtT_ref[...], h)


def _norm_dt(x_parts, g, w_dt, w_dtT):
    D = x_parts[0].shape[1]
    M = sum(p.shape[0] for p in x_parts)
    H2 = w_dt.shape[1]
    tm = _pick(functools.reduce(math.gcd, [p.shape[0] for p in x_parts]), (512, 256, 128))
    blk = (len(x_parts) * _nbytes((tm, D), F32) + _nbytes((tm, D), BF16) + 2 * _nbytes((D, H2), BF16)
           + 2 * _nbytes((tm, H2), F32))
    return pl.pallas_call(
        functools.partial(_norm_dt_kernel, tiles=_part_tiles(x_parts, tm)),
        grid=(M // tm,),
        in_specs=_part_specs(x_parts, tm, D) + [
            pl.BlockSpec((1, D), lambda i: (0, 0)),
            pl.BlockSpec((D, H2), lambda i: (0, 0)),
            pl.BlockSpec((H2, D), lambda i: (0, 0))],
        out_specs=[pl.BlockSpec((tm, D), lambda i: (i, 0)),
                   pl.BlockSpec((tm, H2), lambda i: (i, 0)),
                   pl.BlockSpec((H2, tm), lambda i: (0, i))],
        out_shape=[jax.ShapeDtypeStruct((M, D), BF16),
                   jax.ShapeDtypeStruct((M, H2), F32),
                   jax.ShapeDtypeStruct((H2, M), F32)],
        compiler_params=_params(("arbitrary",), blk),
        name="norm_dt",
    )(*x_parts, g.reshape(1, D), w_dt, w_dtT)


def _norm_final_kernel(x_ref, g_ref, *o_refs, tiles):
    i = pl.program_id(0)
    y = _rms(x_ref[...], g_ref[...])
    start = 0
    for o_ref, n in zip(o_refs, tiles):
        @pl.when((i >= start) & (i < start + n))
        def _(o_ref=o_ref):
            o_ref[...] = y
        start += n


def _norm_final(x, g, part_rows):
    M, D = x.shape
    tm = _pick(functools.reduce(math.gcd, part_rows), (512, 256, 128))
    outs = [jax.ShapeDtypeStruct((r, D), F32) for r in part_rows]
    blk = (1 + len(part_rows)) * _nbytes((tm, D), F32)
    return pl.pallas_call(
        functools.partial(_norm_final_kernel, tiles=_part_tiles(outs, tm)),
        grid=(M // tm,),
        in_specs=[pl.BlockSpec((tm, D), lambda i: (i, 0)),
                  pl.BlockSpec((1, D), lambda i: (0, 0))],
        out_specs=_part_specs(outs, tm, D),
        out_shape=outs,
        compiler_params=_params(("arbitrary",), blk),
        name="norm_final",
    )(x, g.reshape(1, D))


def _row_rsqrt(ssq_ref, d):
    return lax.rsqrt(jnp.sum(ssq_ref[...], axis=-1, keepdims=True) * (1.0 / d) + EPS)


def _lane_partial_sumsq(v):
    sq = v * v
    part = sq[:, 0:LANES]
    for q in range(1, v.shape[1] // LANES):
        part = part + sq[:, q * LANES:(q + 1) * LANES]
    return part


def _mm_kernel(lhs_ref, w_ref, *rest, epilogue, norm_width, n_casts):
    n_in = len(rest) - 1 - 2 * n_casts
    ins, cast_srcs, out_ref, cast_dsts = rest[:n_in], rest[n_in:n_in + n_casts], rest[n_in + n_casts], rest[n_in + n_casts + 1:]
    acc = _dot(lhs_ref[...], w_ref[...])
    if norm_width:
        acc = acc * _row_rsqrt(ins[0], norm_width)
        ins = ins[1:]
    out_ref[...] = epilogue(acc, *ins).astype(out_ref.dtype)
    for src_ref, dst_ref in zip(cast_srcs, cast_dsts):
        dst_ref[...] = src_ref[...].astype(dst_ref.dtype)


def _plan_casts(arrays, steps):
    jobs, rest, used = [], [], 0
    for a in arrays:
        cols = _pick(a.shape[1], (2048, 1024, 512, 256, 128))
        fit = [r for r in (128, 256, 512, 1024, 2048, 4096) if a.shape[0] % r == 0
               and _nbytes((r, cols), F32) <= CAST_BLOCK_BYTES
               and used + (a.shape[0] // r) * (a.shape[1] // cols) <= steps]
        if fit and not rest:
            jobs.append((a, (fit[0], cols)))
            used += (a.shape[0] // fit[0]) * (a.shape[1] // cols)
        else:
            rest.append(a)
    return tuple(jobs), rest


def _cast_specs(casts, n_i, n_j):
    specs, start = [], 0
    for src, block in casts:
        n_rb, n_cb = src.shape[0] // block[0], src.shape[1] // block[1]

        def index(i, j, start=start, n=n_rb * n_cb, n_cb=n_cb):
            b = jnp.clip(i * n_j + j - start, 0, n - 1)
            return b // n_cb, b % n_cb
        specs.append(pl.BlockSpec(block, index))
        start += n_rb * n_cb
    assert start <= n_i * n_j, "not enough grid steps to carry the weight casts"
    return specs


def _matmul(lhs, w, out_dtype, name, epilogue=None, extras=(), extra_specs=None, ssq=None, tm=None, tn=None,
            col_offsets=(), w_cols=None, casts=()):
    M, K = lhs.shape
    w_off, N = w_cols if w_cols else (0, w.shape[1])
    tm = tm or _pick(M, (1024, 512, 256, 128))
    tn = next(t for t in ((tn,) if tn else ()) + (512, 256, 128)
              if N % t == 0 and all(o % t == 0 for o in col_offsets))
    if epilogue is None:
        epilogue = lambda acc: acc
    specs = extra_specs(tm, tn) if extra_specs else []
    if ssq is not None:
        extras = (ssq,) + tuple(extras)
        specs = [pl.BlockSpec((tm, LANES), lambda i, j: (i, 0))] + specs
    cast_arrays = tuple(casts)
    casts, uncast = _plan_casts(cast_arrays, (M // tm) * (N // tn))
    cast_specs = _cast_specs(casts, M // tm, N // tn)
    blk = (_nbytes((tm, K), lhs.dtype) + _nbytes((K, tn), w.dtype) + _nbytes((tm, tn), out_dtype)
           + sum(_nbytes(s.block_shape, e.dtype) for s, e in zip(specs, extras))
           + sum(_nbytes(b, F32) + _nbytes(b, BF16) for _, b in casts))
    if w_off % tn == 0:
        w_blk0 = w_off // tn
        w_spec = pl.BlockSpec((K, tn), lambda i, j: (0, w_blk0 + j))
    elif w_off % LANES == 0:
        w_spec = pl.BlockSpec((pl.Element(K), pl.Element(tn)),
                              lambda i, j: (0, pl.multiple_of(w_off + j * tn, LANES)))
    else:
        w = w[:, w_off:w_off + N]
        w_spec = pl.BlockSpec((K, tn), lambda i, j: (0, j))
    outs = pl.pallas_call(
        functools.partial(_mm_kernel, epilogue=epilogue, norm_width=K if ssq is not None else 0,
                          n_casts=len(casts)),
        grid=(M // tm, N // tn),
        in_specs=[pl.BlockSpec((tm, K), lambda i, j: (i, 0)), w_spec] + specs + cast_specs,
        out_specs=[pl.BlockSpec((tm, tn), lambda i, j: (i, j))] + cast_specs,
        out_shape=[jax.ShapeDtypeStruct((M, N), out_dtype)]
        + [jax.ShapeDtypeStruct(src.shape, BF16) for src, _ in casts],
        compiler_params=_params(("arbitrary", "arbitrary") if casts else ("parallel", "arbitrary"),
                                blk, _nbytes((tm, tn), F32)),
        name=name,
    )(lhs, w, *extras, *[src for src, _ in casts])
    if not cast_arrays:
        return outs[0]
    return outs[0], tuple(outs[1:]) + tuple(a.astype(BF16) for a in uncast)


def _tile_spec(col_offset):
    def make(tm, tn):
        assert col_offset % tn == 0
        off = col_offset // tn
        return pl.BlockSpec((tm, tn), lambda i, j: (i, off + j))
    return make


def _emit_norm_operand(x_new, g_ref, xg_ref, ssq_ref, first):
    xg_ref[...] = (x_new * g_ref[...]).astype(xg_ref.dtype)
    part = _lane_partial_sumsq(x_new)

    @pl.when(first)
    def _():
        ssq_ref[...] = part

    @pl.when(jnp.logical_not(first))
    def _():
        ssq_ref[...] += part


def _mm_res_norm_kernel(lhs_ref, w_ref, *refs, tiles):
    n = len(tiles)
    res_refs, (g_ref, x_ref, xg_ref, ssq_ref) = refs[:n], refs[n:]
    x_new = _part_read(res_refs, tiles, pl.program_id(0)) + _dot(lhs_ref[...], w_ref[...])
    x_ref[...] = x_new
    _emit_norm_operand(x_new, g_ref, xg_ref, ssq_ref, pl.program_id(1) == 0)


def _matmul_residual_norm(lhs, w, res_parts, g_next, name):
    M, K = lhs.shape
    N = w.shape[1]
    tm = _pick(functools.reduce(math.gcd, [p.shape[0] for p in res_parts]), (1024, 512, 256, 128))
    tn = _pick(N, (512, 256, 128))
    blk = (_nbytes((tm, K), lhs.dtype) + _nbytes((K, tn), w.dtype) + (1 + len(res_parts)) * _nbytes((tm, tn), F32)
           + _nbytes((tm, tn), BF16) + _nbytes((tm, LANES), F32))
    return pl.pallas_call(
        functools.partial(_mm_res_norm_kernel, tiles=_part_tiles(res_parts, tm)),
        grid=(M // tm, N // tn),
        in_specs=[pl.BlockSpec((tm, K), lambda i, j: (i, 0)),
                  pl.BlockSpec((K, tn), lambda i, j: (0, j))]
        + _part_specs(res_parts, tm, tn, col_index=lambda i, j: j)
        + [pl.BlockSpec((1, tn), lambda i, j: (0, j))],
        out_specs=[pl.BlockSpec((tm, tn), lambda i, j: (i, j)),
                   pl.BlockSpec((tm, tn), lambda i, j: (i, j)),
                   pl.BlockSpec((tm, LANES), lambda i, j: (i, 0))],
        out_shape=[jax.ShapeDtypeStruct((M, N), F32),
                   jax.ShapeDtypeStruct((M, N), BF16),
                   jax.ShapeDtypeStruct((M, LANES), F32)],
        compiler_params=_params(("arbitrary", "arbitrary"), blk, _nbytes((tm, tn), F32)),
        name=name,
    )(lhs, w, *res_parts, g_next.reshape(1, N))


def _mm_kacc_kernel(lhs_ref, w_ref, res_ref, g_ref, x_ref, xg_ref, ssq_ref, *, nk):
    k = pl.program_id(2)

    def step(base_ref, finish):
        x_new = base_ref[...] + _dot(lhs_ref[...], w_ref[...])
        x_ref[...] = x_new
        if finish:
            _emit_norm_operand(x_new, g_ref, xg_ref, ssq_ref, pl.program_id(1) == 0)

    if nk == 1:
        step(res_ref, True)
        return
    pl.when(k == 0)(lambda: step(res_ref, False))
    if nk > 2:
        pl.when((k > 0) & (k < nk - 1))(lambda: step(x_ref, False))
    pl.when(k == nk - 1)(lambda: step(x_ref, True))


def _matmul_kacc_residual_norm(lhs, w, res, g_next, name):
    M, K = lhs.shape
    N = w.shape[1]
    tm = _pick(M, (1024, 512, 256, 128))
    tn = _pick(N, (1024, 512, 256, 128))
    tk = _pick(K, (2048, 1024, 512, 256, 128))
    blk = (_nbytes((tm, tk), lhs.dtype) + _nbytes((tk, tn), w.dtype) + 2 * _nbytes((tm, tn), F32)
           + _nbytes((tm, tn), BF16) + _nbytes((tm, LANES), F32))
    return pl.pallas_call(
        functools.partial(_mm_kacc_kernel, nk=K // tk),
        grid=(M // tm, N // tn, K // tk),
        in_specs=[pl.BlockSpec((tm, tk), lambda i, j, k: (i, k)),
                  pl.BlockSpec((tk, tn), lambda i, j, k: (k, j)),
                  pl.BlockSpec((tm, tn), lambda i, j, k: (i, j)),
                  pl.BlockSpec((1, tn), lambda i, j, k: (0, j))],
        out_specs=[pl.BlockSpec((tm, tn), lambda i, j, k: (i, j)),
                   pl.BlockSpec((tm, tn), lambda i, j, k: (i, j)),
                   pl.BlockSpec((tm, LANES), lambda i, j, k: (i, 0))],
        out_shape=[jax.ShapeDtypeStruct((M, N), F32),
                   jax.ShapeDtypeStruct((M, N), BF16),
                   jax.ShapeDtypeStruct((M, LANES), F32)],
        compiler_params=_params(("parallel", "arbitrary", "arbitrary"), blk, _nbytes((tm, tn), F32)),
        name=name,
    )(lhs, w, res, g_next.reshape(1, N))


def _conv_kernel(prev_ref, cur_ref, next_ref, w_ref, b_ref, shift_ref, out_ref, buf_ref, *, tm, seq_tiles, width):
    i = pl.program_id(0)
    first = (i % seq_tiles) == 0
    last = (i % seq_tiles) == seq_tiles - 1
    prev = prev_ref[...]
    nxt = next_ref[...]
    buf_ref[0:HALO, :] = jnp.where(first, jnp.zeros_like(prev), prev)
    buf_ref[HALO:HALO + tm, :] = cur_ref[...]
    buf_ref[HALO + tm:, :] = jnp.where(last, jnp.zeros_like(nxt), nxt)

    half = width // 2
    taps = [k for k in range(width) if k != half]
    sub = shift_ref.shape[0] // len(taps)
    win = sub + 2 * HALO
    shift = shift_ref[...]
    for r0 in range(0, tm, sub):
        window = buf_ref[r0:r0 + win, :]
        shifted = _dot(shift, window)
        acc = b_ref[...] + w_ref[half:half + 1, :] * window[HALO:HALO + sub].astype(F32)
        for n, k in enumerate(taps):
            acc = acc + w_ref[k:k + 1, :] * shifted[n * sub:(n + 1) * sub]
        out_ref[r0:r0 + sub, :] = (acc * _sigmoid(acc)).astype(out_ref.dtype)


def _conv_silu(proj, col_offset, conv_w, conv_b, seq):
    M = proj.shape[0]
    width, C = conv_w.shape
    assert width // 2 <= HALO
    tm = _pick(seq, (512, 256, 128))
    tc = _pick(C, (2048, 1024, 512, 256, 128))
    while col_offset % tc:
        tc //= 2
    off = col_offset // tc
    hb = tm // HALO
    n_halo = M // HALO
    half = width // 2
    sub = min(tm, CONV_SUB_ROWS)
    win = sub + 2 * HALO
    rows = jnp.arange(sub, dtype=jnp.int32)[:, None]
    cols = jnp.arange(win, dtype=jnp.int32)[None, :]
    shift = jnp.concatenate([(cols == rows + HALO + (k - half)) for k in range(width) if k != half],
                            axis=0).astype(BF16)
    blk = (2 * _nbytes((tm, tc), BF16) + 2 * _nbytes((HALO, tc), BF16) + _nbytes((width + 1, tc), F32)
           + _nbytes(shift.shape, BF16))
    return pl.pallas_call(
        functools.partial(_conv_kernel, tm=tm, seq_tiles=seq // tm, width=width),
        grid=(M // tm, C // tc),
        in_specs=[pl.BlockSpec((HALO, tc), lambda i, j: (jnp.maximum(i * hb - 1, 0), off + j)),
                  pl.BlockSpec((tm, tc), lambda i, j: (i, off + j)),
                  pl.BlockSpec((HALO, tc), lambda i, j: (jnp.minimum((i + 1) * hb, n_halo - 1), off + j)),
                  pl.BlockSpec((width, tc), lambda i, j: (0, j)),
                  pl.BlockSpec((1, tc), lambda i, j: (0, j)),
                  pl.BlockSpec(shift.shape, lambda i, j: (0, 0))],
        out_specs=pl.BlockSpec((tm, tc), lambda i, j: (i, j)),
        out_shape=jax.ShapeDtypeStruct((M, C), BF16),
        scratch_shapes=[pltpu.VMEM((tm + 2 * HALO, tc), BF16)],
        compiler_params=_params(("parallel", "parallel"), blk,
                                _nbytes((tm + 2 * HALO, tc), BF16) + 8 * _nbytes((CONV_SUB_ROWS, tc), F32)),
        name="conv_silu",
    )(proj, proj, proj, conv_w, conv_b.reshape(1, C), shift)


def _pool_kernel(prev_ref, cur_ref, next_ref, pw_ref, ps_ref, out_ref, *, tm, seq, Wd):
    i = pl.program_id(0)
    t0 = (i * tm) % seq
    has_prev = t0 > 0
    has_next = t0 + tm < seq
    r = lax.broadcasted_iota(jnp.int32, (tm, tm), 0)
    c = lax.broadcasted_iota(jnp.int32, (tm, tm), 1)
    rh = lax.broadcasted_iota(jnp.int32, (tm, HALO), 0)
    ch = lax.broadcasted_iota(jnp.int32, (tm, HALO), 1)
    t = lax.broadcasted_iota(jnp.int32, (tm, Wd), 0) + t0
    for g, w in enumerate(POOL_WINDOWS):
        lo_off = w // 2
        hi_off = w - lo_off
        sl = slice(g * Wd, (g + 1) * Wd)
        band = jnp.where((c >= r - lo_off) & (c < r + hi_off), 1.0, 0.0).astype(BF16)
        band_prev = jnp.where(has_prev & (ch - HALO >= rh - lo_off), 1.0, 0.0).astype(BF16)
        band_next = jnp.where(has_next & (ch + tm < rh + hi_off), 1.0, 0.0).astype(BF16)
        cur = cur_ref[:, sl]
        wsum = _dot(band, cur) + _dot(band_prev, prev_ref[:, sl]) + _dot(band_next, next_ref[:, sl])
        cnt = jnp.minimum(t + hi_off, seq) - jnp.maximum(t - lo_off, 0)
        pooled = (wsum / cnt.astype(F32) - cur.astype(F32)).astype(BF16)
        out_ref[:, sl] = (_dot(pooled, pw_ref[g]) * ps_ref[:, sl]).astype(out_ref.dtype)


def _pool(proj, col_offset, pool_w, pool_scale, seq):
    M = proj.shape[0]
    PG, Wd, _ = pool_w.shape
    PD = PG * Wd
    assert PG == len(POOL_WINDOWS) and max(POOL_WINDOWS) // 2 <= HALO
    assert col_offset % PD == 0
    off = col_offset // PD
    tm = _pick(seq, (256, 128))
    hb = tm // HALO
    n_halo = M // HALO
    blk = (2 * _nbytes((tm, PD), BF16) + 2 * _nbytes((HALO, PD), BF16) + _nbytes((PG, Wd, Wd), BF16))
    return pl.pallas_call(
        functools.partial(_pool_kernel, tm=tm, seq=seq, Wd=Wd),
        grid=(M // tm,),
        in_specs=[pl.BlockSpec((HALO, PD), lambda i: (jnp.maximum(i * hb - 1, 0), off)),
                  pl.BlockSpec((tm, PD), lambda i: (i, off)),
                  pl.BlockSpec((HALO, PD), lambda i: (jnp.minimum((i + 1) * hb, n_halo - 1), off)),
                  pl.BlockSpec((PG, Wd, Wd), lambda i: (0, 0, 0)),
                  pl.BlockSpec((1, PD), lambda i: (0, 0))],
        out_specs=pl.BlockSpec((tm, PD), lambda i: (i, 0)),
        out_shape=jax.ShapeDtypeStruct((M, PD), BF16),
        compiler_params=_params(("parallel",), blk, 4 * _nbytes((tm, PD), F32)),
        name="pool",
    )(proj, proj, proj, pool_w, pool_scale.reshape(1, PD))


def _cumsum_rows(tri, v):
    a, b, c = _split3(v)
    return _dot(tri, a) + _dot(tri, b) + _dot(tri, c)


def _cumsum_lanes(v, triT):
    a, b, c = _split3(v)
    return _dot(a, triT) + _dot(b, triT) + _dot(c, triT)


def _hilo(v):
    a, b = _split2(v)
    return jnp.concatenate([a, b], axis=1)


def _expand(v_hilo, e2):
    return _dot(v_hilo, e2)


def _ssd_state_kernel(xf_ref, bf_ref, dtf_ref, xb_ref, bb_ref, dtb_ref, bias_ref, alog_ref, e_ref,
                      pf_ref, pb_ref, h_ref, *, H, G, N, Q):
    L = SSD_CHUNK

    @pl.when(pl.program_id(1) == 0)
    def _():
        h_ref[...] = jnp.zeros_like(h_ref)

    row = lax.broadcasted_iota(jnp.int32, (L, L), 0)
    col = lax.broadcasted_iota(jnp.int32, (L, L), 1)
    tri = jnp.where(row >= col, 1.0, 0.0).astype(BF16)
    is_fwd = lax.broadcasted_iota(jnp.int32, (L, 2 * H), 1) < H
    A = -jnp.exp(alog_ref[...])

    def direction(d, x_ref, b_ref, dtraw_ref, p_ref):
        dt = _softplus(dtraw_ref[...] + bias_ref[...])
        a = dt * A
        cum = _cumsum_rows(tri, a)
        total = cum[L - 1:L, :]
        dte = jnp.exp(jnp.where(is_fwd, total - cum, cum - a))
        wexp = _expand(_hilo(dt * dte), e_ref[d])
        xs = (x_ref[...].astype(F32) * wexp).astype(BF16)
        cdexp = _expand(_hilo(jnp.broadcast_to(jnp.exp(total), (8, 2 * H))), e_ref[d])[0:1, :]
        bm = b_ref[...]
        for g in range(G):
            st = _dot_tn(bm[:, g * N:(g + 1) * N], xs[:, g * Q:(g + 1) * Q])
            h_old = h_ref[d, g]
            p_ref[0, 0, g] = h_old.astype(BF16)
            h_ref[d, g] = cdexp[:, g * Q:(g + 1) * Q] * h_old + st

    direction(0, xf_ref, bf_ref, dtf_ref, pf_ref)
    direction(1, xb_ref, bb_ref, dtb_ref, pb_ref)


def _ssd_states(xbc, dtraw, bias, alog, emat, batch, seq, D, H):
    L, G, N = SSD_CHUNK, SSD_GROUPS, SSD_STATE
    GN = G * N
    Q = D // G
    NC = seq // L
    assert D % GN == 0 or GN % D == 0
    b_off = D // GN
    H2 = 2 * H
    blk = 2 * (_nbytes((L, D), BF16) + _nbytes((L, GN), BF16) + _nbytes((L, H2), F32)
               + _nbytes((G, N, Q), BF16)) + _nbytes((2, 2 * H2, D), BF16)
    fwd = lambda b, t: b * NC + t
    bwd = lambda b, t: b * NC + (NC - 1 - t)
    out_sds = jax.ShapeDtypeStruct((batch, NC, G, N, Q), BF16)
    return pl.pallas_call(
        functools.partial(_ssd_state_kernel, H=H, G=G, N=N, Q=Q),
        grid=(batch, NC),
        in_specs=[pl.BlockSpec((L, D), lambda b, t: (fwd(b, t), 0)),
                  pl.BlockSpec((L, GN), lambda b, t: (fwd(b, t), b_off)),
                  pl.BlockSpec((L, H2), lambda b, t: (fwd(b, t), 0)),
                  pl.BlockSpec((L, D), lambda b, t: (bwd(b, t), 0)),
                  pl.BlockSpec((L, GN), lambda b, t: (bwd(b, t), b_off)),
                  pl.BlockSpec((L, H2), lambda b, t: (bwd(b, t), 0)),
                  pl.BlockSpec((1, H2), lambda b, t: (0, 0)),
                  pl.BlockSpec((1, H2), lambda b, t: (0, 0)),
                  pl.BlockSpec((2, 2 * H2, D), lambda b, t: (0, 0, 0))],
        out_specs=[pl.BlockSpec((1, 1, G, N, Q), lambda b, t: (b, t, 0, 0, 0)),
                   pl.BlockSpec((1, 1, G, N, Q), lambda b, t: (b, NC - 1 - t, 0, 0, 0))],
        out_shape=[out_sds, out_sds],
        scratch_shapes=[pltpu.VMEM((2, G, N, Q), F32)],
        compiler_params=_params(("parallel", "arbitrary"), blk,
                                _nbytes((2, G, N, Q), F32) + 4 * _nbytes((L, D), F32)),
        name="ssd_states",
    )(xbc, xbc, dtraw, xbc, xbc, dtraw, bias, alog, emat)


def _ssd_out_kernel(dsk_ref, x_ref, b_ref, c_ref, zs_ref, dt_ref, dtT_ref, pf_ref, pb_ref,
                    bias_ref, alog_ref, biasT_ref, alogT_ref, ng_ref, e_ref, out_ref,
                    *, H, G, N, Q, P):
    L = SSD_CHUNK
    E = H // G
    row = lax.broadcasted_iota(jnp.int32, (L, L), 0)
    col = lax.broadcasted_iota(jnp.int32, (L, L), 1)
    lower = row >= col
    eye = row == col
    tri = jnp.where(lower, 1.0, 0.0).astype(BF16)
    triT = jnp.where(row <= col, 1.0, 0.0).astype(BF16)
    is_fwd = lax.broadcasted_iota(jnp.int32, (L, 2 * H), 1) < H
    is_fwdT = lax.broadcasted_iota(jnp.int32, (2 * H, L), 0) < H

    dt = _softplus(dt_ref[...] + bias_ref[...])
    a = dt * (-jnp.exp(alog_ref[...]))
    cum = _cumsum_rows(tri, a)
    total = cum[L - 1:L, :]
    excl = cum - a
    col2 = jnp.where(is_fwd, cum, excl) * LOG2E
    sdec = _hilo(jnp.exp(jnp.where(is_fwd, cum, total - excl)))

    dtT = _softplus(dtT_ref[...] + biasT_ref[...])
    aT = dtT * (-jnp.exp(alogT_ref[...]))
    cumT = _cumsum_lanes(aT, triT)
    lg = jnp.log2(dtT)
    row2 = jnp.where(is_fwdT, cumT * LOG2E - lg, (cumT - aT) * LOG2E + lg)

    lane = lax.broadcasted_iota(jnp.int32, (L, 2 * P), 1)
    keep_left = jnp.where(lane < P, 1.0, 0.0).astype(BF16)
    keep_right = jnp.where(lane < P, 0.0, 1.0).astype(BF16)

    for g in range(G):
        c_g = c_ref[:, g * N:(g + 1) * N]
        cb = _dot_nt(c_g, b_ref[:, g * N:(g + 1) * N])
        cb_diag = jnp.sum(jnp.where(eye, cb, 0.0), axis=0, keepdims=True)
        pieces = []
        for e2 in range(E // 2):
            ws = []
            for h in (g * E + 2 * e2, g * E + 2 * e2 + 1):
                arg = jnp.where(lower,
                                col2[:, h:h + 1] - row2[h:h + 1, :],
                                row2[H + h:H + h + 1, :] - col2[:, H + h:H + h + 1])
                diag = cb_diag * dtT[H + h:H + h + 1, :] + dsk_ref[h]
                ws.append((cb * jnp.exp2(arg) + jnp.where(eye, diag, 0.0)).astype(BF16))
            h0 = g * E + 2 * e2
            xp = x_ref[:, h0 * P:(h0 + 2) * P]
            rhs = jnp.concatenate([xp * keep_left, xp * keep_right], axis=0)
            pieces.append(_dot(jnp.concatenate(ws, axis=1), rhs))
        y = jnp.concatenate(pieces, axis=1) if len(pieces) > 1 else pieces[0]
        sl = slice(g * Q, (g + 1) * Q)
        y = (y + _dot(c_g, pf_ref[0, 0, g]) * _expand(sdec, e_ref[0, :, sl])
             + _dot(c_g, pb_ref[0, 0, g]) * _expand(sdec, e_ref[1, :, sl]))
        y = y * zs_ref[:, sl].astype(F32)
        ms = jnp.mean(y * y, axis=-1, keepdims=True)
        out_ref[:, sl] = (y * lax.rsqrt(ms + EPS) * ng_ref[:, sl]).astype(out_ref.dtype)


def _ssd_out(zs, xbc, dtraw, dtrawT, prev_f, prev_b, bias, alog, biasT, alogT, d_skip, norm_g, emat,
             batch, seq, D, H):
    L, G, N = SSD_CHUNK, SSD_GROUPS, SSD_STATE
    GN = G * N
    Q = D // G
    P = D // H
    NC = seq // L
    M = batch * seq
    H2 = 2 * H
    assert (H // G) % 2 == 0 and 2 * P == 128 and D % GN == 0
    b_off = D // GN
    blk = (2 * _nbytes((L, D), BF16) + 2 * _nbytes((L, GN), BF16) + 2 * _nbytes((L, H2), F32)
           + 2 * _nbytes((G, N, Q), BF16) + _nbytes((2, 2 * H2, D), BF16) + _nbytes((L, D), BF16))
    rowblk = lambda b, c: b * NC + c
    return pl.pallas_call(
        functools.partial(_ssd_out_kernel, H=H, G=G, N=N, Q=Q, P=P),
        grid=(batch, NC),
        in_specs=[pl.BlockSpec(memory_space=pltpu.SMEM),
                  pl.BlockSpec((L, D), lambda b, c: (rowblk(b, c), 0)),
                  pl.BlockSpec((L, GN), lambda b, c: (rowblk(b, c), b_off)),
                  pl.BlockSpec((L, GN), lambda b, c: (rowblk(b, c), b_off + 1)),
                  pl.BlockSpec((L, D), lambda b, c: (rowblk(b, c), 0)),
                  pl.BlockSpec((L, H2), lambda b, c: (rowblk(b, c), 0)),
                  pl.BlockSpec((H2, L), lambda b, c: (0, rowblk(b, c))),
                  pl.BlockSpec((1, 1, G, N, Q), lambda b, c: (b, c, 0, 0, 0)),
                  pl.BlockSpec((1, 1, G, N, Q), lambda b, c: (b, c, 0, 0, 0)),
                  pl.BlockSpec((1, H2), lambda b, c: (0, 0)),
                  pl.BlockSpec((1, H2), lambda b, c: (0, 0)),
                  pl.BlockSpec((H2, L), lambda b, c: (0, 0)),
                  pl.BlockSpec((H2, L), lambda b, c: (0, 0)),
                  pl.BlockSpec((1, D), lambda b, c: (0, 0)),
                  pl.BlockSpec((2, 2 * H2, D), lambda b, c: (0, 0, 0))],
        out_specs=pl.BlockSpec((L, D), lambda b, c: (rowblk(b, c), 0)),
        out_shape=jax.ShapeDtypeStruct((M, D), BF16),
        compiler_params=_params(("parallel", "parallel"), blk, 8 * _nbytes((L, D), F32)),
        name="ssd_out",
    )(d_skip, xbc, xbc, xbc, zs, dtraw, dtrawT, prev_f, prev_b, bias, alog, biasT, alogT,
      norm_g.reshape(1, D), emat)


def _prepare_layer(norm_mix_g, w_in, conv_w, conv_b, dt_bias_f, dt_bias_b, a_log_f, a_log_b, d_skip,
                   ssd_norm_g, w_ssd_up, pool_w, pool_scale, w_pool_up, w_out, norm_mlp_g, w_ff1, w_ff2,
                   norm_ple_g, w_ple_gate, w_ple_proj):
    D = w_in.shape[0]
    H = d_skip.shape[0]
    C = conv_w.shape[1]
    s_xbc = D + C
    s_dt = s_xbc + 2 * H
    w_dt = w_in[:, s_xbc:s_dt].astype(BF16)
    head_of_lane = jnp.arange(D, dtype=jnp.int32) // (D // H)
    onehot = (jnp.arange(H, dtype=jnp.int32)[:, None] == head_of_lane[None, :]).astype(BF16)
    zeros = jnp.zeros_like(onehot)
    L = SSD_CHUNK
    bias = jnp.concatenate([dt_bias_f, dt_bias_b])
    alog = jnp.concatenate([a_log_f, a_log_b])
    return dict(
        D=D, H=H, C=C,
        norm_mix_g=norm_mix_g,
        w_in=w_in.astype(BF16), s_dt=s_dt,
        w_dt=w_dt, w_dtT=w_dt.T,
        conv_w=conv_w, conv_b=conv_b,
        bias=bias.reshape(1, 2 * H), alog=alog.reshape(1, 2 * H),
        biasT=jnp.broadcast_to(bias[:, None], (2 * H, L)), alogT=jnp.broadcast_to(alog[:, None], (2 * H, L)),
        emat=jnp.stack([jnp.concatenate([onehot, zeros, onehot, zeros]),
                        jnp.concatenate([zeros, onehot, zeros, onehot])]),
        d_skip=d_skip, ssd_norm_g=ssd_norm_g,
        w_ssd_up=w_ssd_up, w_pool_up=w_pool_up, w_out=w_out, w_ff1=w_ff1, w_ff2=w_ff2, w_ple_gate=w_ple_gate,
        pool_w=pool_w.astype(BF16), pool_scale=pool_scale,
        norm_mlp_g=norm_mlp_g, norm_ple_g=norm_ple_g, w_ple_proj=w_ple_proj.astype(BF16),
    )


def _layer(x_parts, p_parts, lp, batch, seq):
    D, H = lp["D"], lp["H"]
    PD = lp["w_pool_up"].shape[0]
    off_ga, off_gb = PD, PD + D

    h, dtraw, dtrawT = _norm_dt(x_parts, lp["norm_mix_g"], lp["w_dt"], lp["w_dtT"])
    zs, (w_ssd_up, w_pool_up) = _matmul(
        h, lp["w_in"], BF16, "in_proj_z", epilogue=lambda acc: acc * _sigmoid(acc), tn=_TN_WIDE,
        w_cols=(0, D), casts=(lp["w_ssd_up"], lp["w_pool_up"]))
    tail, (w_ff1, w_out) = _matmul(
        h, lp["w_in"], BF16, "in_proj_tail", tn=_TN_WIDE, w_cols=(lp["s_dt"], PD + 2 * D),
        casts=(lp["w_ff1"], lp["w_out"]))

    xbc_raw, (w_ple_gate,) = _matmul(h, lp["w_in"], BF16, "in_proj_xbc", tn=_TN_WIDE, w_cols=(D, lp["C"]),
                                     casts=(lp["w_ple_gate"],))
    xbc = _conv_silu(xbc_raw, 0, lp["conv_w"], lp["conv_b"], seq)
    prev_f, prev_b = _ssd_states(xbc, dtraw, lp["bias"], lp["alog"], lp["emat"], batch, seq, D, H)
    y = _ssd_out(zs, xbc, dtraw, dtrawT, prev_f, prev_b, lp["bias"], lp["alog"], lp["biasT"],
                 lp["alogT"], lp["d_skip"], lp["ssd_norm_g"], lp["emat"], batch, seq, D, H)
    gated_a = _matmul(y, w_ssd_up, BF16, "ssd_up", tn=_TN_WIDE,
                      epilogue=lambda acc, g_ref: _sigmoid(g_ref[...].astype(F32)) * acc,
                      extras=(tail,), col_offsets=(off_ga,),
                      extra_specs=lambda tm, tn: [_tile_spec(off_ga)(tm, tn)])

    pooled = _pool(tail, 0, lp["pool_w"], lp["pool_scale"], seq)
    mixed = _matmul(pooled, w_pool_up, BF16, "pool_up", tn=_TN_WIDE,
                    epilogue=lambda acc, g_ref, a_ref: (_sigmoid(g_ref[...].astype(F32)) * acc
                                                        + a_ref[...].astype(F32)),
                    extras=(tail, gated_a), col_offsets=(off_gb,),
                    extra_specs=lambda tm, tn: [_tile_spec(off_gb)(tm, tn), _tile_spec(0)(tm, tn)])
    x, xg, ssq = _matmul_residual_norm(mixed, w_out, x_parts, lp["norm_mlp_g"], "out_proj")

    ff, (w_ff2,) = _matmul(xg, w_ff1, BF16, "ff1", ssq=ssq, tn=_TN_WIDE, casts=(lp["w_ff2"],),
                           epilogue=lambda acc: jnp.square(jnp.maximum(acc, 0.0)))
    x, xg, ssq = _matmul_kacc_residual_norm(ff, w_ff2, x, lp["norm_ple_g"], "ff2")

    ple = p_parts[0].shape[1]
    tm = _pick(functools.reduce(math.gcd, [p.shape[0] for p in p_parts]), (1024, 512, 256, 128))
    tiles = _part_tiles(p_parts, tm)

    def ple_epilogue(acc, x_ref, *refs):
        p = _part_read(refs[:-1], tiles, pl.program_id(0))
        return x_ref[...] + _sigmoid(acc) * _dot(p.astype(BF16), refs[-1][...])

    return _matmul(xg, w_ple_gate, F32, "ple", ssq=ssq, tm=tm, epilogue=ple_epilogue,
                   extras=(x, *p_parts, lp["w_ple_proj"]),
                   extra_specs=lambda tm, tn: ([_tile_spec(0)(tm, tn)] + _part_specs(p_parts, tm, ple)
                                               + [pl.BlockSpec((ple, tn), lambda i, j: (0, j))]))


def _run_groups(xs, ps, layers, norm_final_g):
    seq, D = xs[0].shape[1:]
    assert seq % SSD_CHUNK == 0
    batch = sum(x.shape[0] for x in xs)
    x_parts = [x.reshape(-1, D) for x in xs]
    for i, lp in enumerate(layers):
        p_parts = [p[i].reshape(-1, p.shape[-1]) for p in ps]
        x_parts = [_layer(x_parts, p_parts, lp, batch, seq)]
    outs = _norm_final(x_parts[0] if len(x_parts) == 1 else jnp.concatenate(x_parts), norm_final_g,
                       [x.shape[0] * seq for x in xs])
    return [o.reshape(x.shape) for o, x in zip(outs, xs)]


def kernel(x_prompt, x_sample, p_prompt, p_sample, norm_mix_g, w_in, conv_w, conv_b, dt_bias_f, dt_bias_b, a_log_f, a_log_b, d_skip, ssd_norm_g, w_ssd_up, pool_w, pool_scale, w_pool_up, w_out, norm_mlp_g, w_ff1, w_ff2, norm_ple_g, w_ple_gate, w_ple_proj, norm_final_g):
    stacked = (norm_mix_g, w_in, conv_w, conv_b, dt_bias_f, dt_bias_b, a_log_f, a_log_b, d_skip,
               ssd_norm_g, w_ssd_up, pool_w, pool_scale, w_pool_up, w_out, norm_mlp_g, w_ff1, w_ff2,
               norm_ple_g, w_ple_gate, w_ple_proj)
    layers = [_prepare_layer(*[w[i] for w in stacked]) for i in range(w_in.shape[0])]
    if x_prompt.shape[1] == x_sample.shape[1]:
        y_prompt, y_sample = _run_groups([x_prompt, x_sample], [p_prompt, p_sample], layers, norm_final_g)
    else:
        (y_prompt,) = _run_groups([x_prompt], [p_prompt], layers, norm_final_g)
        (y_sample,) = _run_groups([x_sample], [p_sample], layers, norm_final_g)
    return (y_prompt, y_sample)
```

```python
import functools
import math

import jax
import jax.numpy as jnp
from jax import lax
from jax.experimental import pallas as pl
from jax.experimental.pallas import tpu as pltpu

F32 = jnp.float32
BF16 = jnp.bfloat16

EPS = 1e-6
LOG2E = 1.4426950408889634
LN2 = 0.6931471805599453
SSD_GROUPS = 8
SSD_STATE = 128
SSD_CHUNK = 128
POOL_WINDOWS = (2, 4, 8, 16)

V7X_VMEM_BYTES = 64 * 1024 * 1024
V7X_VMEM_RESERVE_BYTES = 8 * 1024 * 1024
BF16_SUBLANE_TILE = 16
LANES = 128
HALO = BF16_SUBLANE_TILE
_TN_WIDE = 1024
CONV_SUB_ROWS = 128
CAST_BLOCK_BYTES = 2 * 1024 * 1024


def _pick(dim, prefs):
    for p in prefs:
        if dim % p == 0:
            return p
    return dim


def _nbytes(shape, dtype):
    n = 1
    for s in shape:
        n *= s
    return n * jnp.dtype(dtype).itemsize


def _params(semantics, block_bytes, scratch_bytes=0):
    est = 2 * block_bytes + scratch_bytes
    limit = min(2 * est + (4 << 20), V7X_VMEM_BYTES - V7X_VMEM_RESERVE_BYTES)
    return pltpu.CompilerParams(dimension_semantics=semantics, vmem_limit_bytes=int(limit))


def _sigmoid(v):
    return 1.0 / (1.0 + jnp.exp(-v))


def _softplus(v):
    u = jnp.exp(-jnp.abs(v))
    w = 1.0 + u
    return jnp.maximum(v, 0.0) + jnp.where(w == 1.0, u, (jnp.log2(w) * LN2) * (u / (w - 1.0)))


def _split3(v):
    a = v.astype(BF16)
    r = v - a.astype(F32)
    b = r.astype(BF16)
    c = (r - b.astype(F32)).astype(BF16)
    return a, b, c


def _split2(v):
    a = v.astype(BF16)
    b = (v - a.astype(F32)).astype(BF16)
    return a, b


def _dot(a, b):
    return jnp.dot(a, b, preferred_element_type=F32)


def _dot_nt(a, b):
    return lax.dot_general(a, b, (((1,), (1,)), ((), ())), preferred_element_type=F32)


def _dot_tn(a, b):
    return lax.dot_general(a, b, (((0,), (0,)), ((), ())), preferred_element_type=F32)


def _rms(x, g):
    ms = jnp.mean(x * x, axis=-1, keepdims=True)
    return x * lax.rsqrt(ms + EPS) * g


def _part_tiles(parts, tm):
    return [p.shape[0] // tm for p in parts]


def _part_specs(parts, tm, ncols, col_index=None):
    specs, start = [], 0
    for n in _part_tiles(parts, tm):
        def index(*idx, s=start, n=n):
            row = idx[0] - s
            if col_index is None:
                return jnp.clip(row, 0, n - 1), 0
            return jnp.clip(row, 0, n - 1), jnp.where((row >= 0) & (row < n), col_index(*idx), 0)
        specs.append(pl.BlockSpec((tm, ncols), index))
        start += n
    return specs


def _part_read(refs, tiles, i):
    val = refs[-1][...]
    end = sum(tiles[:-1])
    for ref, n in zip(reversed(refs[:-1]), reversed(tiles[:-1])):
        val = jnp.where(i < end, ref[...], val)
        end -= n
    return val


def _norm_dt_kernel(*refs, tiles):
    n = len(tiles)
    x_refs, (g_ref, wdt_ref, wdtT_ref, h_ref, dt_ref, dtT_ref) = refs[:n], refs[n:]
    h = _rms(_part_read(x_refs, tiles, pl.program_id(0)), g_ref[...]).astype(BF16)
    h_ref[...] = h
    dt_ref[...] = _dot(h, wdt_ref[...])
    dtT_ref[...] = _dot_nt(wdtT_ref[...], h)


def _norm_dt(x_parts, g, w_dt, w_dtT):
    D = x_parts[0].shape[1]
    M = sum(p.shape[0] for p in x_parts)
    H2 = w_dt.shape[1]
    tm = _pick(functools.reduce(math.gcd, [p.shape[0] for p in x_parts]), (512, 256, 128))
    blk = (len(x_parts) * _nbytes((tm, D), F32) + _nbytes((tm, D), BF16) + 2 * _nbytes((D, H2), BF16)
           + 2 * _nbytes((tm, H2), F32))
    return pl.pallas_call(
        functools.partial(_norm_dt_kernel, tiles=_part_tiles(x_parts, tm)),
        grid=(M // tm,),
        in_specs=_part_specs(x_parts, tm, D) + [
            pl.BlockSpec((1, D), lambda i: (0, 0)),
            pl.BlockSpec((D, H2), lambda i: (0, 0)),
            pl.BlockSpec((H2, D), lambda i: (0, 0))],
        out_specs=[pl.BlockSpec((tm, D), lambda i: (i, 0)),
                   pl.BlockSpec((tm, H2), lambda i: (i, 0)),
                   pl.BlockSpec((H2, tm), lambda i: (0, i))],
        out_shape=[jax.ShapeDtypeStruct((M, D), BF16),
                   jax.ShapeDtypeStruct((M, H2), F32),
                   jax.ShapeDtypeStruct((H2, M), F32)],
        compiler_params=_params(("arbitrary",), blk),
        name="norm_dt",
    )(*x_parts, g.reshape(1, D), w_dt, w_dtT)


def _norm_final_kernel(x_ref, g_ref, *o_refs, tiles):
    i = pl.program_id(0)
    y = _rms(x_ref[...], g_ref[...])
    start = 0
    for o_ref, n in zip(o_refs, tiles):
        @pl.when((i >= start) & (i < start + n))
        def _(o_ref=o_ref):
            o_ref[...] = y
        start += n


def _norm_final(x, g, part_rows):
    M, D = x.shape
    tm = _pick(functools.reduce(math.gcd, part_rows), (512, 256, 128))
    outs = [jax.ShapeDtypeStruct((r, D), F32) for r in part_rows]
    blk = (1 + len(part_rows)) * _nbytes((tm, D), F32)
    return pl.pallas_call(
        functools.partial(_norm_final_kernel, tiles=_part_tiles(outs, tm)),
        grid=(M // tm,),
        in_specs=[pl.BlockSpec((tm, D), lambda i: (i, 0)),
                  pl.BlockSpec((1, D), lambda i: (0, 0))],
        out_specs=_part_specs(outs, tm, D),
        out_shape=outs,
        compiler_params=_params(("arbitrary",), blk),
        name="norm_final",
    )(x, g.reshape(1, D))


def _row_rsqrt(ssq_ref, d):
    return lax.rsqrt(jnp.sum(ssq_ref[...], axis=-1, keepdims=True) * (1.0 / d) + EPS)


def _lane_partial_sumsq(v):
    sq = v * v
    part = sq[:, 0:LANES]
    for q in range(1, v.shape[1] // LANES):
        part = part + sq[:, q * LANES:(q + 1) * LANES]
    return part


def _mm_kernel(lhs_ref, w_ref, *rest, epilogue, norm_width, n_casts):
    n_in = len(rest) - 1 - 2 * n_casts
    ins, cast_srcs, out_ref, cast_dsts = rest[:n_in], rest[n_in:n_in + n_casts], rest[n_in + n_casts], rest[n_in + n_casts + 1:]
    acc = _dot(lhs_ref[...], w_ref[...])
    if norm_width:
        acc = acc * _row_rsqrt(ins[0], norm_width)
        ins = ins[1:]
    out_ref[...] = epilogue(acc, *ins).astype(out_ref.dtype)
    for src_ref, dst_ref in zip(cast_srcs, cast_dsts):
        dst_ref[...] = src_ref[...].astype(dst_ref.dtype)


def _plan_casts(arrays, steps):
    jobs, rest, used = [], [], 0
    for a in arrays:
        cols = _pick(a.shape[1], (2048, 1024, 512, 256, 128))
        fit = [r for r in (128, 256, 512, 1024, 2048, 4096) if a.shape[0] % r == 0
               and _nbytes((r, cols), F32) <= CAST_BLOCK_BYTES
               and used + (a.shape[0] // r) * (a.shape[1] // cols) <= steps]
        if fit and not rest:
            jobs.append((a, (fit[0], cols)))
            used += (a.shape[0] // fit[0]) * (a.shape[1] // cols)
        else:
            rest.append(a)
    return tuple(jobs), rest


def _cast_specs(casts, n_i, n_j):
    specs, start = [], 0
    for src, block in casts:
        n_rb, n_cb = src.shape[0] // block[0], src.shape[1] // block[1]

        def index(i, j, start=start, n=n_rb * n_cb, n_cb=n_cb):
            b = jnp.clip(i * n_j + j - start, 0, n - 1)
            return b // n_cb, b % n_cb
        specs.append(pl.BlockSpec(block, index))
        start += n_rb * n_cb
    assert start <= n_i * n_j, "not enough grid steps to carry the weight casts"
    return specs


def _matmul(lhs, w, out_dtype, name, epilogue=None, extras=(), extra_specs=None, ssq=None, tm=None, tn=None,
            col_offsets=(), w_cols=None, casts=()):
    M, K = lhs.shape
    w_off, N = w_cols if w_cols else (0, w.shape[1])
    tm = tm or _pick(M, (1024, 512, 256, 128))
    tn = next(t for t in ((tn,) if tn else ()) + (512, 256, 128)
              if N % t == 0 and all(o % t == 0 for o in col_offsets))
    if epilogue is None:
        epilogue = lambda acc: acc
    specs = extra_specs(tm, tn) if extra_specs else []
    if ssq is not None:
        extras = (ssq,) + tuple(extras)
        specs = [pl.BlockSpec((tm, LANES), lambda i, j: (i, 0))] + specs
    cast_arrays = tuple(casts)
    casts, uncast = _plan_casts(cast_arrays, (M // tm) * (N // tn))
    cast_specs = _cast_specs(casts, M // tm, N // tn)
    blk = (_nbytes((tm, K), lhs.dtype) + _nbytes((K, tn), w.dtype) + _nbytes((tm, tn), out_dtype)
           + sum(_nbytes(s.block_shape, e.dtype) for s, e in zip(specs, extras))
           + sum(_nbytes(b, F32) + _nbytes(b, BF16) for _, b in casts))
    if w_off % tn == 0:
        w_blk0 = w_off // tn
        w_spec = pl.BlockSpec((K, tn), lambda i, j: (0, w_blk0 + j))
    elif w_off % LANES == 0:
        w_spec = pl.BlockSpec((pl.Element(K), pl.Element(tn)),
                              lambda i, j: (0, pl.multiple_of(w_off + j * tn, LANES)))
    else:
        w = w[:, w_off:w_off + N]
        w_spec = pl.BlockSpec((K, tn), lambda i, j: (0, j))
    outs = pl.pallas_call(
        functools.partial(_mm_kernel, epilogue=epilogue, norm_width=K if ssq is not None else 0,
                          n_casts=len(casts)),
        grid=(M // tm, N // tn),
        in_specs=[pl.BlockSpec((tm, K), lambda i, j: (i, 0)), w_spec] + specs + cast_specs,
        out_specs=[pl.BlockSpec((tm, tn), lambda i, j: (i, j))] + cast_specs,
        out_shape=[jax.ShapeDtypeStruct((M, N), out_dtype)]
        + [jax.ShapeDtypeStruct(src.shape, BF16) for src, _ in casts],
        compiler_params=_params(("arbitrary", "arbitrary") if casts else ("parallel", "arbitrary"),
                                blk, _nbytes((tm, tn), F32)),
        name=name,
    )(lhs, w, *extras, *[src for src, _ in casts])
    if not cast_arrays:
        return outs[0]
    return outs[0], tuple(outs[1:]) + tuple(a.astype(BF16) for a in uncast)


def _tile_spec(col_offset):
    def make(tm, tn):
        assert col_offset % tn == 0
        off = col_offset // tn
        return pl.BlockSpec((tm, tn), lambda i, j: (i, off + j))
    return make


def _emit_norm_operand(x_new, g_ref, xg_ref, ssq_ref, first):
    xg_ref[...] = (x_new * g_ref[...]).astype(xg_ref.dtype)
    part = _lane_partial_sumsq(x_new)

    @pl.when(first)
    def _():
        ssq_ref[...] = part

    @pl.when(jnp.logical_not(first))
    def _():
        ssq_ref[...] += part


def _mm_res_norm_kernel(lhs_ref, w_ref, *refs, tiles):
    n = len(tiles)
    res_refs, (g_ref, x_ref, xg_ref, ssq_ref) = refs[:n], refs[n:]
    x_new = _part_read(res_refs, tiles, pl.program_id(0)) + _dot(lhs_ref[...], w_ref[...])
    x_ref[...] = x_new
    _emit_norm_operand(x_new, g_ref, xg_ref, ssq_ref, pl.program_id(1) == 0)


def _matmul_residual_norm(lhs, w, res_parts, g_next, name):
    M, K = lhs.shape
    N = w.shape[1]
    tm = _pick(functools.reduce(math.gcd, [p.shape[0] for p in res_parts]), (1024, 512, 256, 128))
    tn = _pick(N, (512, 256, 128))
    blk = (_nbytes((tm, K), lhs.dtype) + _nbytes((K, tn), w.dtype) + (1 + len(res_parts)) * _nbytes((tm, tn), F32)
           + _nbytes((tm, tn), BF16) + _nbytes((tm, LANES), F32))
    return pl.pallas_call(
        functools.partial(_mm_res_norm_kernel, tiles=_part_tiles(res_parts, tm)),
        grid=(M // tm, N // tn),
        in_specs=[pl.BlockSpec((tm, K), lambda i, j: (i, 0)),
                  pl.BlockSpec((K, tn), lambda i, j: (0, j))]
        + _part_specs(res_parts, tm, tn, col_index=lambda i, j: j)
        + [pl.BlockSpec((1, tn), lambda i, j: (0, j))],
        out_specs=[pl.BlockSpec((tm, tn), lambda i, j: (i, j)),
                   pl.BlockSpec((tm, tn), lambda i, j: (i, j)),
                   pl.BlockSpec((tm, LANES), lambda i, j: (i, 0))],
        out_shape=[jax.ShapeDtypeStruct((M, N), F32),
                   jax.ShapeDtypeStruct((M, N), BF16),
                   jax.ShapeDtypeStruct((M, LANES), F32)],
        compiler_params=_params(("arbitrary", "arbitrary"), blk, _nbytes((tm, tn), F32)),
        name=name,
    )(lhs, w, *res_parts, g_next.reshape(1, N))


def _mm_kacc_kernel(lhs_ref, w_ref, res_ref, g_ref, x_ref, xg_ref, ssq_ref, *, nk):
    k = pl.program_id(2)

    def step(base_ref, finish):
        x_new = base_ref[...] + _dot(lhs_ref[...], w_ref[...])
        x_ref[...] = x_new
        if finish:
            _emit_norm_operand(x_new, g_ref, xg_ref, ssq_ref, pl.program_id(1) == 0)

    if nk == 1:
        step(res_ref, True)
        return
    pl.when(k == 0)(lambda: step(res_ref, False))
    if nk > 2:
        pl.when((k > 0) & (k < nk - 1))(lambda: step(x_ref, False))
    pl.when(k == nk - 1)(lambda: step(x_ref, True))


def _matmul_kacc_residual_norm(lhs, w, res, g_next, name):
    M, K = lhs.shape
    N = w.shape[1]
    tm = _pick(M, (1024, 512, 256, 128))
    tn = _pick(N, (1024, 512, 256, 128))
    tk = _pick(K, (2048, 1024, 512, 256, 128))
    blk = (_nbytes((tm, tk), lhs.dtype) + _nbytes((tk, tn), w.dtype) + 2 * _nbytes((tm, tn), F32)
           + _nbytes((tm, tn), BF16) + _nbytes((tm, LANES), F32))
    return pl.pallas_call(
        functools.partial(_mm_kacc_kernel, nk=K // tk),
        grid=(M // tm, N // tn, K // tk),
        in_specs=[pl.BlockSpec((tm, tk), lambda i, j, k: (i, k)),
                  pl.BlockSpec((tk, tn), lambda i, j, k: (k, j)),
                  pl.BlockSpec((tm, tn), lambda i, j, k: (i, j)),
                  pl.BlockSpec((1, tn), lambda i, j, k: (0, j))],
        out_specs=[pl.BlockSpec((tm, tn), lambda i, j, k: (i, j)),
                   pl.BlockSpec((tm, tn), lambda i, j, k: (i, j)),
                   pl.BlockSpec((tm, LANES), lambda i, j, k: (i, 0))],
        out_shape=[jax.ShapeDtypeStruct((M, N), F32),
                   jax.ShapeDtypeStruct((M, N), BF16),
                   jax.ShapeDtypeStruct((M, LANES), F32)],
        compiler_params=_params(("parallel", "arbitrary", "arbitrary"), blk, _nbytes((tm, tn), F32)),
        name=name,
    )(lhs, w, res, g_next.reshape(1, N))


def _conv_kernel(prev_ref, cur_ref, next_ref, w_ref, b_ref, shift_ref, out_ref, buf_ref, *, tm, seq_tiles, width):
    i = pl.program_id(0)
    first = (i % seq_tiles) == 0
    last = (i % seq_tiles) == seq_tiles - 1
    prev = prev_ref[...]
    nxt = next_ref[...]
    buf_ref[0:HALO, :] = jnp.where(first, jnp.zeros_like(prev), prev)
    buf_ref[HALO:HALO + tm, :] = cur_ref[...]
    buf_ref[HALO + tm:, :] = jnp.where(last, jnp.zeros_like(nxt), nxt)

    half = width // 2
    taps = [k for k in range(width) if k != half]
    sub = shift_ref.shape[0]
    win = sub + 2 * HALO
    shift = shift_ref[...]
    w_taps = [w_ref[k:k + 1, :].astype(BF16) for k in taps]
    for r0 in range(0, tm, sub):
        window = buf_ref[r0:r0 + win, :]
        scaled = jnp.concatenate([window * wk for wk in w_taps], axis=0)
        acc = (b_ref[...] + w_ref[half:half + 1, :] * window[HALO:HALO + sub].astype(F32)
               + _dot(shift, scaled))
        out_ref[r0:r0 + sub, :] = (acc * _sigmoid(acc)).astype(out_ref.dtype)


def _conv_silu(proj, col_offset, conv_w, conv_b, seq):
    M = proj.shape[0]
    width, C = conv_w.shape
    assert width // 2 <= HALO
    tm = _pick(seq, (512, 256, 128))
    tc = _pick(C, (2048, 1024, 512, 256, 128))
    while col_offset % tc:
        tc //= 2
    off = col_offset // tc
    hb = tm // HALO
    n_halo = M // HALO
    half = width // 2
    sub = min(tm, CONV_SUB_ROWS)
    win = sub + 2 * HALO
    rows = jnp.arange(sub, dtype=jnp.int32)[:, None]
    cols = jnp.arange(win, dtype=jnp.int32)[None, :]
    shift = jnp.concatenate([(cols == rows + HALO + (k - half)) for k in range(width) if k != half],
                            axis=1).astype(BF16)
    blk = (2 * _nbytes((tm, tc), BF16) + 2 * _nbytes((HALO, tc), BF16) + _nbytes((width + 1, tc), F32)
           + _nbytes(shift.shape, BF16))
    return pl.pallas_call(
        functools.partial(_conv_kernel, tm=tm, seq_tiles=seq // tm, width=width),
        grid=(M // tm, C // tc),
        in_specs=[pl.BlockSpec((HALO, tc), lambda i, j: (jnp.maximum(i * hb - 1, 0), off + j)),
                  pl.BlockSpec((tm, tc), lambda i, j: (i, off + j)),
                  pl.BlockSpec((HALO, tc), lambda i, j: (jnp.minimum((i + 1) * hb, n_halo - 1), off + j)),
                  pl.BlockSpec((width, tc), lambda i, j: (0, j)),
                  pl.BlockSpec((1, tc), lambda i, j: (0, j)),
                  pl.BlockSpec(shift.shape, lambda i, j: (0, 0))],
        out_specs=pl.BlockSpec((tm, tc), lambda i, j: (i, j)),
        out_shape=jax.ShapeDtypeStruct((M, C), BF16),
        scratch_shapes=[pltpu.VMEM((tm + 2 * HALO, tc), BF16)],
        compiler_params=_params(("parallel", "parallel"), blk,
                                _nbytes((tm + 2 * HALO, tc), BF16) + 8 * _nbytes((CONV_SUB_ROWS, tc), F32)),
        name="conv_silu",
    )(proj, proj, proj, conv_w, conv_b.reshape(1, C), shift)


def _pool_kernel(prev_ref, cur_ref, next_ref, pw_ref, ps_ref, out_ref, *, tm, seq, Wd):
    i = pl.program_id(0)
    t0 = (i * tm) % seq
    has_prev = t0 > 0
    has_next = t0 + tm < seq
    r = lax.broadcasted_iota(jnp.int32, (tm, tm), 0)
    c = lax.broadcasted_iota(jnp.int32, (tm, tm), 1)
    rh = lax.broadcasted_iota(jnp.int32, (tm, HALO), 0)
    ch = lax.broadcasted_iota(jnp.int32, (tm, HALO), 1)
    t = lax.broadcasted_iota(jnp.int32, (tm, Wd), 0) + t0
    for g, w in enumerate(POOL_WINDOWS):
        lo_off = w // 2
        hi_off = w - lo_off
        sl = slice(g * Wd, (g + 1) * Wd)
        band = jnp.where((c >= r - lo_off) & (c < r + hi_off), 1.0, 0.0).astype(BF16)
        band_prev = jnp.where(has_prev & (ch - HALO >= rh - lo_off), 1.0, 0.0).astype(BF16)
        band_next = jnp.where(has_next & (ch + tm < rh + hi_off), 1.0, 0.0).astype(BF16)
        cur = cur_ref[:, sl]
        wsum = _dot(band, cur) + _dot(band_prev, prev_ref[:, sl]) + _dot(band_next, next_ref[:, sl])
        cnt = jnp.minimum(t + hi_off, seq) - jnp.maximum(t - lo_off, 0)
        pooled = (wsum / cnt.astype(F32) - cur.astype(F32)).astype(BF16)
        out_ref[:, sl] = (_dot(pooled, pw_ref[g]) * ps_ref[:, sl]).astype(out_ref.dtype)


def _pool(proj, col_offset, pool_w, pool_scale, seq):
    M = proj.shape[0]
    PG, Wd, _ = pool_w.shape
    PD = PG * Wd
    assert PG == len(POOL_WINDOWS) and max(POOL_WINDOWS) // 2 <= HALO
    assert col_offset % PD == 0
    off = col_offset // PD
    tm = _pick(seq, (256, 128))
    hb = tm // HALO
    n_halo = M // HALO
    blk = (2 * _nbytes((tm, PD), BF16) + 2 * _nbytes((HALO, PD), BF16) + _nbytes((PG, Wd, Wd), BF16))
    return pl.pallas_call(
        functools.partial(_pool_kernel, tm=tm, seq=seq, Wd=Wd),
        grid=(M // tm,),
        in_specs=[pl.BlockSpec((HALO, PD), lambda i: (jnp.maximum(i * hb - 1, 0), off)),
                  pl.BlockSpec((tm, PD), lambda i: (i, off)),
                  pl.BlockSpec((HALO, PD), lambda i: (jnp.minimum((i + 1) * hb, n_halo - 1), off)),
                  pl.BlockSpec((PG, Wd, Wd), lambda i: (0, 0, 0)),
                  pl.BlockSpec((1, PD), lambda i: (0, 0))],
        out_specs=pl.BlockSpec((tm, PD), lambda i: (i, 0)),
        out_shape=jax.ShapeDtypeStruct((M, PD), BF16),
        compiler_params=_params(("parallel",), blk, 4 * _nbytes((tm, PD), F32)),
        name="pool",
    )(proj, proj, proj, pool_w, pool_scale.reshape(1, PD))


def _cumsum_rows(tri, v):
    a, b, c = _split3(v)
    return _dot(tri, a) + _dot(tri, b) + _dot(tri, c)


def _cumsum_lanes(v, triT):
    a, b, c = _split3(v)
    return _dot(a, triT) + _dot(b, triT) + _dot(c, triT)


def _hilo(v):
    a, b = _split2(v)
    return jnp.concatenate([a, b], axis=1)


def _expand(v_hilo, e2):
    return _dot(v_hilo, e2)


def _ssd_state_kernel(xf_ref, bf_ref, dtf_ref, xb_ref, bb_ref, dtb_ref, bias_ref, alog_ref, e_ref,
                      pf_ref, pb_ref, h_ref, *, H, G, N, Q):
    L = SSD_CHUNK

    @pl.when(pl.program_id(1) == 0)
    def _():
        h_ref[...] = jnp.zeros_like(h_ref)

    row = lax.broadcasted_iota(jnp.int32, (L, L), 0)
    col = lax.broadcasted_iota(jnp.int32, (L, L), 1)
    tri = jnp.where(row >= col, 1.0, 0.0).astype(BF16)
    is_fwd = lax.broadcasted_iota(jnp.int32, (L, 2 * H), 1) < H
    A = -jnp.exp(alog_ref[...])

    def direction(d, x_ref, b_ref, dtraw_ref, p_ref):
        dt = _softplus(dtraw_ref[...] + bias_ref[...])
        a = dt * A
        cum = _cumsum_rows(tri, a)
        total = cum[L - 1:L, :]
        dte = jnp.exp(jnp.where(is_fwd, total - cum, cum - a))
        wexp = _expand(_hilo(dt * dte), e_ref[d])
        xs = (x_ref[...].astype(F32) * wexp).astype(BF16)
        cdexp = _expand(_hilo(jnp.broadcast_to(jnp.exp(total), (8, 2 * H))), e_ref[d])[0:1, :]
        bm = b_ref[...]
        for g in range(G):
            st = _dot_tn(bm[:, g * N:(g + 1) * N], xs[:, g * Q:(g + 1) * Q])
            h_old = h_ref[d, g]
            p_ref[0, 0, g] = h_old.astype(BF16)
            h_ref[d, g] = cdexp[:, g * Q:(g + 1) * Q] * h_old + st

    direction(0, xf_ref, bf_ref, dtf_ref, pf_ref)
    direction(1, xb_ref, bb_ref, dtb_ref, pb_ref)


def _ssd_states(xbc, dtraw, bias, alog, emat, batch, seq, D, H):
    L, G, N = SSD_CHUNK, SSD_GROUPS, SSD_STATE
    GN = G * N
    Q = D // G
    NC = seq // L
    assert D % GN == 0 or GN % D == 0
    b_off = D // GN
    H2 = 2 * H
    blk = 2 * (_nbytes((L, D), BF16) + _nbytes((L, GN), BF16) + _nbytes((L, H2), F32)
               + _nbytes((G, N, Q), BF16)) + _nbytes((2, 2 * H2, D), BF16)
    fwd = lambda b, t: b * NC + t
    bwd = lambda b, t: b * NC + (NC - 1 - t)
    out_sds = jax.ShapeDtypeStruct((batch, NC, G, N, Q), BF16)
    return pl.pallas_call(
        functools.partial(_ssd_state_kernel, H=H, G=G, N=N, Q=Q),
        grid=(batch, NC),
        in_specs=[pl.BlockSpec((L, D), lambda b, t: (fwd(b, t), 0)),
                  pl.BlockSpec((L, GN), lambda b, t: (fwd(b, t), b_off)),
                  pl.BlockSpec((L, H2), lambda b, t: (fwd(b, t), 0)),
                  pl.BlockSpec((L, D), lambda b, t: (bwd(b, t), 0)),
                  pl.BlockSpec((L, GN), lambda b, t: (bwd(b, t), b_off)),
                  pl.BlockSpec((L, H2), lambda b, t: (bwd(b, t), 0)),
                  pl.BlockSpec((1, H2), lambda b, t: (0, 0)),
                  pl.BlockSpec((1, H2), lambda b, t: (0, 0)),
                  pl.BlockSpec((2, 2 * H2, D), lambda b, t: (0, 0, 0))],
        out_specs=[pl.BlockSpec((1, 1, G, N, Q), lambda b, t: (b, t, 0, 0, 0)),
                   pl.BlockSpec((1, 1, G, N, Q), lambda b, t: (b, NC - 1 - t, 0, 0, 0))],
        out_shape=[out_sds, out_sds],
        scratch_shapes=[pltpu.VMEM((2, G, N, Q), F32)],
        compiler_params=_params(("parallel", "arbitrary"), blk,
                                _nbytes((2, G, N, Q), F32) + 4 * _nbytes((L, D), F32)),
        name="ssd_states",
    )(xbc, xbc, dtraw, xbc, xbc, dtraw, bias, alog, emat)


def _ssd_out_kernel(dsk_ref, x_ref, b_ref, c_ref, zs_ref, dt_ref, dtT_ref, pf_ref, pb_ref,
                    bias_ref, alog_ref, biasT_ref, alogT_ref, ng_ref, e_ref, out_ref,
                    *, H, G, N, Q, P):
    L = SSD_CHUNK
    E = H // G
    row = lax.broadcasted_iota(jnp.int32, (L, L), 0)
    col = lax.broadcasted_iota(jnp.int32, (L, L), 1)
    lower = row >= col
    eye = row == col
    tri = jnp.where(lower, 1.0, 0.0).astype(BF16)
    triT = jnp.where(row <= col, 1.0, 0.0).astype(BF16)
    is_fwd = lax.broadcasted_iota(jnp.int32, (L, 2 * H), 1) < H
    is_fwdT = lax.broadcasted_iota(jnp.int32, (2 * H, L), 0) < H

    dt = _softplus(dt_ref[...] + bias_ref[...])
    a = dt * (-jnp.exp(alog_ref[...]))
    cum = _cumsum_rows(tri, a)
    total = cum[L - 1:L, :]
    excl = cum - a
    col2 = jnp.where(is_fwd, cum, excl) * LOG2E
    sdec = _hilo(jnp.exp(jnp.where(is_fwd, cum, total - excl)))

    dtT = _softplus(dtT_ref[...] + biasT_ref[...])
    aT = dtT * (-jnp.exp(alogT_ref[...]))
    cumT = _cumsum_lanes(aT, triT)
    lg = jnp.log2(dtT)
    row2 = jnp.where(is_fwdT, cumT * LOG2E - lg, (cumT - aT) * LOG2E + lg)

    lane = lax.broadcasted_iota(jnp.int32, (L, 2 * P), 1)
    keep_left = jnp.where(lane < P, 1.0, 0.0).astype(BF16)
    keep_right = jnp.where(lane < P, 0.0, 1.0).astype(BF16)

    for g in range(G):
        c_g = c_ref[:, g * N:(g + 1) * N]
        cb = _dot_nt(c_g, b_ref[:, g * N:(g + 1) * N])
        cb_diag = jnp.sum(jnp.where(eye, cb, 0.0), axis=0, keepdims=True)
        pieces = []
        for e2 in range(E // 2):
            ws = []
            for h in (g * E + 2 * e2, g * E + 2 * e2 + 1):
                arg = jnp.where(lower,
                                col2[:, h:h + 1] - row2[h:h + 1, :],
                                row2[H + h:H + h + 1, :] - col2[:, H + h:H + h + 1])
                diag = cb_diag * dtT[H + h:H + h + 1, :] + dsk_ref[h]
                ws.append((cb * jnp.exp2(arg) + jnp.where(eye, diag, 0.0)).astype(BF16))
            h0 = g * E + 2 * e2
            xp = x_ref[:, h0 * P:(h0 + 2) * P]
            rhs = jnp.concatenate([xp * keep_left, xp * keep_right], axis=0)
            pieces.append(_dot(jnp.concatenate(ws, axis=1), rhs))
        y = jnp.concatenate(pieces, axis=1) if len(pieces) > 1 else pieces[0]
        sl = slice(g * Q, (g + 1) * Q)
        y = (y + _dot(c_g, pf_ref[0, 0, g]) * _expand(sdec, e_ref[0, :, sl])
             + _dot(c_g, pb_ref[0, 0, g]) * _expand(sdec, e_ref[1, :, sl]))
        y = y * zs_ref[:, sl].astype(F32)
        ms = jnp.mean(y * y, axis=-1, keepdims=True)
        out_ref[:, sl] = (y * lax.rsqrt(ms + EPS) * ng_ref[:, sl]).astype(out_ref.dtype)


def _ssd_out(zs, xbc, dtraw, dtrawT, prev_f, prev_b, bias, alog, biasT, alogT, d_skip, norm_g, emat,
             batch, seq, D, H):
    L, G, N = SSD_CHUNK, SSD_GROUPS, SSD_STATE
    GN = G * N
    Q = D // G
    P = D // H
    NC = seq // L
    M = batch * seq
    H2 = 2 * H
    assert (H // G) % 2 == 0 and 2 * P == 128 and D % GN == 0
    b_off = D // GN
    blk = (2 * _nbytes((L, D), BF16) + 2 * _nbytes((L, GN), BF16) + 2 * _nbytes((L, H2), F32)
           + 2 * _nbytes((G, N, Q), BF16) + _nbytes((2, 2 * H2, D), BF16) + _nbytes((L, D), BF16))
    rowblk = lambda b, c: b * NC + c
    return pl.pallas_call(
        functools.partial(_ssd_out_kernel, H=H, G=G, N=N, Q=Q, P=P),
        grid=(batch, NC),
        in_specs=[pl.BlockSpec(memory_space=pltpu.SMEM),
                  pl.BlockSpec((L, D), lambda b, c: (rowblk(b, c), 0)),
                  pl.BlockSpec((L, GN), lambda b, c: (rowblk(b, c), b_off)),
                  pl.BlockSpec((L, GN), lambda b, c: (rowblk(b, c), b_off + 1)),
                  pl.BlockSpec((L, D), lambda b, c: (rowblk(b, c), 0)),
                  pl.BlockSpec((L, H2), lambda b, c: (rowblk(b, c), 0)),
                  pl.BlockSpec((H2, L), lambda b, c: (0, rowblk(b, c))),
                  pl.BlockSpec((1, 1, G, N, Q), lambda b, c: (b, c, 0, 0, 0)),
                  pl.BlockSpec((1, 1, G, N, Q), lambda b, c: (b, c, 0, 0, 0)),
                  pl.BlockSpec((1, H2), lambda b, c: (0, 0)),
                  pl.BlockSpec((1, H2), lambda b, c: (0, 0)),
                  pl.BlockSpec((H2, L), lambda b, c: (0, 0)),
                  pl.BlockSpec((H2, L), lambda b, c: (0, 0)),
                  pl.BlockSpec((1, D), lambda b, c: (0, 0)),
                  pl.BlockSpec((2, 2 * H2, D), lambda b, c: (0, 0, 0))],
        out_specs=pl.BlockSpec((L, D), lambda b, c: (rowblk(b, c), 0)),
        out_shape=jax.ShapeDtypeStruct((M, D), BF16),
        compiler_params=_params(("parallel", "parallel"), blk, 8 * _nbytes((L, D), F32)),
        name="ssd_out",
    )(d_skip, xbc, xbc, xbc, zs, dtraw, dtrawT, prev_f, prev_b, bias, alog, biasT, alogT,
      norm_g.reshape(1, D), emat)


def _prepare_layer(norm_mix_g, w_in, conv_w, conv_b, dt_bias_f, dt_bias_b, a_log_f, a_log_b, d_skip,
                   ssd_norm_g, w_ssd_up, pool_w, pool_scale, w_pool_up, w_out, norm_mlp_g, w_ff1, w_ff2,
                   norm_ple_g, w_ple_gate, w_ple_proj):
    D = w_in.shape[0]
    H = d_skip.shape[0]
    C = conv_w.shape[1]
    s_xbc = D + C
    s_dt = s_xbc + 2 * H
    w_dt = w_in[:, s_xbc:s_dt].astype(BF16)
    head_of_lane = jnp.arange(D, dtype=jnp.int32) // (D // H)
    onehot = (jnp.arange(H, dtype=jnp.int32)[:, None] == head_of_lane[None, :]).astype(BF16)
    zeros = jnp.zeros_like(onehot)
    L = SSD_CHUNK
    bias = jnp.concatenate([dt_bias_f, dt_bias_b])
    alog = jnp.concatenate([a_log_f, a_log_b])
    return dict(
        D=D, H=H, C=C,
        norm_mix_g=norm_mix_g,
        w_in=w_in.astype(BF16), s_dt=s_dt,
        w_dt=w_dt, w_dtT=w_dt.T,
        conv_w=conv_w, conv_b=conv_b,
        bias=bias.reshape(1, 2 * H), alog=alog.reshape(1, 2 * H),
        biasT=jnp.broadcast_to(bias[:, None], (2 * H, L)), alogT=jnp.broadcast_to(alog[:, None], (2 * H, L)),
        emat=jnp.stack([jnp.concatenate([onehot, zeros, onehot, zeros]),
                        jnp.concatenate([zeros, onehot, zeros, onehot])]),
        d_skip=d_skip, ssd_norm_g=ssd_norm_g,
        w_ssd_up=w_ssd_up, w_pool_up=w_pool_up, w_out=w_out, w_ff1=w_ff1, w_ff2=w_ff2, w_ple_gate=w_ple_gate,
        pool_w=pool_w.astype(BF16), pool_scale=pool_scale,
        norm_mlp_g=norm_mlp_g, norm_ple_g=norm_ple_g, w_ple_proj=w_ple_proj.astype(BF16),
    )


def _layer(x_parts, p_parts, lp, batch, seq):
    D, H = lp["D"], lp["H"]
    PD = lp["w_pool_up"].shape[0]
    off_ga, off_gb = PD, PD + D

    h, dtraw, dtrawT = _norm_dt(x_parts, lp["norm_mix_g"], lp["w_dt"], lp["w_dtT"])
    zs, (w_ssd_up, w_pool_up) = _matmul(
        h, lp["w_in"], BF16, "in_proj_z", epilogue=lambda acc: acc * _sigmoid(acc), tn=_TN_WIDE,
        w_cols=(0, D), casts=(lp["w_ssd_up"], lp["w_pool_up"]))
    tail, (w_ff1, w_out) = _matmul(
        h, lp["w_in"], BF16, "in_proj_tail", tn=_TN_WIDE, w_cols=(lp["s_dt"], PD + 2 * D),
        casts=(lp["w_ff1"], lp["w_out"]))

    xbc_raw, (w_ple_gate,) = _matmul(h, lp["w_in"], BF16, "in_proj_xbc", tn=_TN_WIDE, w_cols=(D, lp["C"]),
                                     casts=(lp["w_ple_gate"],))
    xbc = _conv_silu(xbc_raw, 0, lp["conv_w"], lp["conv_b"], seq)
    prev_f, prev_b = _ssd_states(xbc, dtraw, lp["bias"], lp["alog"], lp["emat"], batch, seq, D, H)
    y = _ssd_out(zs, xbc, dtraw, dtrawT, prev_f, prev_b, lp["bias"], lp["alog"], lp["biasT"],
                 lp["alogT"], lp["d_skip"], lp["ssd_norm_g"], lp["emat"], batch, seq, D, H)
    gated_a = _matmul(y, w_ssd_up, BF16, "ssd_up", tn=_TN_WIDE,
                      epilogue=lambda acc, g_ref: _sigmoid(g_ref[...].astype(F32)) * acc,
                      extras=(tail,), col_offsets=(off_ga,),
                      extra_specs=lambda tm, tn: [_tile_spec(off_ga)(tm, tn)])

    pooled = _pool(tail, 0, lp["pool_w"], lp["pool_scale"], seq)
    mixed = _matmul(pooled, w_pool_up, BF16, "pool_up", tn=_TN_WIDE,
                    epilogue=lambda acc, g_ref, a_ref: (_sigmoid(g_ref[...].astype(F32)) * acc
                                                        + a_ref[...].astype(F32)),
                    extras=(tail, gated_a), col_offsets=(off_gb,),
                    extra_specs=lambda tm, tn: [_tile_spec(off_gb)(tm, tn), _tile_spec(0)(tm, tn)])
    x, xg, ssq = _matmul_residual_norm(mixed, w_out, x_parts, lp["norm_mlp_g"], "out_proj")

    ff, (w_ff2,) = _matmul(xg, w_ff1, BF16, "ff1", ssq=ssq, tn=_TN_WIDE, casts=(lp["w_ff2"],),
                           epilogue=lambda acc: jnp.square(jnp.maximum(acc, 0.0)))
    x, xg, ssq = _matmul_kacc_residual_norm(ff, w_ff2, x, lp["norm_ple_g"], "ff2")

    ple = p_parts[0].shape[1]
    tm = _pick(functools.reduce(math.gcd, [p.shape[0] for p in p_parts]), (1024, 512, 256, 128))
    tiles = _part_tiles(p_parts, tm)

    def ple_epilogue(acc, x_ref, *refs):
        p = _part_read(refs[:-1], tiles, pl.program_id(0))
        return x_ref[...] + _sigmoid(acc) * _dot(p.astype(BF16), refs[-1][...])

    return _matmul(xg, w_ple_gate, F32, "ple", ssq=ssq, tm=tm, epilogue=ple_epilogue,
                   extras=(x, *p_parts, lp["w_ple_proj"]),
                   extra_specs=lambda tm, tn: ([_tile_spec(0)(tm, tn)] + _part_specs(p_parts, tm, ple)
                                               + [pl.BlockSpec((ple, tn), lambda i, j: (0, j))]))


def _run_groups(xs, ps, layers, norm_final_g):
    seq, D = xs[0].shape[1:]
    assert seq % SSD_CHUNK == 0
    batch = sum(x.shape[0] for x in xs)
    x_parts = [x.reshape(-1, D) for x in xs]
    for i, lp in enumerate(layers):
        p_parts = [p[i].reshape(-1, p.shape[-1]) for p in ps]
        x_parts = [_layer(x_parts, p_parts, lp, batch, seq)]
    outs = _norm_final(x_parts[0] if len(x_parts) == 1 else jnp.concatenate(x_parts), norm_final_g,
                       [x.shape[0] * seq for x in xs])
    return [o.reshape(x.shape) for o, x in zip(outs, xs)]


def kernel(x_prompt, x_sample, p_prompt, p_sample, norm_mix_g, w_in, conv_w, conv_b, dt_bias_f, dt_bias_b, a_log_f, a_log_b, d_skip, ssd_norm_g, w_ssd_up, pool_w, pool_scale, w_pool_up, w_out, norm_mlp_g, w_ff1, w_ff2, norm_ple_g, w_ple_gate, w_ple_proj, norm_final_g):
    stacked = (norm_mix_g, w_in, conv_w, conv_b, dt_bias_f, dt_bias_b, a_log_f, a_log_b, d_skip,
               ssd_norm_g, w_ssd_up, pool_w, pool_scale, w_pool_up, w_out, norm_mlp_g, w_ff1, w_ff2,
               norm_ple_g, w_ple_gate, w_ple_proj)
    layers = [_prepare_layer(*[w[i] for w in stacked]) for i in range(w_in.shape[0])]
    if x_prompt.shape[1] == x_sample.shape[1]:
        y_prompt, y_sample = _run_groups([x_prompt, x_sample], [p_prompt, p_sample], layers, norm_final_g)
    else:
        (y_prompt,) = _run_groups([x_prompt], [p_prompt], layers, norm_final_g)
        (y_sample,) = _run_groups([x_sample], [p_sample], layers, norm_final_g)
    return (y_prompt, y_sample)
```

```python
import functools
import math

import jax
import jax.numpy as jnp
from jax import lax
from jax.experimental import pallas as pl
from jax.experimental.pallas import tpu as pltpu

F32 = jnp.float32
BF16 = jnp.bfloat16

EPS = 1e-6
LOG2E = 1.4426950408889634
LN2 = 0.6931471805599453
SSD_GROUPS = 8
SSD_STATE = 128
SSD_CHUNK = 128
POOL_WINDOWS = (2, 4, 8, 16)

V7X_VMEM_BYTES = 64 * 1024 * 1024
V7X_VMEM_RESERVE_BYTES = 8 * 1024 * 1024
BF16_SUBLANE_TILE = 16
LANES = 128
HALO = BF16_SUBLANE_TILE
_TN_WIDE = 1024
CONV_SUB_ROWS = 128
CAST_BLOCK_BYTES = 2 * 1024 * 1024


def _pick(dim, prefs):
    for p in prefs:
        if dim % p == 0:
            return p
    return dim


def _nbytes(shape, dtype):
    n = 1
    for s in shape:
        n *= s
    return n * jnp.dtype(dtype).itemsize


def _params(semantics, block_bytes, scratch_bytes=0):
    est = 2 * block_bytes + scratch_bytes
    limit = min(2 * est + (4 << 20), V7X_VMEM_BYTES - V7X_VMEM_RESERVE_BYTES)
    return pltpu.CompilerParams(dimension_semantics=semantics, vmem_limit_bytes=int(limit))


def _sigmoid(v):
    return 1.0 / (1.0 + jnp.exp(-v))


def _softplus(v):
    u = jnp.exp(-jnp.abs(v))
    w = 1.0 + u
    return jnp.maximum(v, 0.0) + jnp.where(w == 1.0, u, (jnp.log2(w) * LN2) * (u / (w - 1.0)))


def _split3(v):
    a = v.astype(BF16)
    r = v - a.astype(F32)
    b = r.astype(BF16)
    c = (r - b.astype(F32)).astype(BF16)
    return a, b, c


def _split2(v):
    a = v.astype(BF16)
    b = (v - a.astype(F32)).astype(BF16)
    return a, b


def _dot(a, b):
    return jnp.dot(a, b, preferred_element_type=F32)


def _dot_nt(a, b):
    return lax.dot_general(a, b, (((1,), (1,)), ((), ())), preferred_element_type=F32)


def _dot_tn(a, b):
    return lax.dot_general(a, b, (((0,), (0,)), ((), ())), preferred_element_type=F32)


def _rms(x, g):
    ms = jnp.mean(x * x, axis=-1, keepdims=True)
    return x * lax.rsqrt(ms + EPS) * g


def _part_tiles(parts, tm):
    return [p.shape[0] // tm for p in parts]


def _part_specs(parts, tm, ncols, col_index=None):
    specs, start = [], 0
    for n in _part_tiles(parts, tm):
        def index(*idx, s=start, n=n):
            row = idx[0] - s
            if col_index is None:
                return jnp.clip(row, 0, n - 1), 0
            return jnp.clip(row, 0, n - 1), jnp.where((row >= 0) & (row < n), col_index(*idx), 0)
        specs.append(pl.BlockSpec((tm, ncols), index))
        start += n
    return specs


def _part_read(refs, tiles, i):
    val = refs[-1][...]
    end = sum(tiles[:-1])
    for ref, n in zip(reversed(refs[:-1]), reversed(tiles[:-1])):
        val = jnp.where(i < end, ref[...], val)
        end -= n
    return val


def _norm_dt_kernel(*refs, tiles):
    n = len(tiles)
    x_refs, (g_ref, wdt_ref, wdtT_ref, h_ref, dt_ref, dtT_ref) = refs[:n], refs[n:]
    h = _rms(_part_read(x_refs, tiles, pl.program_id(0)), g_ref[...]).astype(BF16)
    h_ref[...] = h
    dt_ref[...] = _dot(h, wdt_ref[...])
    dtT_ref[...] = _dot_nt(wdtT_ref[...], h)


def _norm_dt(x_parts, g, w_dt, w_dtT):
    D = x_parts[0].shape[1]
    M = sum(p.shape[0] for p in x_parts)
    H2 = w_dt.shape[1]
    tm = _pick(functools.reduce(math.gcd, [p.shape[0] for p in x_parts]), (512, 256, 128))
    blk = (len(x_parts) * _nbytes((tm, D), F32) + _nbytes((tm, D), BF16) + 2 * _nbytes((D, H2), BF16)
           + 2 * _nbytes((tm, H2), F32))
    return pl.pallas_call(
        functools.partial(_norm_dt_kernel, tiles=_part_tiles(x_parts, tm)),
        grid=(M // tm,),
        in_specs=_part_specs(x_parts, tm, D) + [
            pl.BlockSpec((1, D), lambda i: (0, 0)),
            pl.BlockSpec((D, H2), lambda i: (0, 0)),
            pl.BlockSpec((H2, D), lambda i: (0, 0))],
        out_specs=[pl.BlockSpec((tm, D), lambda i: (i, 0)),
                   pl.BlockSpec((tm, H2), lambda i: (i, 0)),
                   pl.BlockSpec((H2, tm), lambda i: (0, i))],
        out_shape=[jax.ShapeDtypeStruct((M, D), BF16),
                   jax.ShapeDtypeStruct((M, H2), F32),
                   jax.ShapeDtypeStruct((H2, M), F32)],
        compiler_params=_params(("arbitrary",), blk),
        name="norm_dt",
    )(*x_parts, g.reshape(1, D), w_dt, w_dtT)


def _norm_final_kernel(x_ref, g_ref, *o_refs, tiles):
    i = pl.program_id(0)
    y = _rms(x_ref[...], g_ref[...])
    start = 0
    for o_ref, n in zip(o_refs, tiles):
        @pl.when((i >= start) & (i < start + n))
        def _(o_ref=o_ref):
            o_ref[...] = y
        start += n


def _norm_final(x, g, part_rows):
    M, D = x.shape
    tm = _pick(functools.reduce(math.gcd, part_rows), (512, 256, 128))
    outs = [jax.ShapeDtypeStruct((r, D), F32) for r in part_rows]
    blk = (1 + len(part_rows)) * _nbytes((tm, D), F32)
    return pl.pallas_call(
        functools.partial(_norm_final_kernel, tiles=_part_tiles(outs, tm)),
        grid=(M // tm,),
        in_specs=[pl.BlockSpec((tm, D), lambda i: (i, 0)),
                  pl.BlockSpec((1, D), lambda i: (0, 0))],
        out_specs=_part_specs(outs, tm, D),
        out_shape=outs,
        compiler_params=_params(("arbitrary",), blk),
        name="norm_final",
    )(x, g.reshape(1, D))


def _row_rsqrt(ssq_ref, d):
    return lax.rsqrt(jnp.sum(ssq_ref[...], axis=-1, keepdims=True) * (1.0 / d) + EPS)


def _lane_partial_sumsq(v):
    sq = v * v
    part = sq[:, 0:LANES]
    for q in range(1, v.shape[1] // LANES):
        part = part + sq[:, q * LANES:(q + 1) * LANES]
    return part


def _mm_kernel(lhs_ref, w_ref, *rest, epilogue, norm_width, n_casts):
    n_in = len(rest) - 1 - 2 * n_casts
    ins, cast_srcs, out_ref, cast_dsts = rest[:n_in], rest[n_in:n_in + n_casts], rest[n_in + n_casts], rest[n_in + n_casts + 1:]
    acc = _dot(lhs_ref[...], w_ref[...])
    if norm_width:
        acc = acc * _row_rsqrt(ins[0], norm_width)
        ins = ins[1:]
    out_ref[...] = epilogue(acc, *ins).astype(out_ref.dtype)
    for src_ref, dst_ref in zip(cast_srcs, cast_dsts):
        dst_ref[...] = src_ref[...].astype(dst_ref.dtype)


def _plan_casts(arrays, steps):
    jobs, rest, used = [], [], 0
    for a in arrays:
        cols = _pick(a.shape[1], (2048, 1024, 512, 256, 128))
        fit = [r for r in (128, 256, 512, 1024, 2048, 4096) if a.shape[0] % r == 0
               and _nbytes((r, cols), F32) <= CAST_BLOCK_BYTES
               and used + (a.shape[0] // r) * (a.shape[1] // cols) <= steps]
        if fit and not rest:
            jobs.append((a, (fit[0], cols)))
            used += (a.shape[0] // fit[0]) * (a.shape[1] // cols)
        else:
            rest.append(a)
    return tuple(jobs), rest


def _cast_specs(casts, n_i, n_j):
    specs, start = [], 0
    for src, block in casts:
        n_rb, n_cb = src.shape[0] // block[0], src.shape[1] // block[1]

        def index(i, j, start=start, n=n_rb * n_cb, n_cb=n_cb):
            b = jnp.clip(i * n_j + j - start, 0, n - 1)
            return b // n_cb, b % n_cb
        specs.append(pl.BlockSpec(block, index))
        start += n_rb * n_cb
    assert start <= n_i * n_j, "not enough grid steps to carry the weight casts"
    return specs


def _matmul(lhs, w, out_dtype, name, epilogue=None, extras=(), extra_specs=None, ssq=None, tm=None, tn=None,
            col_offsets=(), w_cols=None, casts=()):
    M, K = lhs.shape
    w_off, N = w_cols if w_cols else (0, w.shape[1])
    tm = tm or _pick(M, (1024, 512, 256, 128))
    tn = next(t for t in ((tn,) if tn else ()) + (512, 256, 128)
              if N % t == 0 and all(o % t == 0 for o in col_offsets))
    if epilogue is None:
        epilogue = lambda acc: acc
    specs = extra_specs(tm, tn) if extra_specs else []
    if ssq is not None:
        extras = (ssq,) + tuple(extras)
        specs = [pl.BlockSpec((tm, LANES), lambda i, j: (i, 0))] + specs
    cast_arrays = tuple(casts)
    casts, uncast = _plan_casts(cast_arrays, (M // tm) * (N // tn))
    cast_specs = _cast_specs(casts, M // tm, N // tn)
    blk = (_nbytes((tm, K), lhs.dtype) + _nbytes((K, tn), w.dtype) + _nbytes((tm, tn), out_dtype)
           + sum(_nbytes(s.block_shape, e.dtype) for s, e in zip(specs, extras))
           + sum(_nbytes(b, F32) + _nbytes(b, BF16) for _, b in casts))
    if w_off % tn == 0:
        w_blk0 = w_off // tn
        w_spec = pl.BlockSpec((K, tn), lambda i, j: (0, w_blk0 + j))
    elif w_off % LANES == 0:
        w_spec = pl.BlockSpec((pl.Element(K), pl.Element(tn)),
                              lambda i, j: (0, pl.multiple_of(w_off + j * tn, LANES)))
    else:
        w = w[:, w_off:w_off + N]
        w_spec = pl.BlockSpec((K, tn), lambda i, j: (0, j))
    outs = pl.pallas_call(
        functools.partial(_mm_kernel, epilogue=epilogue, norm_width=K if ssq is not None else 0,
                          n_casts=len(casts)),
        grid=(M // tm, N // tn),
        in_specs=[pl.BlockSpec((tm, K), lambda i, j: (i, 0)), w_spec] + specs + cast_specs,
        out_specs=[pl.BlockSpec((tm, tn), lambda i, j: (i, j))] + cast_specs,
        out_shape=[jax.ShapeDtypeStruct((M, N), out_dtype)]
        + [jax.ShapeDtypeStruct(src.shape, BF16) for src, _ in casts],
        compiler_params=_params(("arbitrary", "arbitrary") if casts else ("parallel", "arbitrary"),
                                blk, _nbytes((tm, tn), F32)),
        name=name,
    )(lhs, w, *extras, *[src for src, _ in casts])
    if not cast_arrays:
        return outs[0]
    return outs[0], tuple(outs[1:]) + tuple(a.astype(BF16) for a in uncast)


def _tile_spec(col_offset):
    def make(tm, tn):
        assert col_offset % tn == 0
        off = col_offset // tn
        return pl.BlockSpec((tm, tn), lambda i, j: (i, off + j))
    return make


def _emit_norm_operand(x_new, g_ref, xg_ref, ssq_ref, first):
    xg_ref[...] = (x_new * g_ref[...]).astype(xg_ref.dtype)
    part = _lane_partial_sumsq(x_new)

    @pl.when(first)
    def _():
        ssq_ref[...] = part

    @pl.when(jnp.logical_not(first))
    def _():
        ssq_ref[...] += part


def _mm_res_norm_kernel(lhs_ref, w_ref, *refs, tiles):
    n = len(tiles)
    res_refs, (g_ref, x_ref, xg_ref, ssq_ref) = refs[:n], refs[n:]
    x_new = _part_read(res_refs, tiles, pl.program_id(0)) + _dot(lhs_ref[...], w_ref[...])
    x_ref[...] = x_new
    _emit_norm_operand(x_new, g_ref, xg_ref, ssq_ref, pl.program_id(1) == 0)


def _matmul_residual_norm(lhs, w, res_parts, g_next, name):
    M, K = lhs.shape
    N = w.shape[1]
    tm = _pick(functools.reduce(math.gcd, [p.shape[0] for p in res_parts]), (1024, 512, 256, 128))
    tn = _pick(N, (512, 256, 128))
    blk = (_nbytes((tm, K), lhs.dtype) + _nbytes((K, tn), w.dtype) + (1 + len(res_parts)) * _nbytes((tm, tn), F32)
           + _nbytes((tm, tn), BF16) + _nbytes((tm, LANES), F32))
    return pl.pallas_call(
        functools.partial(_mm_res_norm_kernel, tiles=_part_tiles(res_parts, tm)),
        grid=(M // tm, N // tn),
        in_specs=[pl.BlockSpec((tm, K), lambda i, j: (i, 0)),
                  pl.BlockSpec((K, tn), lambda i, j: (0, j))]
        + _part_specs(res_parts, tm, tn, col_index=lambda i, j: j)
        + [pl.BlockSpec((1, tn), lambda i, j: (0, j))],
        out_specs=[pl.BlockSpec((tm, tn), lambda i, j: (i, j)),
                   pl.BlockSpec((tm, tn), lambda i, j: (i, j)),
                   pl.BlockSpec((tm, LANES), lambda i, j: (i, 0))],
        out_shape=[jax.ShapeDtypeStruct((M, N), F32),
                   jax.ShapeDtypeStruct((M, N), BF16),
                   jax.ShapeDtypeStruct((M, LANES), F32)],
        compiler_params=_params(("arbitrary", "arbitrary"), blk, _nbytes((tm, tn), F32)),
        name=name,
    )(lhs, w, *res_parts, g_next.reshape(1, N))


def _mm_kacc_kernel(lhs_ref, w_ref, res_ref, g_ref, x_ref, xg_ref, ssq_ref, *, nk):
    k = pl.program_id(2)

    def step(base_ref, finish):
        x_new = base_ref[...] + _dot(lhs_ref[...], w_ref[...])
        x_ref[...] = x_new
        if finish:
            _emit_norm_operand(x_new, g_ref, xg_ref, ssq_ref, pl.program_id(1) == 0)

    if nk == 1:
        step(res_ref, True)
        return
    pl.when(k == 0)(lambda: step(res_ref, False))
    if nk > 2:
        pl.when((k > 0) & (k < nk - 1))(lambda: step(x_ref, False))
    pl.when(k == nk - 1)(lambda: step(x_ref, True))


def _matmul_kacc_residual_norm(lhs, w, res, g_next, name):
    M, K = lhs.shape
    N = w.shape[1]
    tm = _pick(M, (1024, 512, 256, 128))
    tn = _pick(N, (1024, 512, 256, 128))
    tk = _pick(K, (2048, 1024, 512, 256, 128))
    blk = (_nbytes((tm, tk), lhs.dtype) + _nbytes((tk, tn), w.dtype) + 2 * _nbytes((tm, tn), F32)
           + _nbytes((tm, tn), BF16) + _nbytes((tm, LANES), F32))
    return pl.pallas_call(
        functools.partial(_mm_kacc_kernel, nk=K // tk),
        grid=(M // tm, N // tn, K // tk),
        in_specs=[pl.BlockSpec((tm, tk), lambda i, j, k: (i, k)),
                  pl.BlockSpec((tk, tn), lambda i, j, k: (k, j)),
                  pl.BlockSpec((tm, tn), lambda i, j, k: (i, j)),
                  pl.BlockSpec((1, tn), lambda i, j, k: (0, j))],
        out_specs=[pl.BlockSpec((tm, tn), lambda i, j, k: (i, j)),
                   pl.BlockSpec((tm, tn), lambda i, j, k: (i, j)),
                   pl.BlockSpec((tm, LANES), lambda i, j, k: (i, 0))],
        out_shape=[jax.ShapeDtypeStruct((M, N), F32),
                   jax.ShapeDtypeStruct((M, N), BF16),
                   jax.ShapeDtypeStruct((M, LANES), F32)],
        compiler_params=_params(("parallel", "arbitrary", "arbitrary"), blk, _nbytes((tm, tn), F32)),
        name=name,
    )(lhs, w, res, g_next.reshape(1, N))


def _conv_kernel(prev_ref, cur_ref, next_ref, w_ref, b_ref, shift_ref, out_ref, buf_ref, *, tm, seq_tiles, width):
    i = pl.program_id(0)
    first = (i % seq_tiles) == 0
    last = (i % seq_tiles) == seq_tiles - 1
    prev = prev_ref[...]
    nxt = next_ref[...]
    buf_ref[0:HALO, :] = jnp.where(first, jnp.zeros_like(prev), prev)
    buf_ref[HALO:HALO + tm, :] = cur_ref[...]
    buf_ref[HALO + tm:, :] = jnp.where(last, jnp.zeros_like(nxt), nxt)

    half = width // 2
    taps = [k for k in range(width) if k != half]
    sub = shift_ref.shape[0]
    win = sub + 2 * HALO
    shift = shift_ref[...]
    w_taps = [w_ref[k:k + 1, :].astype(BF16) for k in taps]
    for r0 in range(0, tm, sub):
        window = buf_ref[r0:r0 + win, :]
        scaled = jnp.concatenate([window * wk for wk in w_taps], axis=0)
        acc = (b_ref[...] + w_ref[half:half + 1, :] * window[HALO:HALO + sub].astype(F32)
               + _dot(shift, scaled))
        out_ref[r0:r0 + sub, :] = (acc * _sigmoid(acc)).astype(out_ref.dtype)


def _conv_silu(proj, col_offset, conv_w, conv_b, seq):
    M = proj.shape[0]
    width, C = conv_w.shape
    assert width // 2 <= HALO
    tm = _pick(seq, (512, 256, 128))
    tc = _pick(C, (2048, 1024, 512, 256, 128))
    while col_offset % tc:
        tc //= 2
    off = col_offset // tc
    hb = tm // HALO
    n_halo = M // HALO
    half = width // 2
    sub = min(tm, CONV_SUB_ROWS)
    win = sub + 2 * HALO
    rows = jnp.arange(sub, dtype=jnp.int32)[:, None]
    cols = jnp.arange(win, dtype=jnp.int32)[None, :]
    shift = jnp.concatenate([(cols == rows + HALO + (k - half)) for k in range(width) if k != half],
                            axis=1).astype(BF16)
    blk = (2 * _nbytes((tm, tc), BF16) + 2 * _nbytes((HALO, tc), BF16) + _nbytes((width + 1, tc), F32)
           + _nbytes(shift.shape, BF16))
    return pl.pallas_call(
        functools.partial(_conv_kernel, tm=tm, seq_tiles=seq // tm, width=width),
        grid=(M // tm, C // tc),
        in_specs=[pl.BlockSpec((HALO, tc), lambda i, j: (jnp.maximum(i * hb - 1, 0), off + j)),
                  pl.BlockSpec((tm, tc), lambda i, j: (i, off + j)),
                  pl.BlockSpec((HALO, tc), lambda i, j: (jnp.minimum((i + 1) * hb, n_halo - 1), off + j)),
                  pl.BlockSpec((width, tc), lambda i, j: (0, j)),
                  pl.BlockSpec((1, tc), lambda i, j: (0, j)),
                  pl.BlockSpec(shift.shape, lambda i, j: (0, 0))],
        out_specs=pl.BlockSpec((tm, tc), lambda i, j: (i, j)),
        out_shape=jax.ShapeDtypeStruct((M, C), BF16),
        scratch_shapes=[pltpu.VMEM((tm + 2 * HALO, tc), BF16)],
        compiler_params=_params(("parallel", "parallel"), blk,
                                _nbytes((tm + 2 * HALO, tc), BF16) + 8 * _nbytes((CONV_SUB_ROWS, tc), F32)),
        name="conv_silu",
    )(proj, proj, proj, conv_w, conv_b.reshape(1, C), shift)


def _pool_kernel(prev_ref, cur_ref, next_ref, pw_ref, ps_ref, out_ref, *, tm, seq, Wd):
    i = pl.program_id(0)
    t0 = (i * tm) % seq
    has_prev = t0 > 0
    has_next = t0 + tm < seq
    r = lax.broadcasted_iota(jnp.int32, (tm, tm), 0)
    c = lax.broadcasted_iota(jnp.int32, (tm, tm), 1)
    rh = lax.broadcasted_iota(jnp.int32, (HALO, HALO), 0)
    ch = lax.broadcasted_iota(jnp.int32, (HALO, HALO), 1)
    t = lax.broadcasted_iota(jnp.int32, (tm, Wd), 0) + t0
    for g, w in enumerate(POOL_WINDOWS):
        lo_off = w // 2
        hi_off = w - lo_off
        sl = slice(g * Wd, (g + 1) * Wd)
        band = jnp.where((c >= r - lo_off) & (c < r + hi_off), 1.0, 0.0).astype(BF16)
        band_prev = jnp.where(has_prev & (ch - HALO >= rh - lo_off), 1.0, 0.0).astype(BF16)
        band_next = jnp.where(has_next & (ch < rh - HALO + hi_off), 1.0, 0.0).astype(BF16)
        cur = cur_ref[:, sl]
        wsum = _dot(band, cur)
        wsum = jnp.concatenate([wsum[0:HALO] + _dot(band_prev, prev_ref[:, sl]),
                                wsum[HALO:tm - HALO],
                                wsum[tm - HALO:] + _dot(band_next, next_ref[:, sl])], axis=0)
        cnt = jnp.minimum(t + hi_off, seq) - jnp.maximum(t - lo_off, 0)
        pooled = (wsum / cnt.astype(F32) - cur.astype(F32)).astype(BF16)
        out_ref[:, sl] = (_dot(pooled, pw_ref[g]) * ps_ref[:, sl]).astype(out_ref.dtype)


def _pool(proj, col_offset, pool_w, pool_scale, seq):
    M = proj.shape[0]
    PG, Wd, _ = pool_w.shape
    PD = PG * Wd
    assert PG == len(POOL_WINDOWS) and max(POOL_WINDOWS) // 2 <= HALO
    assert col_offset % PD == 0
    off = col_offset // PD
    tm = _pick(seq, (256, 128))
    hb = tm // HALO
    n_halo = M // HALO
    blk = (2 * _nbytes((tm, PD), BF16) + 2 * _nbytes((HALO, PD), BF16) + _nbytes((PG, Wd, Wd), BF16))
    return pl.pallas_call(
        functools.partial(_pool_kernel, tm=tm, seq=seq, Wd=Wd),
        grid=(M // tm,),
        in_specs=[pl.BlockSpec((HALO, PD), lambda i: (jnp.maximum(i * hb - 1, 0), off)),
                  pl.BlockSpec((tm, PD), lambda i: (i, off)),
                  pl.BlockSpec((HALO, PD), lambda i: (jnp.minimum((i + 1) * hb, n_halo - 1), off)),
                  pl.BlockSpec((PG, Wd, Wd), lambda i: (0, 0, 0)),
                  pl.BlockSpec((1, PD), lambda i: (0, 0))],
        out_specs=pl.BlockSpec((tm, PD), lambda i: (i, 0)),
        out_shape=jax.ShapeDtypeStruct((M, PD), BF16),
        compiler_params=_params(("parallel",), blk, 4 * _nbytes((tm, PD), F32)),
        name="pool",
    )(proj, proj, proj, pool_w, pool_scale.reshape(1, PD))


def _cumsum_rows(tri, v):
    a, b, c = _split3(v)
    return _dot(tri, a) + _dot(tri, b) + _dot(tri, c)


def _cumsum_lanes(v, triT):
    a, b, c = _split3(v)
    return _dot(a, triT) + _dot(b, triT) + _dot(c, triT)


def _hilo(v):
    a, b = _split2(v)
    return jnp.concatenate([a, b], axis=1)


def _expand(v_hilo, e2):
    return _dot(v_hilo, e2)


def _ssd_state_kernel(xf_ref, bf_ref, dtf_ref, xb_ref, bb_ref, dtb_ref, bias_ref, alog_ref, e_ref,
                      pf_ref, pb_ref, h_ref, *, H, G, N, Q):
    L = SSD_CHUNK

    @pl.when(pl.program_id(1) == 0)
    def _():
        h_ref[...] = jnp.zeros_like(h_ref)

    row = lax.broadcasted_iota(jnp.int32, (L, L), 0)
    col = lax.broadcasted_iota(jnp.int32, (L, L), 1)
    tri = jnp.where(row >= col, 1.0, 0.0).astype(BF16)
    is_fwd = lax.broadcasted_iota(jnp.int32, (L, 2 * H), 1) < H
    A = -jnp.exp(alog_ref[...])

    def direction(d, x_ref, b_ref, dtraw_ref, p_ref):
        dt = _softplus(dtraw_ref[...] + bias_ref[...])
        a = dt * A
        cum = _cumsum_rows(tri, a)
        total = cum[L - 1:L, :]
        dte = jnp.exp(jnp.where(is_fwd, total - cum, cum - a))
        factors = jnp.concatenate([dt * dte, jnp.broadcast_to(jnp.exp(total), (8, 2 * H))], axis=0)
        expanded = _expand(_hilo(factors), e_ref[d])
        xs = (x_ref[...].astype(F32) * expanded[0:L]).astype(BF16)
        cdexp = expanded[L:L + 1, :]
        bm = b_ref[...]
        for g in range(G):
            st = _dot_tn(bm[:, g * N:(g + 1) * N], xs[:, g * Q:(g + 1) * Q])
            h_old = h_ref[d, g]
            p_ref[0, 0, g] = h_old.astype(BF16)
            h_ref[d, g] = cdexp[:, g * Q:(g + 1) * Q] * h_old + st

    direction(0, xf_ref, bf_ref, dtf_ref, pf_ref)
    direction(1, xb_ref, bb_ref, dtb_ref, pb_ref)


def _ssd_states(xbc, dtraw, bias, alog, emat, batch, seq, D, H):
    L, G, N = SSD_CHUNK, SSD_GROUPS, SSD_STATE
    GN = G * N
    Q = D // G
    NC = seq // L
    assert D % GN == 0 or GN % D == 0
    b_off = D // GN
    H2 = 2 * H
    blk = 2 * (_nbytes((L, D), BF16) + _nbytes((L, GN), BF16) + _nbytes((L, H2), F32)
               + _nbytes((G, N, Q), BF16)) + _nbytes((2, 2 * H2, D), BF16)
    fwd = lambda b, t: b * NC + t
    bwd = lambda b, t: b * NC + (NC - 1 - t)
    out_sds = jax.ShapeDtypeStruct((batch, NC, G, N, Q), BF16)
    return pl.pallas_call(
        functools.partial(_ssd_state_kernel, H=H, G=G, N=N, Q=Q),
        grid=(batch, NC),
        in_specs=[pl.BlockSpec((L, D), lambda b, t: (fwd(b, t), 0)),
                  pl.BlockSpec((L, GN), lambda b, t: (fwd(b, t), b_off)),
                  pl.BlockSpec((L, H2), lambda b, t: (fwd(b, t), 0)),
                  pl.BlockSpec((L, D), lambda b, t: (bwd(b, t), 0)),
                  pl.BlockSpec((L, GN), lambda b, t: (bwd(b, t), b_off)),
                  pl.BlockSpec((L, H2), lambda b, t: (bwd(b, t), 0)),
                  pl.BlockSpec((1, H2), lambda b, t: (0, 0)),
                  pl.BlockSpec((1, H2), lambda b, t: (0, 0)),
                  pl.BlockSpec((2, 2 * H2, D), lambda b, t: (0, 0, 0))],
        out_specs=[pl.BlockSpec((1, 1, G, N, Q), lambda b, t: (b, t, 0, 0, 0)),
                   pl.BlockSpec((1, 1, G, N, Q), lambda b, t: (b, NC - 1 - t, 0, 0, 0))],
        out_shape=[out_sds, out_sds],
        scratch_shapes=[pltpu.VMEM((2, G, N, Q), F32)],
        compiler_params=_params(("parallel", "arbitrary"), blk,
                                _nbytes((2, G, N, Q), F32) + 4 * _nbytes((L, D), F32)),
        name="ssd_states",
    )(xbc, xbc, dtraw, xbc, xbc, dtraw, bias, alog, emat)


def _ssd_out_kernel(dsk_ref, x_ref, b_ref, c_ref, zs_ref, dt_ref, dtT_ref, pf_ref, pb_ref,
                    bias_ref, alog_ref, biasT_ref, alogT_ref, ng_ref, e_ref, out_ref,
                    *, H, G, N, Q, P):
    L = SSD_CHUNK
    E = H // G
    row = lax.broadcasted_iota(jnp.int32, (L, L), 0)
    col = lax.broadcasted_iota(jnp.int32, (L, L), 1)
    lower = row >= col
    eye = row == col
    tri = jnp.where(lower, 1.0, 0.0).astype(BF16)
    triT = jnp.where(row <= col, 1.0, 0.0).astype(BF16)
    is_fwd = lax.broadcasted_iota(jnp.int32, (L, 2 * H), 1) < H
    is_fwdT = lax.broadcasted_iota(jnp.int32, (2 * H, L), 0) < H

    dt = _softplus(dt_ref[...] + bias_ref[...])
    a = dt * (-jnp.exp(alog_ref[...]))
    cum = _cumsum_rows(tri, a)
    total = cum[L - 1:L, :]
    excl = cum - a
    col2 = jnp.where(is_fwd, cum, excl) * LOG2E
    sdec = _hilo(jnp.exp(jnp.where(is_fwd, cum, total - excl)))

    dtT = _softplus(dtT_ref[...] + biasT_ref[...])
    aT = dtT * (-jnp.exp(alogT_ref[...]))
    cumT = _cumsum_lanes(aT, triT)
    lg = jnp.log2(dtT)
    row2 = jnp.where(is_fwdT, cumT * LOG2E - lg, (cumT - aT) * LOG2E + lg)

    lane = lax.broadcasted_iota(jnp.int32, (L, 2 * P), 1)
    keep_left = jnp.where(lane < P, 1.0, 0.0).astype(BF16)
    keep_right = jnp.where(lane < P, 0.0, 1.0).astype(BF16)

    for g in range(G):
        c_g = c_ref[:, g * N:(g + 1) * N]
        cb = _dot_nt(c_g, b_ref[:, g * N:(g + 1) * N])
        cb_diag = jnp.sum(jnp.where(eye, cb, 0.0), axis=0, keepdims=True)
        pieces = []
        for e2 in range(E // 2):
            ws = []
            for h in (g * E + 2 * e2, g * E + 2 * e2 + 1):
                arg = jnp.where(lower,
                                col2[:, h:h + 1] - row2[h:h + 1, :],
                                row2[H + h:H + h + 1, :] - col2[:, H + h:H + h + 1])
                diag = cb_diag * dtT[H + h:H + h + 1, :] + dsk_ref[h]
                ws.append((cb * jnp.exp2(arg) + jnp.where(eye, diag, 0.0)).astype(BF16))
            h0 = g * E + 2 * e2
            xp = x_ref[:, h0 * P:(h0 + 2) * P]
            rhs = jnp.concatenate([xp * keep_left, xp * keep_right], axis=0)
            pieces.append(_dot(jnp.concatenate(ws, axis=1), rhs))
        y = jnp.concatenate(pieces, axis=1) if len(pieces) > 1 else pieces[0]
        sl = slice(g * Q, (g + 1) * Q)
        y = (y + _dot(c_g, pf_ref[0, 0, g]) * _expand(sdec, e_ref[0, :, sl])
             + _dot(c_g, pb_ref[0, 0, g]) * _expand(sdec, e_ref[1, :, sl]))
        y = y * zs_ref[:, sl].astype(F32)
        ms = jnp.mean(y * y, axis=-1, keepdims=True)
        out_ref[:, sl] = (y * lax.rsqrt(ms + EPS) * ng_ref[:, sl]).astype(out_ref.dtype)


def _ssd_out(zs, xbc, dtraw, dtrawT, prev_f, prev_b, bias, alog, biasT, alogT, d_skip, norm_g, emat,
             batch, seq, D, H):
    L, G, N = SSD_CHUNK, SSD_GROUPS, SSD_STATE
    GN = G * N
    Q = D // G
    P = D // H
    NC = seq // L
    M = batch * seq
    H2 = 2 * H
    assert (H // G) % 2 == 0 and 2 * P == 128 and D % GN == 0
    b_off = D // GN
    blk = (2 * _nbytes((L, D), BF16) + 2 * _nbytes((L, GN), BF16) + 2 * _nbytes((L, H2), F32)
           + 2 * _nbytes((G, N, Q), BF16) + _nbytes((2, 2 * H2, D), BF16) + _nbytes((L, D), BF16))
    rowblk = lambda b, c: b * NC + c
    return pl.pallas_call(
        functools.partial(_ssd_out_kernel, H=H, G=G, N=N, Q=Q, P=P),
        grid=(batch, NC),
        in_specs=[pl.BlockSpec(memory_space=pltpu.SMEM),
                  pl.BlockSpec((L, D), lambda b, c: (rowblk(b, c), 0)),
                  pl.BlockSpec((L, GN), lambda b, c: (rowblk(b, c), b_off)),
                  pl.BlockSpec((L, GN), lambda b, c: (rowblk(b, c), b_off + 1)),
                  pl.BlockSpec((L, D), lambda b, c: (rowblk(b, c), 0)),
                  pl.BlockSpec((L, H2), lambda b, c: (rowblk(b, c), 0)),
                  pl.BlockSpec((H2, L), lambda b, c: (0, rowblk(b, c))),
                  pl.BlockSpec((1, 1, G, N, Q), lambda b, c: (b, c, 0, 0, 0)),
                  pl.BlockSpec((1, 1, G, N, Q), lambda b, c: (b, c, 0, 0, 0)),
                  pl.BlockSpec((1, H2), lambda b, c: (0, 0)),
                  pl.BlockSpec((1, H2), lambda b, c: (0, 0)),
                  pl.BlockSpec((H2, L), lambda b, c: (0, 0)),
                  pl.BlockSpec((H2, L), lambda b, c: (0, 0)),
                  pl.BlockSpec((1, D), lambda b, c: (0, 0)),
                  pl.BlockSpec((2, 2 * H2, D), lambda b, c: (0, 0, 0))],
        out_specs=pl.BlockSpec((L, D), lambda b, c: (rowblk(b, c), 0)),
        out_shape=jax.ShapeDtypeStruct((M, D), BF16),
        compiler_params=_params(("parallel", "parallel"), blk, 8 * _nbytes((L, D), F32)),
        name="ssd_out",
    )(d_skip, xbc, xbc, xbc, zs, dtraw, dtrawT, prev_f, prev_b, bias, alog, biasT, alogT,
      norm_g.reshape(1, D), emat)


def _prepare_layer(norm_mix_g, w_in, conv_w, conv_b, dt_bias_f, dt_bias_b, a_log_f, a_log_b, d_skip,
                   ssd_norm_g, w_ssd_up, pool_w, pool_scale, w_pool_up, w_out, norm_mlp_g, w_ff1, w_ff2,
                   norm_ple_g, w_ple_gate, w_ple_proj):
    D = w_in.shape[0]
    H = d_skip.shape[0]
    C = conv_w.shape[1]
    s_xbc = D + C
    s_dt = s_xbc + 2 * H
    w_dt = w_in[:, s_xbc:s_dt].astype(BF16)
    head_of_lane = jnp.arange(D, dtype=jnp.int32) // (D // H)
    onehot = (jnp.arange(H, dtype=jnp.int32)[:, None] == head_of_lane[None, :]).astype(BF16)
    zeros = jnp.zeros_like(onehot)
    L = SSD_CHUNK
    bias = jnp.concatenate([dt_bias_f, dt_bias_b])
    alog = jnp.concatenate([a_log_f, a_log_b])
    return dict(
        D=D, H=H, C=C,
        norm_mix_g=norm_mix_g,
        w_in=w_in.astype(BF16), s_dt=s_dt,
        w_dt=w_dt, w_dtT=w_dt.T,
        conv_w=conv_w, conv_b=conv_b,
        bias=bias.reshape(1, 2 * H), alog=alog.reshape(1, 2 * H),
        biasT=jnp.broadcast_to(bias[:, None], (2 * H, L)), alogT=jnp.broadcast_to(alog[:, None], (2 * H, L)),
        emat=jnp.stack([jnp.concatenate([onehot, zeros, onehot, zeros]),
                        jnp.concatenate([zeros, onehot, zeros, onehot])]),
        d_skip=d_skip, ssd_norm_g=ssd_norm_g,
        w_ssd_up=w_ssd_up, w_pool_up=w_pool_up, w_out=w_out, w_ff1=w_ff1, w_ff2=w_ff2, w_ple_gate=w_ple_gate,
        pool_w=pool_w.astype(BF16), pool_scale=pool_scale,
        norm_mlp_g=norm_mlp_g, norm_ple_g=norm_ple_g, w_ple_proj=w_ple_proj.astype(BF16),
    )


def _layer(x_parts, p_parts, lp, batch, seq):
    D, H = lp["D"], lp["H"]
    PD = lp["w_pool_up"].shape[0]
    off_ga, off_gb = PD, PD + D

    h, dtraw, dtrawT = _norm_dt(x_parts, lp["norm_mix_g"], lp["w_dt"], lp["w_dtT"])
    zs, (w_ssd_up, w_pool_up) = _matmul(
        h, lp["w_in"], BF16, "in_proj_z", epilogue=lambda acc: acc * _sigmoid(acc), tn=_TN_WIDE,
        w_cols=(0, D), casts=(lp["w_ssd_up"], lp["w_pool_up"]))
    tail, (w_ff1, w_out) = _matmul(
        h, lp["w_in"], BF16, "in_proj_tail", tn=_TN_WIDE, w_cols=(lp["s_dt"], PD + 2 * D),
        casts=(lp["w_ff1"], lp["w_out"]))

    xbc_raw, (w_ple_gate,) = _matmul(h, lp["w_in"], BF16, "in_proj_xbc", tn=_TN_WIDE, w_cols=(D, lp["C"]),
                                     casts=(lp["w_ple_gate"],))
    xbc = _conv_silu(xbc_raw, 0, lp["conv_w"], lp["conv_b"], seq)
    prev_f, prev_b = _ssd_states(xbc, dtraw, lp["bias"], lp["alog"], lp["emat"], batch, seq, D, H)
    y = _ssd_out(zs, xbc, dtraw, dtrawT, prev_f, prev_b, lp["bias"], lp["alog"], lp["biasT"],
                 lp["alogT"], lp["d_skip"], lp["ssd_norm_g"], lp["emat"], batch, seq, D, H)
    gated_a = _matmul(y, w_ssd_up, BF16, "ssd_up", tn=_TN_WIDE,
                      epilogue=lambda acc, g_ref: _sigmoid(g_ref[...].astype(F32)) * acc,
                      extras=(tail,), col_offsets=(off_ga,),
                      extra_specs=lambda tm, tn: [_tile_spec(off_ga)(tm, tn)])

    pooled = _pool(tail, 0, lp["pool_w"], lp["pool_scale"], seq)
    mixed = _matmul(pooled, w_pool_up, BF16, "pool_up", tn=_TN_WIDE,
                    epilogue=lambda acc, g_ref, a_ref: (_sigmoid(g_ref[...].astype(F32)) * acc
                                                        + a_ref[...].astype(F32)),
                    extras=(tail, gated_a), col_offsets=(off_gb,),
                    extra_specs=lambda tm, tn: [_tile_spec(off_gb)(tm, tn), _tile_spec(0)(tm, tn)])
    x, xg, ssq = _matmul_residual_norm(mixed, w_out, x_parts, lp["norm_mlp_g"], "out_proj")

    ff, (w_ff2,) = _matmul(xg, w_ff1, BF16, "ff1", ssq=ssq, tn=_TN_WIDE, casts=(lp["w_ff2"],),
                           epilogue=lambda acc: jnp.square(jnp.maximum(acc, 0.0)))
    x, xg, ssq = _matmul_kacc_residual_norm(ff, w_ff2, x, lp["norm_ple_g"], "ff2")

    ple = p_parts[0].shape[1]
    tm = _pick(functools.reduce(math.gcd, [p.shape[0] for p in p_parts]), (1024, 512, 256, 128))
    tiles = _part_tiles(p_parts, tm)

    def ple_epilogue(acc, x_ref, *refs):
        p = _part_read(refs[:-1], tiles, pl.program_id(0))
        return x_ref[...] + _sigmoid(acc) * _dot(p.astype(BF16), refs[-1][...])

    return _matmul(xg, w_ple_gate, F32, "ple", ssq=ssq, tm=tm, epilogue=ple_epilogue,
                   extras=(x, *p_parts, lp["w_ple_proj"]),
                   extra_specs=lambda tm, tn: ([_tile_spec(0)(tm, tn)] + _part_specs(p_parts, tm, ple)
                                               + [pl.BlockSpec((ple, tn), lambda i, j: (0, j))]))


def _run_groups(xs, ps, layers, norm_final_g):
    seq, D = xs[0].shape[1:]
    assert seq % SSD_CHUNK == 0
    batch = sum(x.shape[0] for x in xs)
    x_parts = [x.reshape(-1, D) for x in xs]
    for i, lp in enumerate(layers):
        p_parts = [p[i].reshape(-1, p.shape[-1]) for p in ps]
        x_parts = [_layer(x_parts, p_parts, lp, batch, seq)]
    outs = _norm_final(x_parts[0] if len(x_parts) == 1 else jnp.concatenate(x_parts), norm_final_g,
                       [x.shape[0] * seq for x in xs])
    return [o.reshape(x.shape) for o, x in zip(outs, xs)]


def kernel(x_prompt, x_sample, p_prompt, p_sample, norm_mix_g, w_in, conv_w, conv_b, dt_bias_f, dt_bias_b, a_log_f, a_log_b, d_skip, ssd_norm_g, w_ssd_up, pool_w, pool_scale, w_pool_up, w_out, norm_mlp_g, w_ff1, w_ff2, norm_ple_g, w_ple_gate, w_ple_proj, norm_final_g):
    stacked = (norm_mix_g, w_in, conv_w, conv_b, dt_bias_f, dt_bias_b, a_log_f, a_log_b, d_skip,
               ssd_norm_g, w_ssd_up, pool_w, pool_scale, w_pool_up, w_out, norm_mlp_g, w_ff1, w_ff2,
               norm_ple_g, w_ple_gate, w_ple_proj)
    layers = [_prepare_layer(*[w[i] for w in stacked]) for i in range(w_in.shape[0])]
    if x_prompt.shape[1] == x_sample.shape[1]:
        y_prompt, y_sample = _run_groups([x_prompt, x_sample], [p_prompt, p_sample], layers, norm_final_g)
    else:
        (y_prompt,) = _run_groups([x_prompt], [p_prompt], layers, norm_final_g)
        (y_sample,) = _run_groups([x_sample], [p_sample], layers, norm_final_g)
    return (y_prompt, y_sample)
```

```python
import functools
import math

import jax
import jax.numpy as jnp
from jax import lax
from jax.experimental import pallas as pl
from jax.experimental.pallas import tpu as pltpu

F32 = jnp.float32
BF16 = jnp.bfloat16

EPS = 1e-6
LOG2E = 1.4426950408889634
LN2 = 0.6931471805599453
SSD_GROUPS = 8
SSD_STATE = 128
SSD_CHUNK = 128
POOL_WINDOWS = (2, 4, 8, 16)

V7X_VMEM_BYTES = 64 * 1024 * 1024
V7X_VMEM_RESERVE_BYTES = 8 * 1024 * 1024
BF16_SUBLANE_TILE = 16
LANES = 128
HALO = BF16_SUBLANE_TILE
_TN_WIDE = 1024
_TN_F32_WEIGHT = 512
CONV_SUB_ROWS = 128
CAST_BLOCK_BYTES = 2 * 1024 * 1024


def _pick(dim, prefs):
    for p in prefs:
        if dim % p == 0:
            return p
    return dim


def _nbytes(shape, dtype):
    n = 1
    for s in shape:
        n *= s
    return n * jnp.dtype(dtype).itemsize


def _params(semantics, block_bytes, scratch_bytes=0):
    est = 2 * block_bytes + scratch_bytes
    limit = min(2 * est + (4 << 20), V7X_VMEM_BYTES - V7X_VMEM_RESERVE_BYTES)
    return pltpu.CompilerParams(dimension_semantics=semantics, vmem_limit_bytes=int(limit))


def _sigmoid(v):
    return 1.0 / (1.0 + jnp.exp(-v))


def _softplus(v):
    u = jnp.exp(-jnp.abs(v))
    w = 1.0 + u
    return jnp.maximum(v, 0.0) + jnp.where(w == 1.0, u, (jnp.log2(w) * LN2) * (u / (w - 1.0)))


def _split3(v):
    a = v.astype(BF16)
    r = v - a.astype(F32)
    b = r.astype(BF16)
    c = (r - b.astype(F32)).astype(BF16)
    return a, b, c


def _split2(v):
    a = v.astype(BF16)
    b = (v - a.astype(F32)).astype(BF16)
    return a, b


def _dot(a, b):
    return jnp.dot(a, b, preferred_element_type=F32)


def _dot_nt(a, b):
    return lax.dot_general(a, b, (((1,), (1,)), ((), ())), preferred_element_type=F32)


def _dot_tn(a, b):
    return lax.dot_general(a, b, (((0,), (0,)), ((), ())), preferred_element_type=F32)


def _rms(x, g):
    ms = jnp.mean(x * x, axis=-1, keepdims=True)
    return x * lax.rsqrt(ms + EPS) * g


def _part_tiles(parts, tm):
    return [p.shape[0] // tm for p in parts]


def _part_specs(parts, tm, ncols, col_index=None):
    specs, start = [], 0
    for n in _part_tiles(parts, tm):
        def index(*idx, s=start, n=n):
            row = idx[0] - s
            if col_index is None:
                return jnp.clip(row, 0, n - 1), 0
            return jnp.clip(row, 0, n - 1), jnp.where((row >= 0) & (row < n), col_index(*idx), 0)
        specs.append(pl.BlockSpec((tm, ncols), index))
        start += n
    return specs


def _part_read(refs, tiles, i):
    val = refs[-1][...]
    end = sum(tiles[:-1])
    for ref, n in zip(reversed(refs[:-1]), reversed(tiles[:-1])):
        val = jnp.where(i < end, ref[...], val)
        end -= n
    return val


def _norm_dt_kernel(*refs, tiles):
    n = len(tiles)
    x_refs, (g_ref, wdt_ref, wdtT_ref, h_ref, dt_ref, dtT_ref) = refs[:n], refs[n:]
    h = _rms(_part_read(x_refs, tiles, pl.program_id(0)), g_ref[...]).astype(BF16)
    h_ref[...] = h
    dt_ref[...] = _dot(h, wdt_ref[...])
    dtT_ref[...] = _dot_nt(wdtT_ref[...], h)


def _norm_dt(x_parts, g, w_dt, w_dtT):
    D = x_parts[0].shape[1]
    M = sum(p.shape[0] for p in x_parts)
    H2 = w_dt.shape[1]
    tm = _pick(functools.reduce(math.gcd, [p.shape[0] for p in x_parts]), (512, 256, 128))
    blk = (len(x_parts) * _nbytes((tm, D), F32) + _nbytes((tm, D), BF16) + 2 * _nbytes((D, H2), BF16)
           + 2 * _nbytes((tm, H2), F32))
    return pl.pallas_call(
        functools.partial(_norm_dt_kernel, tiles=_part_tiles(x_parts, tm)),
        grid=(M // tm,),
        in_specs=_part_specs(x_parts, tm, D) + [
            pl.BlockSpec((1, D), lambda i: (0, 0)),
            pl.BlockSpec((D, H2), lambda i: (0, 0)),
            pl.BlockSpec((H2, D), lambda i: (0, 0))],
        out_specs=[pl.BlockSpec((tm, D), lambda i: (i, 0)),
                   pl.BlockSpec((tm, H2), lambda i: (i, 0)),
                   pl.BlockSpec((H2, tm), lambda i: (0, i))],
        out_shape=[jax.ShapeDtypeStruct((M, D), BF16),
                   jax.ShapeDtypeStruct((M, H2), F32),
                   jax.ShapeDtypeStruct((H2, M), F32)],
        compiler_params=_params(("arbitrary",), blk),
        name="norm_dt",
    )(*x_parts, g.reshape(1, D), w_dt, w_dtT)


def _norm_final_kernel(x_ref, g_ref, *o_refs, tiles):
    i = pl.program_id(0)
    y = _rms(x_ref[...], g_ref[...])
    start = 0
    for o_ref, n in zip(o_refs, tiles):
        @pl.when((i >= start) & (i < start + n))
        def _(o_ref=o_ref):
            o_ref[...] = y
        start += n


def _norm_final(x, g, part_rows):
    M, D = x.shape
    tm = _pick(functools.reduce(math.gcd, part_rows), (512, 256, 128))
    outs = [jax.ShapeDtypeStruct((r, D), F32) for r in part_rows]
    blk = (1 + len(part_rows)) * _nbytes((tm, D), F32)
    return pl.pallas_call(
        functools.partial(_norm_final_kernel, tiles=_part_tiles(outs, tm)),
        grid=(M // tm,),
        in_specs=[pl.BlockSpec((tm, D), lambda i: (i, 0)),
                  pl.BlockSpec((1, D), lambda i: (0, 0))],
        out_specs=_part_specs(outs, tm, D),
        out_shape=outs,
        compiler_params=_params(("arbitrary",), blk),
        name="norm_final",
    )(x, g.reshape(1, D))


def _row_rsqrt(ssq_ref, d):
    return lax.rsqrt(jnp.sum(ssq_ref[...], axis=-1, keepdims=True) * (1.0 / d) + EPS)


def _lane_partial_sumsq(v):
    sq = v * v
    part = sq[:, 0:LANES]
    for q in range(1, v.shape[1] // LANES):
        part = part + sq[:, q * LANES:(q + 1) * LANES]
    return part


def _mm_kernel(lhs_ref, w_ref, *rest, epilogue, norm_width, n_casts):
    n_in = len(rest) - 1 - 2 * n_casts
    ins, cast_srcs, out_ref, cast_dsts = rest[:n_in], rest[n_in:n_in + n_casts], rest[n_in + n_casts], rest[n_in + n_casts + 1:]
    acc = _dot(lhs_ref[...], w_ref[...].astype(lhs_ref.dtype))
    if norm_width:
        acc = acc * _row_rsqrt(ins[0], norm_width)
        ins = ins[1:]
    out_ref[...] = epilogue(acc, *ins).astype(out_ref.dtype)
    for src_ref, dst_ref in zip(cast_srcs, cast_dsts):
        dst_ref[...] = src_ref[...].astype(dst_ref.dtype)


def _plan_casts(items, steps):
    jobs, rest, used = [], [], 0
    for item in items:
        a, (c0, cw) = item if isinstance(item, tuple) else (item, (0, item.shape[1]))
        cols = _pick(math.gcd(cw, c0) if c0 else cw, (2048, 1024, 512, 256, 128))
        fit = [r for r in (128, 256, 512, 1024, 2048, 4096) if a.shape[0] % r == 0 and cw % cols == 0
               and c0 % cols == 0 and cols % LANES == 0
               and _nbytes((r, cols), F32) <= CAST_BLOCK_BYTES
               and used + (a.shape[0] // r) * (cw // cols) <= steps]
        if fit and not rest:
            jobs.append((a, c0, cw, (fit[0], cols)))
            used += (a.shape[0] // fit[0]) * (cw // cols)
        else:
            rest.append(a[:, c0:c0 + cw])
    return tuple(jobs), rest


def _cast_specs(jobs, n_i, n_j):
    src_specs, dst_specs, start = [], [], 0
    for a, c0, cw, block in jobs:
        n_rb, n_cb = a.shape[0] // block[0], cw // block[1]

        def index(i, j, start=start, n=n_rb * n_cb, n_cb=n_cb, cb0=0):
            b = jnp.clip(i * n_j + j - start, 0, n - 1)
            return b // n_cb, cb0 + b % n_cb
        src_specs.append(pl.BlockSpec(block, functools.partial(index, cb0=c0 // block[1])))
        dst_specs.append(pl.BlockSpec(block, index))
        start += n_rb * n_cb
    assert start <= n_i * n_j, "not enough grid steps to carry the weight casts"
    return src_specs, dst_specs


def _matmul(lhs, w, out_dtype, name, epilogue=None, extras=(), extra_specs=None, ssq=None, tm=None, tn=None,
            col_offsets=(), w_cols=None, casts=()):
    M, K = lhs.shape
    w_off, N = w_cols if w_cols else (0, w.shape[1])
    tm = tm or _pick(M, (1024, 512, 256, 128))
    tn = next(t for t in ((tn,) if tn else ()) + (512, 256, 128)
              if N % t == 0 and all(o % t == 0 for o in col_offsets))
    if epilogue is None:
        epilogue = lambda acc: acc
    specs = extra_specs(tm, tn) if extra_specs else []
    if ssq is not None:
        extras = (ssq,) + tuple(extras)
        specs = [pl.BlockSpec((tm, LANES), lambda i, j: (i, 0))] + specs
    cast_items = tuple(casts)
    jobs, uncast = _plan_casts(cast_items, (M // tm) * (N // tn))
    cast_src_specs, cast_dst_specs = _cast_specs(jobs, M // tm, N // tn)
    blk = (_nbytes((tm, K), lhs.dtype) + _nbytes((K, tn), w.dtype) + _nbytes((tm, tn), out_dtype)
           + sum(_nbytes(s.block_shape, e.dtype) for s, e in zip(specs, extras))
           + sum(_nbytes(job[3], F32) + _nbytes(job[3], BF16) for job in jobs)
           + (_nbytes((K, tn), BF16) if w.dtype != lhs.dtype else 0))
    if w_off % tn == 0:
        w_blk0 = w_off // tn
        w_spec = pl.BlockSpec((K, tn), lambda i, j: (0, w_blk0 + j))
    elif w_off % LANES == 0:
        w_spec = pl.BlockSpec((pl.Element(K), pl.Element(tn)),
                              lambda i, j: (0, pl.multiple_of(w_off + j * tn, LANES)))
    else:
        w = w[:, w_off:w_off + N]
        w_spec = pl.BlockSpec((K, tn), lambda i, j: (0, j))
    outs = pl.pallas_call(
        functools.partial(_mm_kernel, epilogue=epilogue, norm_width=K if ssq is not None else 0,
                          n_casts=len(jobs)),
        grid=(M // tm, N // tn),
        in_specs=[pl.BlockSpec((tm, K), lambda i, j: (i, 0)), w_spec] + specs + cast_src_specs,
        out_specs=[pl.BlockSpec((tm, tn), lambda i, j: (i, j))] + cast_dst_specs,
        out_shape=[jax.ShapeDtypeStruct((M, N), out_dtype)]
        + [jax.ShapeDtypeStruct((job[0].shape[0], job[2]), BF16) for job in jobs],
        compiler_params=_params(("arbitrary", "arbitrary") if jobs else ("parallel", "arbitrary"),
                                blk, _nbytes((tm, tn), F32)),
        name=name,
    )(lhs, w, *extras, *[job[0] for job in jobs])
    if not cast_items:
        return outs[0]
    return outs[0], tuple(outs[1:]) + tuple(a.astype(BF16) for a in uncast)


def _tile_spec(col_offset):
    def make(tm, tn):
        assert col_offset % tn == 0
        off = col_offset // tn
        return pl.BlockSpec((tm, tn), lambda i, j: (i, off + j))
    return make


def _emit_norm_operand(x_new, g_ref, xg_ref, ssq_ref, first):
    xg_ref[...] = (x_new * g_ref[...]).astype(xg_ref.dtype)
    part = _lane_partial_sumsq(x_new)

    @pl.when(first)
    def _():
        ssq_ref[...] = part

    @pl.when(jnp.logical_not(first))
    def _():
        ssq_ref[...] += part


def _mm_res_norm_kernel(lhs_ref, w_ref, *refs, tiles):
    n = len(tiles)
    res_refs, (g_ref, x_ref, xg_ref, ssq_ref) = refs[:n], refs[n:]
    x_new = _part_read(res_refs, tiles, pl.program_id(0)) + _dot(lhs_ref[...], w_ref[...])
    x_ref[...] = x_new
    _emit_norm_operand(x_new, g_ref, xg_ref, ssq_ref, pl.program_id(1) == 0)


def _matmul_residual_norm(lhs, w, res_parts, g_next, name):
    M, K = lhs.shape
    N = w.shape[1]
    tm = _pick(functools.reduce(math.gcd, [p.shape[0] for p in res_parts]), (1024, 512, 256, 128))
    tn = _pick(N, (512, 256, 128))
    blk = (_nbytes((tm, K), lhs.dtype) + _nbytes((K, tn), w.dtype) + (1 + len(res_parts)) * _nbytes((tm, tn), F32)
           + _nbytes((tm, tn), BF16) + _nbytes((tm, LANES), F32))
    return pl.pallas_call(
        functools.partial(_mm_res_norm_kernel, tiles=_part_tiles(res_parts, tm)),
        grid=(M // tm, N // tn),
        in_specs=[pl.BlockSpec((tm, K), lambda i, j: (i, 0)),
                  pl.BlockSpec((K, tn), lambda i, j: (0, j))]
        + _part_specs(res_parts, tm, tn, col_index=lambda i, j: j)
        + [pl.BlockSpec((1, tn), lambda i, j: (0, j))],
        out_specs=[pl.BlockSpec((tm, tn), lambda i, j: (i, j)),
                   pl.BlockSpec((tm, tn), lambda i, j: (i, j)),
                   pl.BlockSpec((tm, LANES), lambda i, j: (i, 0))],
        out_shape=[jax.ShapeDtypeStruct((M, N), F32),
                   jax.ShapeDtypeStruct((M, N), BF16),
                   jax.ShapeDtypeStruct((M, LANES), F32)],
        compiler_params=_params(("arbitrary", "arbitrary"), blk, _nbytes((tm, tn), F32)),
        name=name,
    )(lhs, w, *res_parts, g_next.reshape(1, N))


def _mm_kacc_kernel(lhs_ref, w_ref, res_ref, g_ref, x_ref, xg_ref, ssq_ref, *, nk):
    k = pl.program_id(2)

    def step(base_ref, finish):
        x_new = base_ref[...] + _dot(lhs_ref[...], w_ref[...])
        x_ref[...] = x_new
        if finish:
            _emit_norm_operand(x_new, g_ref, xg_ref, ssq_ref, pl.program_id(1) == 0)

    if nk == 1:
        step(res_ref, True)
        return
    pl.when(k == 0)(lambda: step(res_ref, False))
    if nk > 2:
        pl.when((k > 0) & (k < nk - 1))(lambda: step(x_ref, False))
    pl.when(k == nk - 1)(lambda: step(x_ref, True))


def _matmul_kacc_residual_norm(lhs, w, res, g_next, name):
    M, K = lhs.shape
    N = w.shape[1]
    tm = _pick(M, (1024, 512, 256, 128))
    tn = _pick(N, (1024, 512, 256, 128))
    tk = _pick(K, (2048, 1024, 512, 256, 128))
    blk = (_nbytes((tm, tk), lhs.dtype) + _nbytes((tk, tn), w.dtype) + 2 * _nbytes((tm, tn), F32)
           + _nbytes((tm, tn), BF16) + _nbytes((tm, LANES), F32))
    return pl.pallas_call(
        functools.partial(_mm_kacc_kernel, nk=K // tk),
        grid=(M // tm, N // tn, K // tk),
        in_specs=[pl.BlockSpec((tm, tk), lambda i, j, k: (i, k)),
                  pl.BlockSpec((tk, tn), lambda i, j, k: (k, j)),
                  pl.BlockSpec((tm, tn), lambda i, j, k: (i, j)),
                  pl.BlockSpec((1, tn), lambda i, j, k: (0, j))],
        out_specs=[pl.BlockSpec((tm, tn), lambda i, j, k: (i, j)),
                   pl.BlockSpec((tm, tn), lambda i, j, k: (i, j)),
                   pl.BlockSpec((tm, LANES), lambda i, j, k: (i, 0))],
        out_shape=[jax.ShapeDtypeStruct((M, N), F32),
                   jax.ShapeDtypeStruct((M, N), BF16),
                   jax.ShapeDtypeStruct((M, LANES), F32)],
        compiler_params=_params(("parallel", "arbitrary", "arbitrary"), blk, _nbytes((tm, tn), F32)),
        name=name,
    )(lhs, w, res, g_next.reshape(1, N))


def _conv_kernel(prev_ref, cur_ref, next_ref, w_ref, b_ref, shift_ref, out_ref, buf_ref, *, tm, seq_tiles, width):
    i = pl.program_id(0)
    first = (i % seq_tiles) == 0
    last = (i % seq_tiles) == seq_tiles - 1
    prev = prev_ref[...]
    nxt = next_ref[...]
    buf_ref[0:HALO, :] = jnp.where(first, jnp.zeros_like(prev), prev)
    buf_ref[HALO:HALO + tm, :] = cur_ref[...]
    buf_ref[HALO + tm:, :] = jnp.where(last, jnp.zeros_like(nxt), nxt)

    half = width // 2
    taps = [k for k in range(width) if k != half]
    sub = shift_ref.shape[0]
    win = sub + 2 * HALO
    shift = shift_ref[...]
    w_taps = [w_ref[k:k + 1, :].astype(BF16) for k in taps]
    for r0 in range(0, tm, sub):
        window = buf_ref[r0:r0 + win, :]
        scaled = jnp.concatenate([window * wk for wk in w_taps], axis=0)
        acc = (b_ref[...] + w_ref[half:half + 1, :] * window[HALO:HALO + sub].astype(F32)
               + _dot(shift, scaled))
        out_ref[r0:r0 + sub, :] = (acc * _sigmoid(acc)).astype(out_ref.dtype)


def _conv_silu(proj, col_offset, conv_w, conv_b, seq):
    M = proj.shape[0]
    width, C = conv_w.shape
    assert width // 2 <= HALO
    tm = _pick(seq, (512, 256, 128))
    tc = _pick(C, (2048, 1024, 512, 256, 128))
    while col_offset % tc:
        tc //= 2
    off = col_offset // tc
    hb = tm // HALO
    n_halo = M // HALO
    half = width // 2
    sub = min(tm, CONV_SUB_ROWS)
    win = sub + 2 * HALO
    rows = jnp.arange(sub, dtype=jnp.int32)[:, None]
    cols = jnp.arange(win, dtype=jnp.int32)[None, :]
    shift = jnp.concatenate([(cols == rows + HALO + (k - half)) for k in range(width) if k != half],
                            axis=1).astype(BF16)
    blk = (2 * _nbytes((tm, tc), BF16) + 2 * _nbytes((HALO, tc), BF16) + _nbytes((width + 1, tc), F32)
           + _nbytes(shift.shape, BF16))
    return pl.pallas_call(
        functools.partial(_conv_kernel, tm=tm, seq_tiles=seq // tm, width=width),
        grid=(M // tm, C // tc),
        in_specs=[pl.BlockSpec((HALO, tc), lambda i, j: (jnp.maximum(i * hb - 1, 0), off + j)),
                  pl.BlockSpec((tm, tc), lambda i, j: (i, off + j)),
                  pl.BlockSpec((HALO, tc), lambda i, j: (jnp.minimum((i + 1) * hb, n_halo - 1), off + j)),
                  pl.BlockSpec((width, tc), lambda i, j: (0, j)),
                  pl.BlockSpec((1, tc), lambda i, j: (0, j)),
                  pl.BlockSpec(shift.shape, lambda i, j: (0, 0))],
        out_specs=pl.BlockSpec((tm, tc), lambda i, j: (i, j)),
        out_shape=jax.ShapeDtypeStruct((M, C), BF16),
        scratch_shapes=[pltpu.VMEM((tm + 2 * HALO, tc), BF16)],
        compiler_params=_params(("parallel", "parallel"), blk,
                                _nbytes((tm + 2 * HALO, tc), BF16) + 8 * _nbytes((CONV_SUB_ROWS, tc), F32)),
        name="conv_silu",
    )(proj, proj, proj, conv_w, conv_b.reshape(1, C), shift)


def _pool_kernel(prev_ref, cur_ref, next_ref, pw_ref, ps_ref, out_ref, *, tm, seq, Wd):
    i = pl.program_id(0)
    t0 = (i * tm) % seq
    has_prev = t0 > 0
    has_next = t0 + tm < seq
    r = lax.broadcasted_iota(jnp.int32, (tm, tm), 0)
    c = lax.broadcasted_iota(jnp.int32, (tm, tm), 1)
    rh = lax.broadcasted_iota(jnp.int32, (HALO, HALO), 0)
    ch = lax.broadcasted_iota(jnp.int32, (HALO, HALO), 1)
    t = lax.broadcasted_iota(jnp.int32, (tm, Wd), 0) + t0
    for g, w in enumerate(POOL_WINDOWS):
        lo_off = w // 2
        hi_off = w - lo_off
        sl = slice(g * Wd, (g + 1) * Wd)
        band = jnp.where((c >= r - lo_off) & (c < r + hi_off), 1.0, 0.0).astype(BF16)
        band_prev = jnp.where(has_prev & (ch - HALO >= rh - lo_off), 1.0, 0.0).astype(BF16)
        band_next = jnp.where(has_next & (ch < rh - HALO + hi_off), 1.0, 0.0).astype(BF16)
        cur = cur_ref[:, sl]
        wsum = _dot(band, cur)
        wsum = jnp.concatenate([wsum[0:HALO] + _dot(band_prev, prev_ref[:, sl]),
                                wsum[HALO:tm - HALO],
                                wsum[tm - HALO:] + _dot(band_next, next_ref[:, sl])], axis=0)
        cnt = jnp.minimum(t + hi_off, seq) - jnp.maximum(t - lo_off, 0)
        pooled = (wsum / cnt.astype(F32) - cur.astype(F32)).astype(BF16)
        out_ref[:, sl] = (_dot(pooled, pw_ref[g]) * ps_ref[:, sl]).astype(out_ref.dtype)


def _pool(proj, col_offset, pool_w, pool_scale, seq):
    M = proj.shape[0]
    PG, Wd, _ = pool_w.shape
    PD = PG * Wd
    assert PG == len(POOL_WINDOWS) and max(POOL_WINDOWS) // 2 <= HALO
    assert col_offset % PD == 0
    off = col_offset // PD
    tm = _pick(seq, (256, 128))
    hb = tm // HALO
    n_halo = M // HALO
    blk = (2 * _nbytes((tm, PD), BF16) + 2 * _nbytes((HALO, PD), BF16) + _nbytes((PG, Wd, Wd), BF16))
    return pl.pallas_call(
        functools.partial(_pool_kernel, tm=tm, seq=seq, Wd=Wd),
        grid=(M // tm,),
        in_specs=[pl.BlockSpec((HALO, PD), lambda i: (jnp.maximum(i * hb - 1, 0), off)),
                  pl.BlockSpec((tm, PD), lambda i: (i, off)),
                  pl.BlockSpec((HALO, PD), lambda i: (jnp.minimum((i + 1) * hb, n_halo - 1), off)),
                  pl.BlockSpec((PG, Wd, Wd), lambda i: (0, 0, 0)),
                  pl.BlockSpec((1, PD), lambda i: (0, 0))],
        out_specs=pl.BlockSpec((tm, PD), lambda i: (i, 0)),
        out_shape=jax.ShapeDtypeStruct((M, PD), BF16),
        compiler_params=_params(("parallel",), blk, 4 * _nbytes((tm, PD), F32)),
        name="pool",
    )(proj, proj, proj, pool_w, pool_scale.reshape(1, PD))


def _cumsum_rows(tri, v):
    a, b, c = _split3(v)
    return _dot(tri, a) + _dot(tri, b) + _dot(tri, c)


def _cumsum_lanes(v, triT):
    a, b, c = _split3(v)
    return _dot(a, triT) + _dot(b, triT) + _dot(c, triT)


def _hilo(v):
    a, b = _split2(v)
    return jnp.concatenate([a, b], axis=1)


def _expand(v_hilo, e2):
    return _dot(v_hilo, e2)


def _ssd_state_kernel(xf_ref, bf_ref, dtf_ref, xb_ref, bb_ref, dtb_ref, bias_ref, alog_ref, e_ref,
                      pf_ref, pb_ref, h_ref, *, H, G, N, Q):
    L = SSD_CHUNK

    @pl.when(pl.program_id(1) == 0)
    def _():
        h_ref[...] = jnp.zeros_like(h_ref)

    row = lax.broadcasted_iota(jnp.int32, (L, L), 0)
    col = lax.broadcasted_iota(jnp.int32, (L, L), 1)
    tri = jnp.where(row >= col, 1.0, 0.0).astype(BF16)
    is_fwd = lax.broadcasted_iota(jnp.int32, (L, 2 * H), 1) < H
    A = -jnp.exp(alog_ref[...])

    def direction(d, x_ref, b_ref, dtraw_ref, p_ref):
        dt = _softplus(dtraw_ref[...] + bias_ref[...])
        a = dt * A
        cum = _cumsum_rows(tri, a)
        total = cum[L - 1:L, :]
        dte = jnp.exp(jnp.where(is_fwd, total - cum, cum - a))
        factors = jnp.concatenate([dt * dte, jnp.broadcast_to(jnp.exp(total), (8, 2 * H))], axis=0)
        expanded = _expand(_hilo(factors), e_ref[d])
        xs = (x_ref[...].astype(F32) * expanded[0:L]).astype(BF16)
        cdexp = expanded[L:L + 1, :]
        bm = b_ref[...]
        for g in range(G):
            st = _dot_tn(bm[:, g * N:(g + 1) * N], xs[:, g * Q:(g + 1) * Q])
            h_old = h_ref[d, g]
            p_ref[0, 0, g] = h_old.astype(BF16)
            h_ref[d, g] = cdexp[:, g * Q:(g + 1) * Q] * h_old + st

    direction(0, xf_ref, bf_ref, dtf_ref, pf_ref)
    direction(1, xb_ref, bb_ref, dtb_ref, pb_ref)


def _ssd_states(xbc, dtraw, bias, alog, emat, batch, seq, D, H):
    L, G, N = SSD_CHUNK, SSD_GROUPS, SSD_STATE
    GN = G * N
    Q = D // G
    NC = seq // L
    assert D % GN == 0 or GN % D == 0
    b_off = D // GN
    H2 = 2 * H
    blk = 2 * (_nbytes((L, D), BF16) + _nbytes((L, GN), BF16) + _nbytes((L, H2), F32)
               + _nbytes((G, N, Q), BF16)) + _nbytes((2, 2 * H2, D), BF16)
    fwd = lambda b, t: b * NC + t
    bwd = lambda b, t: b * NC + (NC - 1 - t)
    out_sds = jax.ShapeDtypeStruct((batch, NC, G, N, Q), BF16)
    return pl.pallas_call(
        functools.partial(_ssd_state_kernel, H=H, G=G, N=N, Q=Q),
        grid=(batch, NC),
        in_specs=[pl.BlockSpec((L, D), lambda b, t: (fwd(b, t), 0)),
                  pl.BlockSpec((L, GN), lambda b, t: (fwd(b, t), b_off)),
                  pl.BlockSpec((L, H2), lambda b, t: (fwd(b, t), 0)),
                  pl.BlockSpec((L, D), lambda b, t: (bwd(b, t), 0)),
                  pl.BlockSpec((L, GN), lambda b, t: (bwd(b, t), b_off)),
                  pl.BlockSpec((L, H2), lambda b, t: (bwd(b, t), 0)),
                  pl.BlockSpec((1, H2), lambda b, t: (0, 0)),
                  pl.BlockSpec((1, H2), lambda b, t: (0, 0)),
                  pl.BlockSpec((2, 2 * H2, D), lambda b, t: (0, 0, 0))],
        out_specs=[pl.BlockSpec((1, 1, G, N, Q), lambda b, t: (b, t, 0, 0, 0)),
                   pl.BlockSpec((1, 1, G, N, Q), lambda b, t: (b, NC - 1 - t, 0, 0, 0))],
        out_shape=[out_sds, out_sds],
        scratch_shapes=[pltpu.VMEM((2, G, N, Q), F32)],
        compiler_params=_params(("parallel", "arbitrary"), blk,
                                _nbytes((2, G, N, Q), F32) + 4 * _nbytes((L, D), F32)),
        name="ssd_states",
    )(xbc, xbc, dtraw, xbc, xbc, dtraw, bias, alog, emat)


def _ssd_out_kernel(dsk_ref, x_ref, b_ref, c_ref, zs_ref, dt_ref, dtT_ref, pf_ref, pb_ref,
                    bias_ref, alog_ref, biasT_ref, alogT_ref, ng_ref, e_ref, out_ref,
                    *, H, G, N, Q, P):
    L = SSD_CHUNK
    E = H // G
    row = lax.broadcasted_iota(jnp.int32, (L, L), 0)
    col = lax.broadcasted_iota(jnp.int32, (L, L), 1)
    lower = row >= col
    eye = row == col
    tri = jnp.where(lower, 1.0, 0.0).astype(BF16)
    triT = jnp.where(row <= col, 1.0, 0.0).astype(BF16)
    is_fwd = lax.broadcasted_iota(jnp.int32, (L, 2 * H), 1) < H
    is_fwdT = lax.broadcasted_iota(jnp.int32, (2 * H, L), 0) < H

    dt = _softplus(dt_ref[...] + bias_ref[...])
    a = dt * (-jnp.exp(alog_ref[...]))
    cum = _cumsum_rows(tri, a)
    total = cum[L - 1:L, :]
    excl = cum - a
    col2 = jnp.where(is_fwd, cum, excl) * LOG2E
    sdec = _hilo(jnp.exp(jnp.where(is_fwd, cum, total - excl)))

    dtT = _softplus(dtT_ref[...] + biasT_ref[...])
    aT = dtT * (-jnp.exp(alogT_ref[...]))
    cumT = _cumsum_lanes(aT, triT)
    lg = jnp.log2(dtT)
    row2 = jnp.where(is_fwdT, cumT * LOG2E - lg, (cumT - aT) * LOG2E + lg)

    lane = lax.broadcasted_iota(jnp.int32, (L, 2 * P), 1)
    keep_left = jnp.where(lane < P, 1.0, 0.0).astype(BF16)
    keep_right = jnp.where(lane < P, 0.0, 1.0).astype(BF16)

    for g in range(G):
        c_g = c_ref[:, g * N:(g + 1) * N]
        cb = _dot_nt(c_g, b_ref[:, g * N:(g + 1) * N])
        cb_diag = jnp.sum(jnp.where(eye, cb, 0.0), axis=0, keepdims=True)
        pieces = []
        for e2 in range(E // 2):
            ws = []
            for h in (g * E + 2 * e2, g * E + 2 * e2 + 1):
                arg = jnp.where(lower,
                                col2[:, h:h + 1] - row2[h:h + 1, :],
                                row2[H + h:H + h + 1, :] - col2[:, H + h:H + h + 1])
                diag = cb_diag * dtT[H + h:H + h + 1, :] + dsk_ref[h]
                ws.append((cb * jnp.exp2(arg) + jnp.where(eye, diag, 0.0)).astype(BF16))
            h0 = g * E + 2 * e2
            xp = x_ref[:, h0 * P:(h0 + 2) * P]
            rhs = jnp.concatenate([xp * keep_left, xp * keep_right], axis=0)
            pieces.append(_dot(jnp.concatenate(ws, axis=1), rhs))
        y = jnp.concatenate(pieces, axis=1) if len(pieces) > 1 else pieces[0]
        sl = slice(g * Q, (g + 1) * Q)
        y = (y + _dot(c_g, pf_ref[0, 0, g]) * _expand(sdec, e_ref[0, :, sl])
             + _dot(c_g, pb_ref[0, 0, g]) * _expand(sdec, e_ref[1, :, sl]))
        y = y * zs_ref[:, sl].astype(F32)
        ms = jnp.mean(y * y, axis=-1, keepdims=True)
        out_ref[:, sl] = (y * lax.rsqrt(ms + EPS) * ng_ref[:, sl]).astype(out_ref.dtype)


def _ssd_out(zs, xbc, dtraw, dtrawT, prev_f, prev_b, bias, alog, biasT, alogT, d_skip, norm_g, emat,
             batch, seq, D, H):
    L, G, N = SSD_CHUNK, SSD_GROUPS, SSD_STATE
    GN = G * N
    Q = D // G
    P = D // H
    NC = seq // L
    M = batch * seq
    H2 = 2 * H
    assert (H // G) % 2 == 0 and 2 * P == 128 and D % GN == 0
    b_off = D // GN
    blk = (2 * _nbytes((L, D), BF16) + 2 * _nbytes((L, GN), BF16) + 2 * _nbytes((L, H2), F32)
           + 2 * _nbytes((G, N, Q), BF16) + _nbytes((2, 2 * H2, D), BF16) + _nbytes((L, D), BF16))
    rowblk = lambda b, c: b * NC + c
    return pl.pallas_call(
        functools.partial(_ssd_out_kernel, H=H, G=G, N=N, Q=Q, P=P),
        grid=(batch, NC),
        in_specs=[pl.BlockSpec(memory_space=pltpu.SMEM),
                  pl.BlockSpec((L, D), lambda b, c: (rowblk(b, c), 0)),
                  pl.BlockSpec((L, GN), lambda b, c: (rowblk(b, c), b_off)),
                  pl.BlockSpec((L, GN), lambda b, c: (rowblk(b, c), b_off + 1)),
                  pl.BlockSpec((L, D), lambda b, c: (rowblk(b, c), 0)),
                  pl.BlockSpec((L, H2), lambda b, c: (rowblk(b, c), 0)),
                  pl.BlockSpec((H2, L), lambda b, c: (0, rowblk(b, c))),
                  pl.BlockSpec((1, 1, G, N, Q), lambda b, c: (b, c, 0, 0, 0)),
                  pl.BlockSpec((1, 1, G, N, Q), lambda b, c: (b, c, 0, 0, 0)),
                  pl.BlockSpec((1, H2), lambda b, c: (0, 0)),
                  pl.BlockSpec((1, H2), lambda b, c: (0, 0)),
                  pl.BlockSpec((H2, L), lambda b, c: (0, 0)),
                  pl.BlockSpec((H2, L), lambda b, c: (0, 0)),
                  pl.BlockSpec((1, D), lambda b, c: (0, 0)),
                  pl.BlockSpec((2, 2 * H2, D), lambda b, c: (0, 0, 0))],
        out_specs=pl.BlockSpec((L, D), lambda b, c: (rowblk(b, c), 0)),
        out_shape=jax.ShapeDtypeStruct((M, D), BF16),
        compiler_params=_params(("parallel", "parallel"), blk, 8 * _nbytes((L, D), F32)),
        name="ssd_out",
    )(d_skip, xbc, xbc, xbc, zs, dtraw, dtrawT, prev_f, prev_b, bias, alog, biasT, alogT,
      norm_g.reshape(1, D), emat)


def _prepare_layer(norm_mix_g, w_in, conv_w, conv_b, dt_bias_f, dt_bias_b, a_log_f, a_log_b, d_skip,
                   ssd_norm_g, w_ssd_up, pool_w, pool_scale, w_pool_up, w_out, norm_mlp_g, w_ff1, w_ff2,
                   norm_ple_g, w_ple_gate, w_ple_proj):
    D = w_in.shape[0]
    H = d_skip.shape[0]
    C = conv_w.shape[1]
    s_xbc = D + C
    s_dt = s_xbc + 2 * H
    w_dt = w_in[:, s_xbc:s_dt].astype(BF16)
    head_of_lane = jnp.arange(D, dtype=jnp.int32) // (D // H)
    onehot = (jnp.arange(H, dtype=jnp.int32)[:, None] == head_of_lane[None, :]).astype(BF16)
    zeros = jnp.zeros_like(onehot)
    L = SSD_CHUNK
    bias = jnp.concatenate([dt_bias_f, dt_bias_b])
    alog = jnp.concatenate([a_log_f, a_log_b])
    return dict(
        D=D, H=H, C=C,
        norm_mix_g=norm_mix_g,
        w_in=w_in, s_dt=s_dt,
        w_dt=w_dt, w_dtT=w_dt.T,
        conv_w=conv_w, conv_b=conv_b,
        bias=bias.reshape(1, 2 * H), alog=alog.reshape(1, 2 * H),
        biasT=jnp.broadcast_to(bias[:, None], (2 * H, L)), alogT=jnp.broadcast_to(alog[:, None], (2 * H, L)),
        emat=jnp.stack([jnp.concatenate([onehot, zeros, onehot, zeros]),
                        jnp.concatenate([zeros, onehot, zeros, onehot])]),
        d_skip=d_skip, ssd_norm_g=ssd_norm_g,
        w_ssd_up=w_ssd_up, w_pool_up=w_pool_up, w_out=w_out, w_ff1=w_ff1, w_ff2=w_ff2, w_ple_gate=w_ple_gate,
        pool_w=pool_w.astype(BF16), pool_scale=pool_scale,
        norm_mlp_g=norm_mlp_g, norm_ple_g=norm_ple_g, w_ple_proj=w_ple_proj.astype(BF16),
    )


def _layer(x_parts, p_parts, lp, batch, seq):
    D, H = lp["D"], lp["H"]
    PD = lp["w_pool_up"].shape[0]
    off_ga, off_gb = PD, PD + D

    h, dtraw, dtrawT = _norm_dt(x_parts, lp["norm_mix_g"], lp["w_dt"], lp["w_dtT"])
    n_in = lp["w_in"].shape[1]
    zs, (w_rest,) = _matmul(
        h, lp["w_in"], BF16, "in_proj_z", epilogue=lambda acc: acc * _sigmoid(acc), tn=_TN_F32_WEIGHT,
        w_cols=(0, D), casts=((lp["w_in"], (D, n_in - D)),))
    tail, (w_ff1, w_out, w_pool_up) = _matmul(
        h, w_rest, BF16, "in_proj_tail", tn=_TN_WIDE, w_cols=(lp["s_dt"] - D, PD + 2 * D),
        casts=(lp["w_ff1"], lp["w_out"], lp["w_pool_up"]))

    xbc_raw, (w_ple_gate, w_ssd_up) = _matmul(h, w_rest, BF16, "in_proj_xbc", tn=_TN_WIDE, w_cols=(0, lp["C"]),
                                              casts=(lp["w_ple_gate"], lp["w_ssd_up"]))
    xbc = _conv_silu(xbc_raw, 0, lp["conv_w"], lp["conv_b"], seq)
    prev_f, prev_b = _ssd_states(xbc, dtraw, lp["bias"], lp["alog"], lp["emat"], batch, seq, D, H)
    y = _ssd_out(zs, xbc, dtraw, dtrawT, prev_f, prev_b, lp["bias"], lp["alog"], lp["biasT"],
                 lp["alogT"], lp["d_skip"], lp["ssd_norm_g"], lp["emat"], batch, seq, D, H)
    gated_a = _matmul(y, w_ssd_up, BF16, "ssd_up", tn=_TN_WIDE,
                      epilogue=lambda acc, g_ref: _sigmoid(g_ref[...].astype(F32)) * acc,
                      extras=(tail,), col_offsets=(off_ga,),
                      extra_specs=lambda tm, tn: [_tile_spec(off_ga)(tm, tn)])

    pooled = _pool(tail, 0, lp["pool_w"], lp["pool_scale"], seq)
    mixed = _matmul(pooled, w_pool_up, BF16, "pool_up", tn=_TN_WIDE,
                    epilogue=lambda acc, g_ref, a_ref: (_sigmoid(g_ref[...].astype(F32)) * acc
                                                        + a_ref[...].astype(F32)),
                    extras=(tail, gated_a), col_offsets=(off_gb,),
                    extra_specs=lambda tm, tn: [_tile_spec(off_gb)(tm, tn), _tile_spec(0)(tm, tn)])
    x, xg, ssq = _matmul_residual_norm(mixed, w_out, x_parts, lp["norm_mlp_g"], "out_proj")

    ff, (w_ff2,) = _matmul(xg, w_ff1, BF16, "ff1", ssq=ssq, tn=_TN_WIDE, casts=(lp["w_ff2"],),
                           epilogue=lambda acc: jnp.square(jnp.maximum(acc, 0.0)))
    x, xg, ssq = _matmul_kacc_residual_norm(ff, w_ff2, x, lp["norm_ple_g"], "ff2")

    ple = p_parts[0].shape[1]
    tm = _pick(functools.reduce(math.gcd, [p.shape[0] for p in p_parts]), (1024, 512, 256, 128))
    tiles = _part_tiles(p_parts, tm)

    def ple_epilogue(acc, x_ref, *refs):
        p = _part_read(refs[:-1], tiles, pl.program_id(0))
        return x_ref[...] + _sigmoid(acc) * _dot(p.astype(BF16), refs[-1][...])

    return _matmul(xg, w_ple_gate, F32, "ple", ssq=ssq, tm=tm, epilogue=ple_epilogue,
                   extras=(x, *p_parts, lp["w_ple_proj"]),
                   extra_specs=lambda tm, tn: ([_tile_spec(0)(tm, tn)] + _part_specs(p_parts, tm, ple)
                                               + [pl.BlockSpec((ple, tn), lambda i, j: (0, j))]))


def _run_groups(xs, ps, layers, norm_final_g):
    seq, D = xs[0].shape[1:]
    assert seq % SSD_CHUNK == 0
    batch = sum(x.shape[0] for x in xs)
    x_parts = [x.reshape(-1, D) for x in xs]
    for i, lp in enumerate(layers):
        p_parts = [p[i].reshape(-1, p.shape[-1]) for p in ps]
        x_parts = [_layer(x_parts, p_parts, lp, batch, seq)]
    outs = _norm_final(x_parts[0] if len(x_parts) == 1 else jnp.concatenate(x_parts), norm_final_g,
                       [x.shape[0] * seq for x in xs])
    return [o.reshape(x.shape) for o, x in zip(outs, xs)]


def kernel(x_prompt, x_sample, p_prompt, p_sample, norm_mix_g, w_in, conv_w, conv_b, dt_bias_f, dt_bias_b, a_log_f, a_log_b, d_skip, ssd_norm_g, w_ssd_up, pool_w, pool_scale, w_pool_up, w_out, norm_mlp_g, w_ff1, w_ff2, norm_ple_g, w_ple_gate, w_ple_proj, norm_final_g):
    stacked = (norm_mix_g, w_in, conv_w, conv_b, dt_bias_f, dt_bias_b, a_log_f, a_log_b, d_skip,
               ssd_norm_g, w_ssd_up, pool_w, pool_scale, w_pool_up, w_out, norm_mlp_g, w_ff1, w_ff2,
               norm_ple_g, w_ple_gate, w_ple_proj)
    layers = [_prepare_layer(*[w[i] for w in stacked]) for i in range(w_in.shape[0])]
    if x_prompt.shape[1] == x_sample.shape[1]:
        y_prompt, y_sample = _run_groups([x_prompt, x_sample], [p_prompt, p_sample], layers, norm_final_g)
    else:
        (y_prompt,) = _run_groups([x_prompt], [p_prompt], layers, norm_final_g)
        (y_sample,) = _run_groups([x_sample], [p_sample], layers, norm_final_g)
    return (y_prompt, y_sample)
```

```python
import functools
import math

import jax
import jax.numpy as jnp
from jax import lax
from jax.experimental import pallas as pl
from jax.experimental.pallas import tpu as pltpu

F32 = jnp.float32
BF16 = jnp.bfloat16

EPS = 1e-6
LOG2E = 1.4426950408889634
LN2 = 0.6931471805599453
SSD_GROUPS = 8
SSD_STATE = 128
SSD_CHUNK = 128
POOL_WINDOWS = (2, 4, 8, 16)

V7X_VMEM_BYTES = 64 * 1024 * 1024
V7X_VMEM_RESERVE_BYTES = 8 * 1024 * 1024
BF16_SUBLANE_TILE = 16
LANES = 128
HALO = BF16_SUBLANE_TILE
_TN_WIDE = 1024
_TN_CARRIER = 512
CONV_SUB_ROWS = 128
CAST_BLOCK_BYTES = 2 * 1024 * 1024


def _pick(dim, prefs):
    for p in prefs:
        if dim % p == 0:
            return p
    return dim


def _nbytes(shape, dtype):
    n = 1
    for s in shape:
        n *= s
    return n * jnp.dtype(dtype).itemsize


def _params(semantics, block_bytes, scratch_bytes=0):
    est = 2 * block_bytes + scratch_bytes
    limit = min(2 * est + (4 << 20), V7X_VMEM_BYTES - V7X_VMEM_RESERVE_BYTES)
    return pltpu.CompilerParams(dimension_semantics=semantics, vmem_limit_bytes=int(limit))


def _sigmoid(v):
    return 1.0 / (1.0 + jnp.exp(-v))


def _softplus(v):
    u = jnp.exp(-jnp.abs(v))
    w = 1.0 + u
    return jnp.maximum(v, 0.0) + jnp.where(w == 1.0, u, (jnp.log2(w) * LN2) * (u / (w - 1.0)))


def _split3(v):
    a = v.astype(BF16)
    r = v - a.astype(F32)
    b = r.astype(BF16)
    c = (r - b.astype(F32)).astype(BF16)
    return a, b, c


def _split2(v):
    a = v.astype(BF16)
    b = (v - a.astype(F32)).astype(BF16)
    return a, b


def _dot(a, b):
    return jnp.dot(a, b, preferred_element_type=F32)


def _dot_nt(a, b):
    return lax.dot_general(a, b, (((1,), (1,)), ((), ())), preferred_element_type=F32)


def _dot_tn(a, b):
    return lax.dot_general(a, b, (((0,), (0,)), ((), ())), preferred_element_type=F32)


def _rms(x, g):
    ms = jnp.mean(x * x, axis=-1, keepdims=True)
    return x * lax.rsqrt(ms + EPS) * g


def _part_tiles(parts, tm):
    return [p.shape[0] // tm for p in parts]


def _part_specs(parts, tm, ncols, col_index=None):
    specs, start = [], 0
    for n in _part_tiles(parts, tm):
        def index(*idx, s=start, n=n):
            row = idx[0] - s
            if col_index is None:
                return jnp.clip(row, 0, n - 1), 0
            return jnp.clip(row, 0, n - 1), jnp.where((row >= 0) & (row < n), col_index(*idx), 0)
        specs.append(pl.BlockSpec((tm, ncols), index))
        start += n
    return specs


def _part_read(refs, tiles, i):
    val = refs[-1][...]
    end = sum(tiles[:-1])
    for ref, n in zip(reversed(refs[:-1]), reversed(tiles[:-1])):
        val = jnp.where(i < end, ref[...], val)
        end -= n
    return val


def _norm_dt_kernel(*refs, tiles):
    n = len(tiles)
    x_refs, (g_ref, wdt_ref, wdtT_ref, h_ref, dt_ref, dtT_ref) = refs[:n], refs[n:]
    h = _rms(_part_read(x_refs, tiles, pl.program_id(0)), g_ref[...]).astype(BF16)
    h_ref[...] = h
    dt_ref[...] = _dot(h, wdt_ref[...])
    dtT_ref[...] = _dot_nt(wdtT_ref[...], h)


def _norm_dt(x_parts, g, w_dt, w_dtT):
    D = x_parts[0].shape[1]
    M = sum(p.shape[0] for p in x_parts)
    H2 = w_dt.shape[1]
    tm = _pick(functools.reduce(math.gcd, [p.shape[0] for p in x_parts]), (512, 256, 128))
    blk = (len(x_parts) * _nbytes((tm, D), F32) + _nbytes((tm, D), BF16) + 2 * _nbytes((D, H2), BF16)
           + 2 * _nbytes((tm, H2), F32))
    return pl.pallas_call(
        functools.partial(_norm_dt_kernel, tiles=_part_tiles(x_parts, tm)),
        grid=(M // tm,),
        in_specs=_part_specs(x_parts, tm, D) + [
            pl.BlockSpec((1, D), lambda i: (0, 0)),
            pl.BlockSpec((D, H2), lambda i: (0, 0)),
            pl.BlockSpec((H2, D), lambda i: (0, 0))],
        out_specs=[pl.BlockSpec((tm, D), lambda i: (i, 0)),
                   pl.BlockSpec((tm, H2), lambda i: (i, 0)),
                   pl.BlockSpec((H2, tm), lambda i: (0, i))],
        out_shape=[jax.ShapeDtypeStruct((M, D), BF16),
                   jax.ShapeDtypeStruct((M, H2), F32),
                   jax.ShapeDtypeStruct((H2, M), F32)],
        compiler_params=_params(("arbitrary",), blk),
        name="norm_dt",
    )(*x_parts, g.reshape(1, D), w_dt, w_dtT)


def _norm_final_kernel(x_ref, g_ref, *o_refs, tiles):
    i = pl.program_id(0)
    y = _rms(x_ref[...], g_ref[...])
    start = 0
    for o_ref, n in zip(o_refs, tiles):
        @pl.when((i >= start) & (i < start + n))
        def _(o_ref=o_ref):
            o_ref[...] = y
        start += n


def _norm_final(x, g, part_rows):
    M, D = x.shape
    tm = _pick(functools.reduce(math.gcd, part_rows), (512, 256, 128))
    outs = [jax.ShapeDtypeStruct((r, D), F32) for r in part_rows]
    blk = (1 + len(part_rows)) * _nbytes((tm, D), F32)
    return pl.pallas_call(
        functools.partial(_norm_final_kernel, tiles=_part_tiles(outs, tm)),
        grid=(M // tm,),
        in_specs=[pl.BlockSpec((tm, D), lambda i: (i, 0)),
                  pl.BlockSpec((1, D), lambda i: (0, 0))],
        out_specs=_part_specs(outs, tm, D),
        out_shape=outs,
        compiler_params=_params(("arbitrary",), blk),
        name="norm_final",
    )(x, g.reshape(1, D))


def _row_rsqrt(ssq_ref, d):
    return lax.rsqrt(jnp.sum(ssq_ref[...], axis=-1, keepdims=True) * (1.0 / d) + EPS)


def _lane_partial_sumsq(v):
    sq = v * v
    part = sq[:, 0:LANES]
    for q in range(1, v.shape[1] // LANES):
        part = part + sq[:, q * LANES:(q + 1) * LANES]
    return part


def _mm_kernel(lhs_ref, w_ref, *rest, epilogue, norm_width, n_casts):
    n_in = len(rest) - 1 - 2 * n_casts
    ins, cast_srcs, out_ref, cast_dsts = rest[:n_in], rest[n_in:n_in + n_casts], rest[n_in + n_casts], rest[n_in + n_casts + 1:]
    acc = _dot(lhs_ref[...], w_ref[...].astype(lhs_ref.dtype))
    if norm_width:
        acc = acc * _row_rsqrt(ins[0], norm_width)
        ins = ins[1:]
    out_ref[...] = epilogue(acc, *ins).astype(out_ref.dtype)
    for src_ref, dst_ref in zip(cast_srcs, cast_dsts):
        dst_ref[...] = src_ref[...].astype(dst_ref.dtype)


def _plan_casts(items, steps):
    jobs, rest, used = [], [], 0
    for item in items:
        a, (c0, cw) = item if isinstance(item, tuple) else (item, (0, item.shape[1]))
        cols = _pick(math.gcd(cw, c0) if c0 else cw, (2048, 1024, 512, 256, 128))
        fit = [r for r in (128, 256, 512, 1024, 2048, 4096) if a.shape[0] % r == 0 and cw % cols == 0
               and c0 % cols == 0 and cols % LANES == 0
               and _nbytes((r, cols), F32) <= CAST_BLOCK_BYTES
               and used + (a.shape[0] // r) * (cw // cols) <= steps]
        if fit and not rest:
            jobs.append((a, c0, cw, (fit[0], cols)))
            used += (a.shape[0] // fit[0]) * (cw // cols)
        else:
            rest.append(a[:, c0:c0 + cw])
    return tuple(jobs), rest


def _cast_specs(jobs, n_i, n_j):
    src_specs, dst_specs, start = [], [], 0
    for a, c0, cw, block in jobs:
        n_rb, n_cb = a.shape[0] // block[0], cw // block[1]

        def index(i, j, start=start, n=n_rb * n_cb, n_cb=n_cb, cb0=0):
            b = jnp.clip(i * n_j + j - start, 0, n - 1)
            return b // n_cb, cb0 + b % n_cb
        src_specs.append(pl.BlockSpec(block, functools.partial(index, cb0=c0 // block[1])))
        dst_specs.append(pl.BlockSpec(block, index))
        start += n_rb * n_cb
    assert start <= n_i * n_j, "not enough grid steps to carry the weight casts"
    return src_specs, dst_specs


def _matmul(lhs, w, out_dtype, name, epilogue=None, extras=(), extra_specs=None, ssq=None, tm=None, tn=None,
            col_offsets=(), w_cols=None, casts=()):
    M, K = lhs.shape
    w_off, N = w_cols if w_cols else (0, w.shape[1])
    tm = tm or _pick(M, (1024, 512, 256, 128))
    tn = next(t for t in ((tn,) if tn else ()) + (512, 256, 128)
              if N % t == 0 and all(o % t == 0 for o in col_offsets))
    if epilogue is None:
        epilogue = lambda acc: acc
    specs = extra_specs(tm, tn) if extra_specs else []
    if ssq is not None:
        extras = (ssq,) + tuple(extras)
        specs = [pl.BlockSpec((tm, LANES), lambda i, j: (i, 0))] + specs
    cast_items = tuple(casts)
    jobs, uncast = _plan_casts(cast_items, (M // tm) * (N // tn))
    cast_src_specs, cast_dst_specs = _cast_specs(jobs, M // tm, N // tn)
    blk = (_nbytes((tm, K), lhs.dtype) + _nbytes((K, tn), w.dtype) + _nbytes((tm, tn), out_dtype)
           + sum(_nbytes(s.block_shape, e.dtype) for s, e in zip(specs, extras))
           + sum(_nbytes(job[3], F32) + _nbytes(job[3], BF16) for job in jobs)
           + (_nbytes((K, tn), BF16) if w.dtype != lhs.dtype else 0))
    if w_off % tn == 0:
        w_blk0 = w_off // tn
        w_spec = pl.BlockSpec((K, tn), lambda i, j: (0, w_blk0 + j))
    elif w_off % LANES == 0:
        w_spec = pl.BlockSpec((pl.Element(K), pl.Element(tn)),
                              lambda i, j: (0, pl.multiple_of(w_off + j * tn, LANES)))
    else:
        w = w[:, w_off:w_off + N]
        w_spec = pl.BlockSpec((K, tn), lambda i, j: (0, j))
    outs = pl.pallas_call(
        functools.partial(_mm_kernel, epilogue=epilogue, norm_width=K if ssq is not None else 0,
                          n_casts=len(jobs)),
        grid=(M // tm, N // tn),
        in_specs=[pl.BlockSpec((tm, K), lambda i, j: (i, 0)), w_spec] + specs + cast_src_specs,
        out_specs=[pl.BlockSpec((tm, tn), lambda i, j: (i, j))] + cast_dst_specs,
        out_shape=[jax.ShapeDtypeStruct((M, N), out_dtype)]
        + [jax.ShapeDtypeStruct((job[0].shape[0], job[2]), BF16) for job in jobs],
        compiler_params=_params(("arbitrary", "arbitrary") if jobs else ("parallel", "arbitrary"),
                                blk, _nbytes((tm, tn), F32)),
        name=name,
    )(lhs, w, *extras, *[job[0] for job in jobs])
    if not cast_items:
        return outs[0]
    return outs[0], tuple(outs[1:]) + tuple(a.astype(BF16) for a in uncast)


def _tile_spec(col_offset):
    def make(tm, tn):
        assert col_offset % tn == 0
        off = col_offset // tn
        return pl.BlockSpec((tm, tn), lambda i, j: (i, off + j))
    return make


def _emit_norm_operand(x_new, g_ref, xg_ref, ssq_ref, first):
    xg_ref[...] = (x_new * g_ref[...]).astype(xg_ref.dtype)
    part = _lane_partial_sumsq(x_new)

    @pl.when(first)
    def _():
        ssq_ref[...] = part

    @pl.when(jnp.logical_not(first))
    def _():
        ssq_ref[...] += part


def _mm_res_norm_kernel(lhs_ref, w_ref, *refs, tiles):
    n = len(tiles)
    res_refs, (g_ref, x_ref, xg_ref, ssq_ref) = refs[:n], refs[n:]
    x_new = _part_read(res_refs, tiles, pl.program_id(0)) + _dot(lhs_ref[...], w_ref[...])
    x_ref[...] = x_new
    _emit_norm_operand(x_new, g_ref, xg_ref, ssq_ref, pl.program_id(1) == 0)


def _matmul_residual_norm(lhs, w, res_parts, g_next, name):
    M, K = lhs.shape
    N = w.shape[1]
    tm = _pick(functools.reduce(math.gcd, [p.shape[0] for p in res_parts]), (1024, 512, 256, 128))
    tn = _pick(N, (512, 256, 128))
    blk = (_nbytes((tm, K), lhs.dtype) + _nbytes((K, tn), w.dtype) + (1 + len(res_parts)) * _nbytes((tm, tn), F32)
           + _nbytes((tm, tn), BF16) + _nbytes((tm, LANES), F32))
    return pl.pallas_call(
        functools.partial(_mm_res_norm_kernel, tiles=_part_tiles(res_parts, tm)),
        grid=(M // tm, N // tn),
        in_specs=[pl.BlockSpec((tm, K), lambda i, j: (i, 0)),
                  pl.BlockSpec((K, tn), lambda i, j: (0, j))]
        + _part_specs(res_parts, tm, tn, col_index=lambda i, j: j)
        + [pl.BlockSpec((1, tn), lambda i, j: (0, j))],
        out_specs=[pl.BlockSpec((tm, tn), lambda i, j: (i, j)),
                   pl.BlockSpec((tm, tn), lambda i, j: (i, j)),
                   pl.BlockSpec((tm, LANES), lambda i, j: (i, 0))],
        out_shape=[jax.ShapeDtypeStruct((M, N), F32),
                   jax.ShapeDtypeStruct((M, N), BF16),
                   jax.ShapeDtypeStruct((M, LANES), F32)],
        compiler_params=_params(("arbitrary", "arbitrary"), blk, _nbytes((tm, tn), F32)),
        name=name,
    )(lhs, w, *res_parts, g_next.reshape(1, N))


def _mm_kacc_kernel(lhs_ref, w_ref, res_ref, g_ref, x_ref, xg_ref, ssq_ref, *, nk):
    k = pl.program_id(2)

    def step(base_ref, finish):
        x_new = base_ref[...] + _dot(lhs_ref[...], w_ref[...])
        x_ref[...] = x_new
        if finish:
            _emit_norm_operand(x_new, g_ref, xg_ref, ssq_ref, pl.program_id(1) == 0)

    if nk == 1:
        step(res_ref, True)
        return
    pl.when(k == 0)(lambda: step(res_ref, False))
    if nk > 2:
        pl.when((k > 0) & (k < nk - 1))(lambda: step(x_ref, False))
    pl.when(k == nk - 1)(lambda: step(x_ref, True))


def _matmul_kacc_residual_norm(lhs, w, res, g_next, name):
    M, K = lhs.shape
    N = w.shape[1]
    tm = _pick(M, (1024, 512, 256, 128))
    tn = _pick(N, (1024, 512, 256, 128))
    tk = _pick(K, (2048, 1024, 512, 256, 128))
    blk = (_nbytes((tm, tk), lhs.dtype) + _nbytes((tk, tn), w.dtype) + 2 * _nbytes((tm, tn), F32)
           + _nbytes((tm, tn), BF16) + _nbytes((tm, LANES), F32))
    return pl.pallas_call(
        functools.partial(_mm_kacc_kernel, nk=K // tk),
        grid=(M // tm, N // tn, K // tk),
        in_specs=[pl.BlockSpec((tm, tk), lambda i, j, k: (i, k)),
                  pl.BlockSpec((tk, tn), lambda i, j, k: (k, j)),
                  pl.BlockSpec((tm, tn), lambda i, j, k: (i, j)),
                  pl.BlockSpec((1, tn), lambda i, j, k: (0, j))],
        out_specs=[pl.BlockSpec((tm, tn), lambda i, j, k: (i, j)),
                   pl.BlockSpec((tm, tn), lambda i, j, k: (i, j)),
                   pl.BlockSpec((tm, LANES), lambda i, j, k: (i, 0))],
        out_shape=[jax.ShapeDtypeStruct((M, N), F32),
                   jax.ShapeDtypeStruct((M, N), BF16),
                   jax.ShapeDtypeStruct((M, LANES), F32)],
        compiler_params=_params(("parallel", "arbitrary", "arbitrary"), blk, _nbytes((tm, tn), F32)),
        name=name,
    )(lhs, w, res, g_next.reshape(1, N))


def _conv_kernel(prev_ref, cur_ref, next_ref, w_ref, b_ref, shift_ref, out_ref, buf_ref, *, tm, seq_tiles, width):
    i = pl.program_id(0)
    first = (i % seq_tiles) == 0
    last = (i % seq_tiles) == seq_tiles - 1
    prev = prev_ref[...]
    nxt = next_ref[...]
    buf_ref[0:HALO, :] = jnp.where(first, jnp.zeros_like(prev), prev)
    buf_ref[HALO:HALO + tm, :] = cur_ref[...]
    buf_ref[HALO + tm:, :] = jnp.where(last, jnp.zeros_like(nxt), nxt)

    half = width // 2
    taps = [k for k in range(width) if k != half]
    sub = shift_ref.shape[0]
    win = sub + 2 * HALO
    shift = shift_ref[...]
    w_taps = [w_ref[k:k + 1, :].astype(BF16) for k in taps]
    for r0 in range(0, tm, sub):
        window = buf_ref[r0:r0 + win, :]
        scaled = jnp.concatenate([window * wk for wk in w_taps], axis=0)
        acc = (b_ref[...] + w_ref[half:half + 1, :] * window[HALO:HALO + sub].astype(F32)
               + _dot(shift, scaled))
        out_ref[r0:r0 + sub, :] = (acc * _sigmoid(acc)).astype(out_ref.dtype)


def _conv_silu(proj, col_offset, conv_w, conv_b, seq):
    M = proj.shape[0]
    width, C = conv_w.shape
    assert width // 2 <= HALO
    tm = _pick(seq, (512, 256, 128))
    tc = _pick(C, (2048, 1024, 512, 256, 128))
    while col_offset % tc:
        tc //= 2
    off = col_offset // tc
    hb = tm // HALO
    n_halo = M // HALO
    half = width // 2
    sub = min(tm, CONV_SUB_ROWS)
    win = sub + 2 * HALO
    rows = jnp.arange(sub, dtype=jnp.int32)[:, None]
    cols = jnp.arange(win, dtype=jnp.int32)[None, :]
    shift = jnp.concatenate([(cols == rows + HALO + (k - half)) for k in range(width) if k != half],
                            axis=1).astype(BF16)
    blk = (2 * _nbytes((tm, tc), BF16) + 2 * _nbytes((HALO, tc), BF16) + _nbytes((width + 1, tc), F32)
           + _nbytes(shift.shape, BF16))
    return pl.pallas_call(
        functools.partial(_conv_kernel, tm=tm, seq_tiles=seq // tm, width=width),
        grid=(M // tm, C // tc),
        in_specs=[pl.BlockSpec((HALO, tc), lambda i, j: (jnp.maximum(i * hb - 1, 0), off + j)),
                  pl.BlockSpec((tm, tc), lambda i, j: (i, off + j)),
                  pl.BlockSpec((HALO, tc), lambda i, j: (jnp.minimum((i + 1) * hb, n_halo - 1), off + j)),
                  pl.BlockSpec((width, tc), lambda i, j: (0, j)),
                  pl.BlockSpec((1, tc), lambda i, j: (0, j)),
                  pl.BlockSpec(shift.shape, lambda i, j: (0, 0))],
        out_specs=pl.BlockSpec((tm, tc), lambda i, j: (i, j)),
        out_shape=jax.ShapeDtypeStruct((M, C), BF16),
        scratch_shapes=[pltpu.VMEM((tm + 2 * HALO, tc), BF16)],
        compiler_params=_params(("parallel", "parallel"), blk,
                                _nbytes((tm + 2 * HALO, tc), BF16) + 8 * _nbytes((CONV_SUB_ROWS, tc), F32)),
        name="conv_silu",
    )(proj, proj, proj, conv_w, conv_b.reshape(1, C), shift)


def _pool_kernel(prev_ref, cur_ref, next_ref, pw_ref, ps_ref, out_ref, *, tm, seq, Wd):
    i = pl.program_id(0)
    t0 = (i * tm) % seq
    has_prev = t0 > 0
    has_next = t0 + tm < seq
    r = lax.broadcasted_iota(jnp.int32, (tm, tm), 0)
    c = lax.broadcasted_iota(jnp.int32, (tm, tm), 1)
    rh = lax.broadcasted_iota(jnp.int32, (HALO, HALO), 0)
    ch = lax.broadcasted_iota(jnp.int32, (HALO, HALO), 1)
    t = lax.broadcasted_iota(jnp.int32, (tm, Wd), 0) + t0
    for g, w in enumerate(POOL_WINDOWS):
        lo_off = w // 2
        hi_off = w - lo_off
        sl = slice(g * Wd, (g + 1) * Wd)
        band = jnp.where((c >= r - lo_off) & (c < r + hi_off), 1.0, 0.0).astype(BF16)
        band_prev = jnp.where(has_prev & (ch - HALO >= rh - lo_off), 1.0, 0.0).astype(BF16)
        band_next = jnp.where(has_next & (ch < rh - HALO + hi_off), 1.0, 0.0).astype(BF16)
        cur = cur_ref[:, sl]
        wsum = _dot(band, cur)
        wsum = jnp.concatenate([wsum[0:HALO] + _dot(band_prev, prev_ref[:, sl]),
                                wsum[HALO:tm - HALO],
                                wsum[tm - HALO:] + _dot(band_next, next_ref[:, sl])], axis=0)
        cnt = jnp.minimum(t + hi_off, seq) - jnp.maximum(t - lo_off, 0)
        pooled = (wsum / cnt.astype(F32) - cur.astype(F32)).astype(BF16)
        out_ref[:, sl] = (_dot(pooled, pw_ref[g]) * ps_ref[:, sl]).astype(out_ref.dtype)


def _pool(proj, col_offset, pool_w, pool_scale, seq):
    M = proj.shape[0]
    PG, Wd, _ = pool_w.shape
    PD = PG * Wd
    assert PG == len(POOL_WINDOWS) and max(POOL_WINDOWS) // 2 <= HALO
    assert col_offset % PD == 0
    off = col_offset // PD
    tm = _pick(seq, (256, 128))
    hb = tm // HALO
    n_halo = M // HALO
    blk = (2 * _nbytes((tm, PD), BF16) + 2 * _nbytes((HALO, PD), BF16) + _nbytes((PG, Wd, Wd), BF16))
    return pl.pallas_call(
        functools.partial(_pool_kernel, tm=tm, seq=seq, Wd=Wd),
        grid=(M // tm,),
        in_specs=[pl.BlockSpec((HALO, PD), lambda i: (jnp.maximum(i * hb - 1, 0), off)),
                  pl.BlockSpec((tm, PD), lambda i: (i, off)),
                  pl.BlockSpec((HALO, PD), lambda i: (jnp.minimum((i + 1) * hb, n_halo - 1), off)),
                  pl.BlockSpec((PG, Wd, Wd), lambda i: (0, 0, 0)),
                  pl.BlockSpec((1, PD), lambda i: (0, 0))],
        out_specs=pl.BlockSpec((tm, PD), lambda i: (i, 0)),
        out_shape=jax.ShapeDtypeStruct((M, PD), BF16),
        compiler_params=_params(("parallel",), blk, 4 * _nbytes((tm, PD), F32)),
        name="pool",
    )(proj, proj, proj, pool_w, pool_scale.reshape(1, PD))


def _cumsum_rows(tri, v):
    a, b, c = _split3(v)
    return _dot(tri, a) + _dot(tri, b) + _dot(tri, c)


def _cumsum_lanes(v, triT):
    a, b, c = _split3(v)
    return _dot(a, triT) + _dot(b, triT) + _dot(c, triT)


def _hilo(v):
    a, b = _split2(v)
    return jnp.concatenate([a, b], axis=1)


def _expand(v_hilo, e2):
    return _dot(v_hilo, e2)


def _ssd_state_kernel(xf_ref, bf_ref, dtf_ref, xb_ref, bb_ref, dtb_ref, bias_ref, alog_ref, e_ref,
                      pf_ref, pb_ref, h_ref, *, H, G, N, Q):
    L = SSD_CHUNK

    @pl.when(pl.program_id(1) == 0)
    def _():
        h_ref[...] = jnp.zeros_like(h_ref)

    row = lax.broadcasted_iota(jnp.int32, (L, L), 0)
    col = lax.broadcasted_iota(jnp.int32, (L, L), 1)
    tri = jnp.where(row >= col, 1.0, 0.0).astype(BF16)
    is_fwd = lax.broadcasted_iota(jnp.int32, (L, 2 * H), 1) < H
    A = -jnp.exp(alog_ref[...])

    def direction(d, x_ref, b_ref, dtraw_ref, p_ref):
        dt = _softplus(dtraw_ref[...] + bias_ref[...])
        a = dt * A
        cum = _cumsum_rows(tri, a)
        total = cum[L - 1:L, :]
        dte = jnp.exp(jnp.where(is_fwd, total - cum, cum - a))
        factors = jnp.concatenate([dt * dte, jnp.broadcast_to(jnp.exp(total), (8, 2 * H))], axis=0)
        expanded = _expand(_hilo(factors), e_ref[d])
        xs = (x_ref[...].astype(F32) * expanded[0:L]).astype(BF16)
        cdexp = expanded[L:L + 1, :]
        bm = b_ref[...]
        for g in range(G):
            st = _dot_tn(bm[:, g * N:(g + 1) * N], xs[:, g * Q:(g + 1) * Q])
            h_old = h_ref[d, g]
            p_ref[0, 0, g] = h_old.astype(BF16)
            h_ref[d, g] = cdexp[:, g * Q:(g + 1) * Q] * h_old + st

    direction(0, xf_ref, bf_ref, dtf_ref, pf_ref)
    direction(1, xb_ref, bb_ref, dtb_ref, pb_ref)


def _ssd_states(xbc, dtraw, bias, alog, emat, batch, seq, D, H):
    L, G, N = SSD_CHUNK, SSD_GROUPS, SSD_STATE
    GN = G * N
    Q = D // G
    NC = seq // L
    assert D % GN == 0 or GN % D == 0
    b_off = D // GN
    H2 = 2 * H
    blk = 2 * (_nbytes((L, D), BF16) + _nbytes((L, GN), BF16) + _nbytes((L, H2), F32)
               + _nbytes((G, N, Q), BF16)) + _nbytes((2, 2 * H2, D), BF16)
    fwd = lambda b, t: b * NC + t
    bwd = lambda b, t: b * NC + (NC - 1 - t)
    out_sds = jax.ShapeDtypeStruct((batch, NC, G, N, Q), BF16)
    return pl.pallas_call(
        functools.partial(_ssd_state_kernel, H=H, G=G, N=N, Q=Q),
        grid=(batch, NC),
        in_specs=[pl.BlockSpec((L, D), lambda b, t: (fwd(b, t), 0)),
                  pl.BlockSpec((L, GN), lambda b, t: (fwd(b, t), b_off)),
                  pl.BlockSpec((L, H2), lambda b, t: (fwd(b, t), 0)),
                  pl.BlockSpec((L, D), lambda b, t: (bwd(b, t), 0)),
                  pl.BlockSpec((L, GN), lambda b, t: (bwd(b, t), b_off)),
                  pl.BlockSpec((L, H2), lambda b, t: (bwd(b, t), 0)),
                  pl.BlockSpec((1, H2), lambda b, t: (0, 0)),
                  pl.BlockSpec((1, H2), lambda b, t: (0, 0)),
                  pl.BlockSpec((2, 2 * H2, D), lambda b, t: (0, 0, 0))],
        out_specs=[pl.BlockSpec((1, 1, G, N, Q), lambda b, t: (b, t, 0, 0, 0)),
                   pl.BlockSpec((1, 1, G, N, Q), lambda b, t: (b, NC - 1 - t, 0, 0, 0))],
        out_shape=[out_sds, out_sds],
        scratch_shapes=[pltpu.VMEM((2, G, N, Q), F32)],
        compiler_params=_params(("parallel", "arbitrary"), blk,
                                _nbytes((2, G, N, Q), F32) + 4 * _nbytes((L, D), F32)),
        name="ssd_states",
    )(xbc, xbc, dtraw, xbc, xbc, dtraw, bias, alog, emat)


def _ssd_out_kernel(dsk_ref, x_ref, b_ref, c_ref, zs_ref, dt_ref, dtT_ref, pf_ref, pb_ref,
                    bias_ref, alog_ref, biasT_ref, alogT_ref, ng_ref, e_ref, out_ref,
                    *, H, G, N, Q, P):
    L = SSD_CHUNK
    E = H // G
    row = lax.broadcasted_iota(jnp.int32, (L, L), 0)
    col = lax.broadcasted_iota(jnp.int32, (L, L), 1)
    lower = row >= col
    eye = row == col
    tri = jnp.where(lower, 1.0, 0.0).astype(BF16)
    triT = jnp.where(row <= col, 1.0, 0.0).astype(BF16)
    is_fwd = lax.broadcasted_iota(jnp.int32, (L, 2 * H), 1) < H
    is_fwdT = lax.broadcasted_iota(jnp.int32, (2 * H, L), 0) < H

    dt = _softplus(dt_ref[...] + bias_ref[...])
    a = dt * (-jnp.exp(alog_ref[...]))
    cum = _cumsum_rows(tri, a)
    total = cum[L - 1:L, :]
    excl = cum - a
    col2 = jnp.where(is_fwd, cum, excl) * LOG2E
    sdec = _hilo(jnp.exp(jnp.where(is_fwd, cum, total - excl)))

    dtT = _softplus(dtT_ref[...] + biasT_ref[...])
    aT = dtT * (-jnp.exp(alogT_ref[...]))
    cumT = _cumsum_lanes(aT, triT)
    lg = jnp.log2(dtT)
    row2 = jnp.where(is_fwdT, cumT * LOG2E - lg, (cumT - aT) * LOG2E + lg)

    lane = lax.broadcasted_iota(jnp.int32, (L, 2 * P), 1)
    keep_left = jnp.where(lane < P, 1.0, 0.0).astype(BF16)
    keep_right = jnp.where(lane < P, 0.0, 1.0).astype(BF16)

    for g in range(G):
        c_g = c_ref[:, g * N:(g + 1) * N]
        cb = _dot_nt(c_g, b_ref[:, g * N:(g + 1) * N])
        cb_diag = jnp.sum(jnp.where(eye, cb, 0.0), axis=0, keepdims=True)
        pieces = []
        for e2 in range(E // 2):
            ws = []
            for h in (g * E + 2 * e2, g * E + 2 * e2 + 1):
                arg = jnp.where(lower,
                                col2[:, h:h + 1] - row2[h:h + 1, :],
                                row2[H + h:H + h + 1, :] - col2[:, H + h:H + h + 1])
                diag = cb_diag * dtT[H + h:H + h + 1, :] + dsk_ref[h]
                ws.append((cb * jnp.exp2(arg) + jnp.where(eye, diag, 0.0)).astype(BF16))
            h0 = g * E + 2 * e2
            xp = x_ref[:, h0 * P:(h0 + 2) * P]
            rhs = jnp.concatenate([xp * keep_left, xp * keep_right], axis=0)
            pieces.append(_dot(jnp.concatenate(ws, axis=1), rhs))
        y = jnp.concatenate(pieces, axis=1) if len(pieces) > 1 else pieces[0]
        sl = slice(g * Q, (g + 1) * Q)
        y = (y + _dot(c_g, pf_ref[0, 0, g]) * _expand(sdec, e_ref[0, :, sl])
             + _dot(c_g, pb_ref[0, 0, g]) * _expand(sdec, e_ref[1, :, sl]))
        y = y * zs_ref[:, sl].astype(F32)
        ms = jnp.mean(y * y, axis=-1, keepdims=True)
        out_ref[:, sl] = (y * lax.rsqrt(ms + EPS) * ng_ref[:, sl]).astype(out_ref.dtype)


def _ssd_out(zs, xbc, dtraw, dtrawT, prev_f, prev_b, bias, alog, biasT, alogT, d_skip, norm_g, emat,
             batch, seq, D, H):
    L, G, N = SSD_CHUNK, SSD_GROUPS, SSD_STATE
    GN = G * N
    Q = D // G
    P = D // H
    NC = seq // L
    M = batch * seq
    H2 = 2 * H
    assert (H // G) % 2 == 0 and 2 * P == 128 and D % GN == 0
    b_off = D // GN
    blk = (2 * _nbytes((L, D), BF16) + 2 * _nbytes((L, GN), BF16) + 2 * _nbytes((L, H2), F32)
           + 2 * _nbytes((G, N, Q), BF16) + _nbytes((2, 2 * H2, D), BF16) + _nbytes((L, D), BF16))
    rowblk = lambda b, c: b * NC + c
    return pl.pallas_call(
        functools.partial(_ssd_out_kernel, H=H, G=G, N=N, Q=Q, P=P),
        grid=(batch, NC),
        in_specs=[pl.BlockSpec(memory_space=pltpu.SMEM),
                  pl.BlockSpec((L, D), lambda b, c: (rowblk(b, c), 0)),
                  pl.BlockSpec((L, GN), lambda b, c: (rowblk(b, c), b_off)),
                  pl.BlockSpec((L, GN), lambda b, c: (rowblk(b, c), b_off + 1)),
                  pl.BlockSpec((L, D), lambda b, c: (rowblk(b, c), 0)),
                  pl.BlockSpec((L, H2), lambda b, c: (rowblk(b, c), 0)),
                  pl.BlockSpec((H2, L), lambda b, c: (0, rowblk(b, c))),
                  pl.BlockSpec((1, 1, G, N, Q), lambda b, c: (b, c, 0, 0, 0)),
                  pl.BlockSpec((1, 1, G, N, Q), lambda b, c: (b, c, 0, 0, 0)),
                  pl.BlockSpec((1, H2), lambda b, c: (0, 0)),
                  pl.BlockSpec((1, H2), lambda b, c: (0, 0)),
                  pl.BlockSpec((H2, L), lambda b, c: (0, 0)),
                  pl.BlockSpec((H2, L), lambda b, c: (0, 0)),
                  pl.BlockSpec((1, D), lambda b, c: (0, 0)),
                  pl.BlockSpec((2, 2 * H2, D), lambda b, c: (0, 0, 0))],
        out_specs=pl.BlockSpec((L, D), lambda b, c: (rowblk(b, c), 0)),
        out_shape=jax.ShapeDtypeStruct((M, D), BF16),
        compiler_params=_params(("parallel", "parallel"), blk, 8 * _nbytes((L, D), F32)),
        name="ssd_out",
    )(d_skip, xbc, xbc, xbc, zs, dtraw, dtrawT, prev_f, prev_b, bias, alog, biasT, alogT,
      norm_g.reshape(1, D), emat)


def _prepare_layer(norm_mix_g, w_in, conv_w, conv_b, dt_bias_f, dt_bias_b, a_log_f, a_log_b, d_skip,
                   ssd_norm_g, w_ssd_up, pool_w, pool_scale, w_pool_up, w_out, norm_mlp_g, w_ff1, w_ff2,
                   norm_ple_g, w_ple_gate, w_ple_proj):
    D = w_in.shape[0]
    H = d_skip.shape[0]
    C = conv_w.shape[1]
    s_xbc = D + C
    s_dt = s_xbc + 2 * H
    w_dt = w_in[:, s_xbc:s_dt].astype(BF16)
    head_of_lane = jnp.arange(D, dtype=jnp.int32) // (D // H)
    onehot = (jnp.arange(H, dtype=jnp.int32)[:, None] == head_of_lane[None, :]).astype(BF16)
    zeros = jnp.zeros_like(onehot)
    L = SSD_CHUNK
    bias = jnp.concatenate([dt_bias_f, dt_bias_b])
    alog = jnp.concatenate([a_log_f, a_log_b])
    return dict(
        D=D, H=H, C=C,
        norm_mix_g=norm_mix_g,
        w_in=w_in, s_dt=s_dt, w_z=w_in[:, :D].astype(BF16),
        w_dt=w_dt, w_dtT=w_dt.T,
        conv_w=conv_w, conv_b=conv_b,
        bias=bias.reshape(1, 2 * H), alog=alog.reshape(1, 2 * H),
        biasT=jnp.broadcast_to(bias[:, None], (2 * H, L)), alogT=jnp.broadcast_to(alog[:, None], (2 * H, L)),
        emat=jnp.stack([jnp.concatenate([onehot, zeros, onehot, zeros]),
                        jnp.concatenate([zeros, onehot, zeros, onehot])]),
        d_skip=d_skip, ssd_norm_g=ssd_norm_g,
        w_ssd_up=w_ssd_up, w_pool_up=w_pool_up, w_out=w_out, w_ff1=w_ff1, w_ff2=w_ff2, w_ple_gate=w_ple_gate,
        pool_w=pool_w.astype(BF16), pool_scale=pool_scale,
        norm_mlp_g=norm_mlp_g, norm_ple_g=norm_ple_g, w_ple_proj=w_ple_proj.astype(BF16),
    )


def _layer(x_parts, p_parts, lp, batch, seq):
    D, H = lp["D"], lp["H"]
    PD = lp["w_pool_up"].shape[0]
    off_ga, off_gb = PD, PD + D

    h, dtraw, dtrawT = _norm_dt(x_parts, lp["norm_mix_g"], lp["w_dt"], lp["w_dtT"])
    n_in = lp["w_in"].shape[1]
    zs, (w_rest,) = _matmul(
        h, lp["w_z"], BF16, "in_proj_z", epilogue=lambda acc: acc * _sigmoid(acc), tn=_TN_CARRIER,
        casts=((lp["w_in"], (D, n_in - D)),))
    tail, (w_ff1, w_out, w_pool_up) = _matmul(
        h, w_rest, BF16, "in_proj_tail", tn=_TN_WIDE, w_cols=(lp["s_dt"] - D, PD + 2 * D),
        casts=(lp["w_ff1"], lp["w_out"], lp["w_pool_up"]))

    xbc_raw, (w_ple_gate, w_ssd_up) = _matmul(h, w_rest, BF16, "in_proj_xbc", tn=_TN_WIDE, w_cols=(0, lp["C"]),
                                              casts=(lp["w_ple_gate"], lp["w_ssd_up"]))
    xbc = _conv_silu(xbc_raw, 0, lp["conv_w"], lp["conv_b"], seq)
    prev_f, prev_b = _ssd_states(xbc, dtraw, lp["bias"], lp["alog"], lp["emat"], batch, seq, D, H)
    y = _ssd_out(zs, xbc, dtraw, dtrawT, prev_f, prev_b, lp["bias"], lp["alog"], lp["biasT"],
                 lp["alogT"], lp["d_skip"], lp["ssd_norm_g"], lp["emat"], batch, seq, D, H)
    gated_a = _matmul(y, w_ssd_up, BF16, "ssd_up", tn=_TN_WIDE,
                      epilogue=lambda acc, g_ref: _sigmoid(g_ref[...].astype(F32)) * acc,
                      extras=(tail,), col_offsets=(off_ga,),
                      extra_specs=lambda tm, tn: [_tile_spec(off_ga)(tm, tn)])

    pooled = _pool(tail, 0, lp["pool_w"], lp["pool_scale"], seq)
    mixed = _matmul(pooled, w_pool_up, BF16, "pool_up", tn=_TN_WIDE,
                    epilogue=lambda acc, g_ref, a_ref: (_sigmoid(g_ref[...].astype(F32)) * acc
                                                        + a_ref[...].astype(F32)),
                    extras=(tail, gated_a), col_offsets=(off_gb,),
                    extra_specs=lambda tm, tn: [_tile_spec(off_gb)(tm, tn), _tile_spec(0)(tm, tn)])
    x, xg, ssq = _matmul_residual_norm(mixed, w_out, x_parts, lp["norm_mlp_g"], "out_proj")

    ff, (w_ff2,) = _matmul(xg, w_ff1, BF16, "ff1", ssq=ssq, tn=_TN_WIDE, casts=(lp["w_ff2"],),
                           epilogue=lambda acc: jnp.square(jnp.maximum(acc, 0.0)))
    x, xg, ssq = _matmul_kacc_residual_norm(ff, w_ff2, x, lp["norm_ple_g"], "ff2")

    ple = p_parts[0].shape[1]
    tm = _pick(functools.reduce(math.gcd, [p.shape[0] for p in p_parts]), (1024, 512, 256, 128))
    tiles = _part_tiles(p_parts, tm)

    def ple_epilogue(acc, x_ref, *refs):
        p = _part_read(refs[:-1], tiles, pl.program_id(0))
        return x_ref[...] + _sigmoid(acc) * _dot(p.astype(BF16), refs[-1][...])

    return _matmul(xg, w_ple_gate, F32, "ple", ssq=ssq, tm=tm, epilogue=ple_epilogue,
                   extras=(x, *p_parts, lp["w_ple_proj"]),
                   extra_specs=lambda tm, tn: ([_tile_spec(0)(tm, tn)] + _part_specs(p_parts, tm, ple)
                                               + [pl.BlockSpec((ple, tn), lambda i, j: (0, j))]))


def _run_groups(xs, ps, layers, norm_final_g):
    seq, D = xs[0].shape[1:]
    assert seq % SSD_CHUNK == 0
    batch = sum(x.shape[0] for x in xs)
    x_parts = [x.reshape(-1, D) for x in xs]
    for i, lp in enumerate(layers):
        p_parts = [p[i].reshape(-1, p.shape[-1]) for p in ps]
        x_parts = [_layer(x_parts, p_parts, lp, batch, seq)]
    outs = _norm_final(x_parts[0] if len(x_parts) == 1 else jnp.concatenate(x_parts), norm_final_g,
                       [x.shape[0] * seq for x in xs])
    return [o.reshape(x.shape) for o, x in zip(outs, xs)]


def kernel(x_prompt, x_sample, p_prompt, p_sample, norm_mix_g, w_in, conv_w, conv_b, dt_bias_f, dt_bias_b, a_log_f, a_log_b, d_skip, ssd_norm_g, w_ssd_up, pool_w, pool_scale, w_pool_up, w_out, norm_mlp_g, w_ff1, w_ff2, norm_ple_g, w_ple_gate, w_ple_proj, norm_final_g):
    stacked = (norm_mix_g, w_in, conv_w, conv_b, dt_bias_f, dt_bias_b, a_log_f, a_log_b, d_skip,
               ssd_norm_g, w_ssd_up, pool_w, pool_scale, w_pool_up, w_out, norm_mlp_g, w_ff1, w_ff2,
               norm_ple_g, w_ple_gate, w_ple_proj)
    layers = [_prepare_layer(*[w[i] for w in stacked]) for i in range(w_in.shape[0])]
    if x_prompt.shape[1] == x_sample.shape[1]:
        y_prompt, y_sample = _run_groups([x_prompt, x_sample], [p_prompt, p_sample], layers, norm_final_g)
    else:
        (y_prompt,) = _run_groups([x_prompt], [p_prompt], layers, norm_final_g)
        (y_sample,) = _run_groups([x_sample], [p_sample], layers, norm_final_g)
    return (y_prompt, y_sample)
```

```python
import functools
import math

import jax
import jax.numpy as jnp
from jax import lax
from jax.experimental import pallas as pl
from jax.experimental.pallas import tpu as pltpu

F32 = jnp.float32
BF16 = jnp.bfloat16

EPS = 1e-6
LOG2E = 1.4426950408889634
LN2 = 0.6931471805599453
SSD_GROUPS = 8
SSD_STATE = 128
SSD_CHUNK = 128
POOL_WINDOWS = (2, 4, 8, 16)

V7X_VMEM_BYTES = 64 * 1024 * 1024
V7X_VMEM_RESERVE_BYTES = 8 * 1024 * 1024
BF16_SUBLANE_TILE = 16
LANES = 128
HALO = BF16_SUBLANE_TILE
_TN_WIDE = 1024
_TN_F32_WEIGHT = 512
CONV_SUB_ROWS = 128
SSD_OUT_CHUNKS_PER_STEP = 2
CAST_BLOCK_BYTES = 2 * 1024 * 1024


def _pick(dim, prefs):
    for p in prefs:
        if dim % p == 0:
            return p
    return dim


def _nbytes(shape, dtype):
    n = 1
    for s in shape:
        n *= s
    return n * jnp.dtype(dtype).itemsize


def _params(semantics, block_bytes, scratch_bytes=0):
    est = 2 * block_bytes + scratch_bytes
    limit = min(2 * est + (4 << 20), V7X_VMEM_BYTES - V7X_VMEM_RESERVE_BYTES)
    return pltpu.CompilerParams(dimension_semantics=semantics, vmem_limit_bytes=int(limit))


def _sigmoid(v):
    return 1.0 / (1.0 + jnp.exp(-v))


def _softplus(v):
    u = jnp.exp(-jnp.abs(v))
    w = 1.0 + u
    return jnp.maximum(v, 0.0) + jnp.where(w == 1.0, u, (jnp.log2(w) * LN2) * (u / (w - 1.0)))


def _split3(v):
    a = v.astype(BF16)
    r = v - a.astype(F32)
    b = r.astype(BF16)
    c = (r - b.astype(F32)).astype(BF16)
    return a, b, c


def _split2(v):
    a = v.astype(BF16)
    b = (v - a.astype(F32)).astype(BF16)
    return a, b


def _dot(a, b):
    return jnp.dot(a, b, preferred_element_type=F32)


def _dot_nt(a, b):
    return lax.dot_general(a, b, (((1,), (1,)), ((), ())), preferred_element_type=F32)


def _dot_tn(a, b):
    return lax.dot_general(a, b, (((0,), (0,)), ((), ())), preferred_element_type=F32)


def _rms(x, g):
    ms = jnp.mean(x * x, axis=-1, keepdims=True)
    return x * lax.rsqrt(ms + EPS) * g


def _part_tiles(parts, tm):
    return [p.shape[0] // tm for p in parts]


def _part_specs(parts, tm, ncols, col_index=None):
    specs, start = [], 0
    for n in _part_tiles(parts, tm):
        def index(*idx, s=start, n=n):
            row = idx[0] - s
            if col_index is None:
                return jnp.clip(row, 0, n - 1), 0
            return jnp.clip(row, 0, n - 1), jnp.where((row >= 0) & (row < n), col_index(*idx), 0)
        specs.append(pl.BlockSpec((tm, ncols), index))
        start += n
    return specs


def _part_read(refs, tiles, i):
    val = refs[-1][...]
    end = sum(tiles[:-1])
    for ref, n in zip(reversed(refs[:-1]), reversed(tiles[:-1])):
        val = jnp.where(i < end, ref[...], val)
        end -= n
    return val


def _norm_dt_kernel(*refs, tiles):
    n = len(tiles)
    x_refs, (g_ref, wdt_ref, wdtT_ref, h_ref, dt_ref, dtT_ref) = refs[:n], refs[n:]
    h = _rms(_part_read(x_refs, tiles, pl.program_id(0)), g_ref[...]).astype(BF16)
    h_ref[...] = h
    dt_ref[...] = _dot(h, wdt_ref[...])
    dtT_ref[...] = _dot_nt(wdtT_ref[...], h)


def _norm_dt(x_parts, g, w_dt, w_dtT):
    D = x_parts[0].shape[1]
    M = sum(p.shape[0] for p in x_parts)
    H2 = w_dt.shape[1]
    tm = _pick(functools.reduce(math.gcd, [p.shape[0] for p in x_parts]), (512, 256, 128))
    blk = (len(x_parts) * _nbytes((tm, D), F32) + _nbytes((tm, D), BF16) + 2 * _nbytes((D, H2), BF16)
           + 2 * _nbytes((tm, H2), F32))
    return pl.pallas_call(
        functools.partial(_norm_dt_kernel, tiles=_part_tiles(x_parts, tm)),
        grid=(M // tm,),
        in_specs=_part_specs(x_parts, tm, D) + [
            pl.BlockSpec((1, D), lambda i: (0, 0)),
            pl.BlockSpec((D, H2), lambda i: (0, 0)),
            pl.BlockSpec((H2, D), lambda i: (0, 0))],
        out_specs=[pl.BlockSpec((tm, D), lambda i: (i, 0)),
                   pl.BlockSpec((tm, H2), lambda i: (i, 0)),
                   pl.BlockSpec((H2, tm), lambda i: (0, i))],
        out_shape=[jax.ShapeDtypeStruct((M, D), BF16),
                   jax.ShapeDtypeStruct((M, H2), F32),
                   jax.ShapeDtypeStruct((H2, M), F32)],
        compiler_params=_params(("arbitrary",), blk),
        name="norm_dt",
    )(*x_parts, g.reshape(1, D), w_dt, w_dtT)


def _norm_final_kernel(x_ref, g_ref, *o_refs, tiles):
    i = pl.program_id(0)
    y = _rms(x_ref[...], g_ref[...])
    start = 0
    for o_ref, n in zip(o_refs, tiles):
        @pl.when((i >= start) & (i < start + n))
        def _(o_ref=o_ref):
            o_ref[...] = y
        start += n


def _norm_final(x, g, part_rows):
    M, D = x.shape
    tm = _pick(functools.reduce(math.gcd, part_rows), (512, 256, 128))
    outs = [jax.ShapeDtypeStruct((r, D), F32) for r in part_rows]
    blk = (1 + len(part_rows)) * _nbytes((tm, D), F32)
    return pl.pallas_call(
        functools.partial(_norm_final_kernel, tiles=_part_tiles(outs, tm)),
        grid=(M // tm,),
        in_specs=[pl.BlockSpec((tm, D), lambda i: (i, 0)),
                  pl.BlockSpec((1, D), lambda i: (0, 0))],
        out_specs=_part_specs(outs, tm, D),
        out_shape=outs,
        compiler_params=_params(("arbitrary",), blk),
        name="norm_final",
    )(x, g.reshape(1, D))


def _row_rsqrt(ssq_ref, d):
    return lax.rsqrt(jnp.sum(ssq_ref[...], axis=-1, keepdims=True) * (1.0 / d) + EPS)


def _lane_partial_sumsq(v):
    sq = v * v
    part = sq[:, 0:LANES]
    for q in range(1, v.shape[1] // LANES):
        part = part + sq[:, q * LANES:(q + 1) * LANES]
    return part


def _mm_kernel(lhs_ref, w_ref, *rest, epilogue, norm_width, n_casts):
    n_in = len(rest) - 1 - 2 * n_casts
    ins, cast_srcs, out_ref, cast_dsts = rest[:n_in], rest[n_in:n_in + n_casts], rest[n_in + n_casts], rest[n_in + n_casts + 1:]
    acc = _dot(lhs_ref[...], w_ref[...].astype(lhs_ref.dtype))
    if norm_width:
        acc = acc * _row_rsqrt(ins[0], norm_width)
        ins = ins[1:]
    out_ref[...] = epilogue(acc, *ins).astype(out_ref.dtype)
    for src_ref, dst_ref in zip(cast_srcs, cast_dsts):
        dst_ref[...] = src_ref[...].astype(dst_ref.dtype)


def _plan_casts(items, steps):
    jobs, rest, used = [], [], 0
    for item in items:
        a, (c0, cw) = item if isinstance(item, tuple) else (item, (0, item.shape[1]))
        cols = _pick(math.gcd(cw, c0) if c0 else cw, (2048, 1024, 512, 256, 128))
        fit = [r for r in (128, 256, 512, 1024, 2048, 4096) if a.shape[0] % r == 0 and cw % cols == 0
               and c0 % cols == 0 and cols % LANES == 0
               and _nbytes((r, cols), F32) <= CAST_BLOCK_BYTES
               and used + (a.shape[0] // r) * (cw // cols) <= steps]
        if fit and not rest:
            jobs.append((a, c0, cw, (fit[0], cols)))
            used += (a.shape[0] // fit[0]) * (cw // cols)
        else:
            rest.append(a[:, c0:c0 + cw])
    return tuple(jobs), rest


def _cast_specs(jobs, n_i, n_j):
    src_specs, dst_specs, start = [], [], 0
    for a, c0, cw, block in jobs:
        n_rb, n_cb = a.shape[0] // block[0], cw // block[1]

        def index(i, j, start=start, n=n_rb * n_cb, n_cb=n_cb, cb0=0):
            b = jnp.clip(i * n_j + j - start, 0, n - 1)
            return b // n_cb, cb0 + b % n_cb
        src_specs.append(pl.BlockSpec(block, functools.partial(index, cb0=c0 // block[1])))
        dst_specs.append(pl.BlockSpec(block, index))
        start += n_rb * n_cb
    assert start <= n_i * n_j, "not enough grid steps to carry the weight casts"
    return src_specs, dst_specs


def _matmul(lhs, w, out_dtype, name, epilogue=None, extras=(), extra_specs=None, ssq=None, tm=None, tn=None,
            col_offsets=(), w_cols=None, casts=()):
    M, K = lhs.shape
    w_off, N = w_cols if w_cols else (0, w.shape[1])
    tm = tm or _pick(M, (1024, 512, 256, 128))
    tn = next(t for t in ((tn,) if tn else ()) + (512, 256, 128)
              if N % t == 0 and all(o % t == 0 for o in col_offsets))
    if epilogue is None:
        epilogue = lambda acc: acc
    specs = extra_specs(tm, tn) if extra_specs else []
    if ssq is not None:
        extras = (ssq,) + tuple(extras)
        specs = [pl.BlockSpec((tm, LANES), lambda i, j: (i, 0))] + specs
    cast_items = tuple(casts)
    jobs, uncast = _plan_casts(cast_items, (M // tm) * (N // tn))
    cast_src_specs, cast_dst_specs = _cast_specs(jobs, M // tm, N // tn)
    blk = (_nbytes((tm, K), lhs.dtype) + _nbytes((K, tn), w.dtype) + _nbytes((tm, tn), out_dtype)
           + sum(_nbytes(s.block_shape, e.dtype) for s, e in zip(specs, extras))
           + sum(_nbytes(job[3], F32) + _nbytes(job[3], BF16) for job in jobs)
           + (_nbytes((K, tn), BF16) if w.dtype != lhs.dtype else 0))
    if w_off % tn == 0:
        w_blk0 = w_off // tn
        w_spec = pl.BlockSpec((K, tn), lambda i, j: (0, w_blk0 + j))
    elif w_off % LANES == 0:
        w_spec = pl.BlockSpec((pl.Element(K), pl.Element(tn)),
                              lambda i, j: (0, pl.multiple_of(w_off + j * tn, LANES)))
    else:
        w = w[:, w_off:w_off + N]
        w_spec = pl.BlockSpec((K, tn), lambda i, j: (0, j))
    outs = pl.pallas_call(
        functools.partial(_mm_kernel, epilogue=epilogue, norm_width=K if ssq is not None else 0,
                          n_casts=len(jobs)),
        grid=(M // tm, N // tn),
        in_specs=[pl.BlockSpec((tm, K), lambda i, j: (i, 0)), w_spec] + specs + cast_src_specs,
        out_specs=[pl.BlockSpec((tm, tn), lambda i, j: (i, j))] + cast_dst_specs,
        out_shape=[jax.ShapeDtypeStruct((M, N), out_dtype)]
        + [jax.ShapeDtypeStruct((job[0].shape[0], job[2]), BF16) for job in jobs],
        compiler_params=_params(("arbitrary", "arbitrary") if jobs else ("parallel", "arbitrary"),
                                blk, _nbytes((tm, tn), F32)),
        name=name,
    )(lhs, w, *extras, *[job[0] for job in jobs])
    if not cast_items:
        return outs[0]
    return outs[0], tuple(outs[1:]) + tuple(a.astype(BF16) for a in uncast)


def _tile_spec(col_offset):
    def make(tm, tn):
        assert col_offset % tn == 0
        off = col_offset // tn
        return pl.BlockSpec((tm, tn), lambda i, j: (i, off + j))
    return make


def _emit_norm_operand(x_new, g_ref, xg_ref, ssq_ref, first):
    xg_ref[...] = (x_new * g_ref[...]).astype(xg_ref.dtype)
    part = _lane_partial_sumsq(x_new)

    @pl.when(first)
    def _():
        ssq_ref[...] = part

    @pl.when(jnp.logical_not(first))
    def _():
        ssq_ref[...] += part


def _mm_res_norm_kernel(lhs_ref, w_ref, *refs, tiles):
    n = len(tiles)
    res_refs, (g_ref, x_ref, xg_ref, ssq_ref) = refs[:n], refs[n:]
    x_new = _part_read(res_refs, tiles, pl.program_id(0)) + _dot(lhs_ref[...], w_ref[...])
    x_ref[...] = x_new
    _emit_norm_operand(x_new, g_ref, xg_ref, ssq_ref, pl.program_id(1) == 0)


def _matmul_residual_norm(lhs, w, res_parts, g_next, name):
    M, K = lhs.shape
    N = w.shape[1]
    tm = _pick(functools.reduce(math.gcd, [p.shape[0] for p in res_parts]), (1024, 512, 256, 128))
    tn = _pick(N, (512, 256, 128))
    blk = (_nbytes((tm, K), lhs.dtype) + _nbytes((K, tn), w.dtype) + (1 + len(res_parts)) * _nbytes((tm, tn), F32)
           + _nbytes((tm, tn), BF16) + _nbytes((tm, LANES), F32))
    return pl.pallas_call(
        functools.partial(_mm_res_norm_kernel, tiles=_part_tiles(res_parts, tm)),
        grid=(M // tm, N // tn),
        in_specs=[pl.BlockSpec((tm, K), lambda i, j: (i, 0)),
                  pl.BlockSpec((K, tn), lambda i, j: (0, j))]
        + _part_specs(res_parts, tm, tn, col_index=lambda i, j: j)
        + [pl.BlockSpec((1, tn), lambda i, j: (0, j))],
        out_specs=[pl.BlockSpec((tm, tn), lambda i, j: (i, j)),
                   pl.BlockSpec((tm, tn), lambda i, j: (i, j)),
                   pl.BlockSpec((tm, LANES), lambda i, j: (i, 0))],
        out_shape=[jax.ShapeDtypeStruct((M, N), F32),
                   jax.ShapeDtypeStruct((M, N), BF16),
                   jax.ShapeDtypeStruct((M, LANES), F32)],
        compiler_params=_params(("arbitrary", "arbitrary"), blk, _nbytes((tm, tn), F32)),
        name=name,
    )(lhs, w, *res_parts, g_next.reshape(1, N))


def _mm_kacc_kernel(lhs_ref, w_ref, res_ref, g_ref, x_ref, xg_ref, ssq_ref, *, nk):
    k = pl.program_id(2)

    def step(base_ref, finish):
        x_new = base_ref[...] + _dot(lhs_ref[...], w_ref[...])
        x_ref[...] = x_new
        if finish:
            _emit_norm_operand(x_new, g_ref, xg_ref, ssq_ref, pl.program_id(1) == 0)

    if nk == 1:
        step(res_ref, True)
        return
    pl.when(k == 0)(lambda: step(res_ref, False))
    if nk > 2:
        pl.when((k > 0) & (k < nk - 1))(lambda: step(x_ref, False))
    pl.when(k == nk - 1)(lambda: step(x_ref, True))


def _matmul_kacc_residual_norm(lhs, w, res, g_next, name):
    M, K = lhs.shape
    N = w.shape[1]
    tm = _pick(M, (1024, 512, 256, 128))
    tn = _pick(N, (1024, 512, 256, 128))
    tk = _pick(K, (2048, 1024, 512, 256, 128))
    blk = (_nbytes((tm, tk), lhs.dtype) + _nbytes((tk, tn), w.dtype) + 2 * _nbytes((tm, tn), F32)
           + _nbytes((tm, tn), BF16) + _nbytes((tm, LANES), F32))
    return pl.pallas_call(
        functools.partial(_mm_kacc_kernel, nk=K // tk),
        grid=(M // tm, N // tn, K // tk),
        in_specs=[pl.BlockSpec((tm, tk), lambda i, j, k: (i, k)),
                  pl.BlockSpec((tk, tn), lambda i, j, k: (k, j)),
                  pl.BlockSpec((tm, tn), lambda i, j, k: (i, j)),
                  pl.BlockSpec((1, tn), lambda i, j, k: (0, j))],
        out_specs=[pl.BlockSpec((tm, tn), lambda i, j, k: (i, j)),
                   pl.BlockSpec((tm, tn), lambda i, j, k: (i, j)),
                   pl.BlockSpec((tm, LANES), lambda i, j, k: (i, 0))],
        out_shape=[jax.ShapeDtypeStruct((M, N), F32),
                   jax.ShapeDtypeStruct((M, N), BF16),
                   jax.ShapeDtypeStruct((M, LANES), F32)],
        compiler_params=_params(("parallel", "arbitrary", "arbitrary"), blk, _nbytes((tm, tn), F32)),
        name=name,
    )(lhs, w, res, g_next.reshape(1, N))


def _conv_kernel(prev_ref, cur_ref, next_ref, w_ref, b_ref, shift_ref, out_ref, buf_ref, *, tm, seq_tiles, width):
    i = pl.program_id(0)
    first = (i % seq_tiles) == 0
    last = (i % seq_tiles) == seq_tiles - 1
    prev = prev_ref[...]
    nxt = next_ref[...]
    buf_ref[0:HALO, :] = jnp.where(first, jnp.zeros_like(prev), prev)
    buf_ref[HALO:HALO + tm, :] = cur_ref[...]
    buf_ref[HALO + tm:, :] = jnp.where(last, jnp.zeros_like(nxt), nxt)

    half = width // 2
    taps = [k for k in range(width) if k != half]
    sub = shift_ref.shape[0]
    win = sub + 2 * HALO
    shift = shift_ref[...]
    w_taps = [w_ref[k:k + 1, :].astype(BF16) for k in taps]
    for r0 in range(0, tm, sub):
        window = buf_ref[r0:r0 + win, :]
        scaled = jnp.concatenate([window * wk for wk in w_taps], axis=0)
        acc = (b_ref[...] + w_ref[half:half + 1, :] * window[HALO:HALO + sub].astype(F32)
               + _dot(shift, scaled))
        out_ref[r0:r0 + sub, :] = (acc * _sigmoid(acc)).astype(out_ref.dtype)


def _conv_silu(proj, col_offset, conv_w, conv_b, seq):
    M = proj.shape[0]
    width, C = conv_w.shape
    assert width // 2 <= HALO
    tm = _pick(seq, (1024, 512, 256, 128))
    tc = _pick(C, (2048, 1024, 512, 256, 128))
    while col_offset % tc:
        tc //= 2
    off = col_offset // tc
    hb = tm // HALO
    n_halo = M // HALO
    half = width // 2
    sub = min(tm, CONV_SUB_ROWS)
    win = sub + 2 * HALO
    rows = jnp.arange(sub, dtype=jnp.int32)[:, None]
    cols = jnp.arange(win, dtype=jnp.int32)[None, :]
    shift = jnp.concatenate([(cols == rows + HALO + (k - half)) for k in range(width) if k != half],
                            axis=1).astype(BF16)
    blk = (2 * _nbytes((tm, tc), BF16) + 2 * _nbytes((HALO, tc), BF16) + _nbytes((width + 1, tc), F32)
           + _nbytes(shift.shape, BF16))
    return pl.pallas_call(
        functools.partial(_conv_kernel, tm=tm, seq_tiles=seq // tm, width=width),
        grid=(M // tm, C // tc),
        in_specs=[pl.BlockSpec((HALO, tc), lambda i, j: (jnp.maximum(i * hb - 1, 0), off + j)),
                  pl.BlockSpec((tm, tc), lambda i, j: (i, off + j)),
                  pl.BlockSpec((HALO, tc), lambda i, j: (jnp.minimum((i + 1) * hb, n_halo - 1), off + j)),
                  pl.BlockSpec((width, tc), lambda i, j: (0, j)),
                  pl.BlockSpec((1, tc), lambda i, j: (0, j)),
                  pl.BlockSpec(shift.shape, lambda i, j: (0, 0))],
        out_specs=pl.BlockSpec((tm, tc), lambda i, j: (i, j)),
        out_shape=jax.ShapeDtypeStruct((M, C), BF16),
        scratch_shapes=[pltpu.VMEM((tm + 2 * HALO, tc), BF16)],
        compiler_params=_params(("parallel", "parallel"), blk,
                                _nbytes((tm + 2 * HALO, tc), BF16) + 8 * _nbytes((CONV_SUB_ROWS, tc), F32)),
        name="conv_silu",
    )(proj, proj, proj, conv_w, conv_b.reshape(1, C), shift)


def _pool_kernel(prev_ref, cur_ref, next_ref, pw_ref, ps_ref, out_ref, *, tm, seq, Wd):
    i = pl.program_id(0)
    t0 = (i * tm) % seq
    has_prev = t0 > 0
    has_next = t0 + tm < seq
    r = lax.broadcasted_iota(jnp.int32, (tm, tm), 0)
    c = lax.broadcasted_iota(jnp.int32, (tm, tm), 1)
    rh = lax.broadcasted_iota(jnp.int32, (HALO, HALO), 0)
    ch = lax.broadcasted_iota(jnp.int32, (HALO, HALO), 1)
    t = lax.broadcasted_iota(jnp.int32, (tm, Wd), 0) + t0
    for g, w in enumerate(POOL_WINDOWS):
        lo_off = w // 2
        hi_off = w - lo_off
        sl = slice(g * Wd, (g + 1) * Wd)
        band = jnp.where((c >= r - lo_off) & (c < r + hi_off), 1.0, 0.0).astype(BF16)
        band_prev = jnp.where(has_prev & (ch - HALO >= rh - lo_off), 1.0, 0.0).astype(BF16)
        band_next = jnp.where(has_next & (ch < rh - HALO + hi_off), 1.0, 0.0).astype(BF16)
        cur = cur_ref[:, sl]
        wsum = _dot(band, cur)
        wsum = jnp.concatenate([wsum[0:HALO] + _dot(band_prev, prev_ref[:, sl]),
                                wsum[HALO:tm - HALO],
                                wsum[tm - HALO:] + _dot(band_next, next_ref[:, sl])], axis=0)
        cnt = jnp.minimum(t + hi_off, seq) - jnp.maximum(t - lo_off, 0)
        pooled = (wsum / cnt.astype(F32) - cur.astype(F32)).astype(BF16)
        out_ref[:, sl] = (_dot(pooled, pw_ref[g]) * ps_ref[:, sl]).astype(out_ref.dtype)


def _pool(proj, col_offset, pool_w, pool_scale, seq):
    M = proj.shape[0]
    PG, Wd, _ = pool_w.shape
    PD = PG * Wd
    assert PG == len(POOL_WINDOWS) and max(POOL_WINDOWS) // 2 <= HALO
    assert col_offset % PD == 0
    off = col_offset // PD
    tm = _pick(seq, (256, 128))
    hb = tm // HALO
    n_halo = M // HALO
    blk = (2 * _nbytes((tm, PD), BF16) + 2 * _nbytes((HALO, PD), BF16) + _nbytes((PG, Wd, Wd), BF16))
    return pl.pallas_call(
        functools.partial(_pool_kernel, tm=tm, seq=seq, Wd=Wd),
        grid=(M // tm,),
        in_specs=[pl.BlockSpec((HALO, PD), lambda i: (jnp.maximum(i * hb - 1, 0), off)),
                  pl.BlockSpec((tm, PD), lambda i: (i, off)),
                  pl.BlockSpec((HALO, PD), lambda i: (jnp.minimum((i + 1) * hb, n_halo - 1), off)),
                  pl.BlockSpec((PG, Wd, Wd), lambda i: (0, 0, 0)),
                  pl.BlockSpec((1, PD), lambda i: (0, 0))],
        out_specs=pl.BlockSpec((tm, PD), lambda i: (i, 0)),
        out_shape=jax.ShapeDtypeStruct((M, PD), BF16),
        compiler_params=_params(("parallel",), blk, 4 * _nbytes((tm, PD), F32)),
        name="pool",
    )(proj, proj, proj, pool_w, pool_scale.reshape(1, PD))


def _cumsum_rows(tri, v):
    a, b, c = _split3(v)
    return _dot(tri, a) + _dot(tri, b) + _dot(tri, c)


def _cumsum_lanes(v, triT):
    a, b, c = _split3(v)
    return _dot(a, triT) + _dot(b, triT) + _dot(c, triT)


def _hilo(v):
    a, b = _split2(v)
    return jnp.concatenate([a, b], axis=1)


def _expand(v_hilo, e2):
    return _dot(v_hilo, e2)


def _ssd_state_kernel(xf_ref, bf_ref, dtf_ref, xb_ref, bb_ref, dtb_ref, bias_ref, alog_ref, e_ref,
                      pf_ref, pb_ref, h_ref, *, H, G, N, Q):
    L = SSD_CHUNK

    @pl.when(pl.program_id(1) == 0)
    def _():
        h_ref[...] = jnp.zeros_like(h_ref)

    row = lax.broadcasted_iota(jnp.int32, (L, L), 0)
    col = lax.broadcasted_iota(jnp.int32, (L, L), 1)
    tri = jnp.where(row >= col, 1.0, 0.0).astype(BF16)
    is_fwd = lax.broadcasted_iota(jnp.int32, (L, 2 * H), 1) < H
    A = -jnp.exp(alog_ref[...])

    def direction(d, x_ref, b_ref, dtraw_ref, p_ref):
        dt = _softplus(dtraw_ref[...] + bias_ref[...])
        a = dt * A
        cum = _cumsum_rows(tri, a)
        total = cum[L - 1:L, :]
        dte = jnp.exp(jnp.where(is_fwd, total - cum, cum - a))
        factors = jnp.concatenate([dt * dte, jnp.broadcast_to(jnp.exp(total), (8, 2 * H))], axis=0)
        expanded = _expand(_hilo(factors), e_ref[d])
        xs = (x_ref[...].astype(F32) * expanded[0:L]).astype(BF16)
        cdexp = expanded[L:L + 1, :]
        bm = b_ref[...]
        for g in range(G):
            st = _dot_tn(bm[:, g * N:(g + 1) * N], xs[:, g * Q:(g + 1) * Q])
            h_old = h_ref[d, g]
            p_ref[0, 0, g] = h_old.astype(BF16)
            h_ref[d, g] = cdexp[:, g * Q:(g + 1) * Q] * h_old + st

    direction(0, xf_ref, bf_ref, dtf_ref, pf_ref)
    direction(1, xb_ref, bb_ref, dtb_ref, pb_ref)


def _ssd_states(xbc, dtraw, bias, alog, emat, batch, seq, D, H):
    L, G, N = SSD_CHUNK, SSD_GROUPS, SSD_STATE
    GN = G * N
    Q = D // G
    NC = seq // L
    assert D % GN == 0 or GN % D == 0
    b_off = D // GN
    H2 = 2 * H
    blk = 2 * (_nbytes((L, D), BF16) + _nbytes((L, GN), BF16) + _nbytes((L, H2), F32)
               + _nbytes((G, N, Q), BF16)) + _nbytes((2, 2 * H2, D), BF16)
    fwd = lambda b, t: b * NC + t
    bwd = lambda b, t: b * NC + (NC - 1 - t)
    out_sds = jax.ShapeDtypeStruct((batch, NC, G, N, Q), BF16)
    return pl.pallas_call(
        functools.partial(_ssd_state_kernel, H=H, G=G, N=N, Q=Q),
        grid=(batch, NC),
        in_specs=[pl.BlockSpec((L, D), lambda b, t: (fwd(b, t), 0)),
                  pl.BlockSpec((L, GN), lambda b, t: (fwd(b, t), b_off)),
                  pl.BlockSpec((L, H2), lambda b, t: (fwd(b, t), 0)),
                  pl.BlockSpec((L, D), lambda b, t: (bwd(b, t), 0)),
                  pl.BlockSpec((L, GN), lambda b, t: (bwd(b, t), b_off)),
                  pl.BlockSpec((L, H2), lambda b, t: (bwd(b, t), 0)),
                  pl.BlockSpec((1, H2), lambda b, t: (0, 0)),
                  pl.BlockSpec((1, H2), lambda b, t: (0, 0)),
                  pl.BlockSpec((2, 2 * H2, D), lambda b, t: (0, 0, 0))],
        out_specs=[pl.BlockSpec((1, 1, G, N, Q), lambda b, t: (b, t, 0, 0, 0)),
                   pl.BlockSpec((1, 1, G, N, Q), lambda b, t: (b, NC - 1 - t, 0, 0, 0))],
        out_shape=[out_sds, out_sds],
        scratch_shapes=[pltpu.VMEM((2, G, N, Q), F32)],
        compiler_params=_params(("parallel", "arbitrary"), blk,
                                _nbytes((2, G, N, Q), F32) + 4 * _nbytes((L, D), F32)),
        name="ssd_states",
    )(xbc, xbc, dtraw, xbc, xbc, dtraw, bias, alog, emat)


def _ssd_out_kernel(dsk_ref, x_ref, b_ref, c_ref, zs_ref, dt_ref, dtT_ref, pf_ref, pb_ref,
                    bias_ref, alog_ref, biasT_ref, alogT_ref, ng_ref, e_ref, out_ref,
                    *, H, G, N, Q, P):
    L = SSD_CHUNK
    E = H // G
    row = lax.broadcasted_iota(jnp.int32, (L, L), 0)
    col = lax.broadcasted_iota(jnp.int32, (L, L), 1)
    lower = row >= col
    eye = row == col
    tri = jnp.where(lower, 1.0, 0.0).astype(BF16)
    triT = jnp.where(row <= col, 1.0, 0.0).astype(BF16)
    is_fwd = lax.broadcasted_iota(jnp.int32, (L, 2 * H), 1) < H
    is_fwdT = lax.broadcasted_iota(jnp.int32, (2 * H, L), 0) < H
    lane = lax.broadcasted_iota(jnp.int32, (L, 2 * P), 1)
    keep_left = jnp.where(lane < P, 1.0, 0.0).astype(BF16)
    keep_right = jnp.where(lane < P, 0.0, 1.0).astype(BF16)
    masks = (lower, eye, tri, triT, is_fwd, is_fwdT, keep_left, keep_right)
    for sc in range(x_ref.shape[0] // L):
        r = slice(sc * L, (sc + 1) * L)
        _ssd_out_chunk(masks, dsk_ref, x_ref.at[r], b_ref.at[r], c_ref.at[r], zs_ref.at[r], dt_ref.at[r],
                       dtT_ref.at[:, r], pf_ref.at[0, sc], pb_ref.at[0, sc], bias_ref, alog_ref, biasT_ref,
                       alogT_ref, ng_ref, e_ref, out_ref.at[r], H=H, G=G, N=N, Q=Q, P=P)


def _ssd_out_chunk(masks, dsk_ref, x_ref, b_ref, c_ref, zs_ref, dt_ref, dtT_ref, pf_ref, pb_ref,
                   bias_ref, alog_ref, biasT_ref, alogT_ref, ng_ref, e_ref, out_ref, *, H, G, N, Q, P):
    L = SSD_CHUNK
    E = H // G
    lower, eye, tri, triT, is_fwd, is_fwdT, keep_left, keep_right = masks

    dt = _softplus(dt_ref[...] + bias_ref[...])
    a = dt * (-jnp.exp(alog_ref[...]))
    cum = _cumsum_rows(tri, a)
    total = cum[L - 1:L, :]
    excl = cum - a
    col2 = jnp.where(is_fwd, cum, excl) * LOG2E
    sdec = _hilo(jnp.exp(jnp.where(is_fwd, cum, total - excl)))

    dtT = _softplus(dtT_ref[...] + biasT_ref[...])
    aT = dtT * (-jnp.exp(alogT_ref[...]))
    cumT = _cumsum_lanes(aT, triT)
    lg = jnp.log2(dtT)
    row2 = jnp.where(is_fwdT, cumT * LOG2E - lg, (cumT - aT) * LOG2E + lg)

    for g in range(G):
        c_g = c_ref[:, g * N:(g + 1) * N]
        cb = _dot_nt(c_g, b_ref[:, g * N:(g + 1) * N])
        cb_diag = jnp.sum(jnp.where(eye, cb, 0.0), axis=0, keepdims=True)
        pieces = []
        for e2 in range(E // 2):
            ws = []
            for h in (g * E + 2 * e2, g * E + 2 * e2 + 1):
                arg = jnp.where(lower,
                                col2[:, h:h + 1] - row2[h:h + 1, :],
                                row2[H + h:H + h + 1, :] - col2[:, H + h:H + h + 1])
                diag = cb_diag * dtT[H + h:H + h + 1, :] + dsk_ref[h]
                ws.append((cb * jnp.exp2(arg) + jnp.where(eye, diag, 0.0)).astype(BF16))
            h0 = g * E + 2 * e2
            xp = x_ref[:, h0 * P:(h0 + 2) * P]
            rhs = jnp.concatenate([xp * keep_left, xp * keep_right], axis=0)
            pieces.append(_dot(jnp.concatenate(ws, axis=1), rhs))
        y = jnp.concatenate(pieces, axis=1) if len(pieces) > 1 else pieces[0]
        sl = slice(g * Q, (g + 1) * Q)
        y = (y + _dot(c_g, pf_ref[g]) * _expand(sdec, e_ref[0, :, sl])
             + _dot(c_g, pb_ref[g]) * _expand(sdec, e_ref[1, :, sl]))
        y = y * zs_ref[:, sl].astype(F32)
        ms = jnp.mean(y * y, axis=-1, keepdims=True)
        out_ref[:, sl] = (y * lax.rsqrt(ms + EPS) * ng_ref[:, sl]).astype(out_ref.dtype)


def _ssd_out(zs, xbc, dtraw, dtrawT, prev_f, prev_b, bias, alog, biasT, alogT, d_skip, norm_g, emat,
             batch, seq, D, H):
    L, G, N = SSD_CHUNK, SSD_GROUPS, SSD_STATE
    GN = G * N
    Q = D // G
    P = D // H
    NC = seq // L
    M = batch * seq
    H2 = 2 * H
    assert (H // G) % 2 == 0 and 2 * P == 128 and D % GN == 0
    b_off = D // GN
    cps = SSD_OUT_CHUNKS_PER_STEP if NC % SSD_OUT_CHUNKS_PER_STEP == 0 else 1
    R = cps * L
    steps = NC // cps
    blk = (2 * _nbytes((R, D), BF16) + 2 * _nbytes((R, GN), BF16) + 2 * _nbytes((R, H2), F32)
           + 2 * _nbytes((cps, G, N, Q), BF16) + _nbytes((2, 2 * H2, D), BF16) + _nbytes((R, D), BF16))
    rowblk = lambda b, c: b * steps + c
    return pl.pallas_call(
        functools.partial(_ssd_out_kernel, H=H, G=G, N=N, Q=Q, P=P),
        grid=(batch, steps),
        in_specs=[pl.BlockSpec(memory_space=pltpu.SMEM),
                  pl.BlockSpec((R, D), lambda b, c: (rowblk(b, c), 0)),
                  pl.BlockSpec((R, GN), lambda b, c: (rowblk(b, c), b_off)),
                  pl.BlockSpec((R, GN), lambda b, c: (rowblk(b, c), b_off + 1)),
                  pl.BlockSpec((R, D), lambda b, c: (rowblk(b, c), 0)),
                  pl.BlockSpec((R, H2), lambda b, c: (rowblk(b, c), 0)),
                  pl.BlockSpec((H2, R), lambda b, c: (0, rowblk(b, c))),
                  pl.BlockSpec((1, cps, G, N, Q), lambda b, c: (b, c, 0, 0, 0)),
                  pl.BlockSpec((1, cps, G, N, Q), lambda b, c: (b, c, 0, 0, 0)),
                  pl.BlockSpec((1, H2), lambda b, c: (0, 0)),
                  pl.BlockSpec((1, H2), lambda b, c: (0, 0)),
                  pl.BlockSpec((H2, L), lambda b, c: (0, 0)),
                  pl.BlockSpec((H2, L), lambda b, c: (0, 0)),
                  pl.BlockSpec((1, D), lambda b, c: (0, 0)),
                  pl.BlockSpec((2, 2 * H2, D), lambda b, c: (0, 0, 0))],
        out_specs=pl.BlockSpec((R, D), lambda b, c: (rowblk(b, c), 0)),
        out_shape=jax.ShapeDtypeStruct((M, D), BF16),
        compiler_params=_params(("parallel", "parallel"), blk, 8 * _nbytes((L, D), F32)),
        name="ssd_out",
    )(d_skip, xbc, xbc, xbc, zs, dtraw, dtrawT, prev_f, prev_b, bias, alog, biasT, alogT,
      norm_g.reshape(1, D), emat)


def _prepare_layer(norm_mix_g, w_in, conv_w, conv_b, dt_bias_f, dt_bias_b, a_log_f, a_log_b, d_skip,
                   ssd_norm_g, w_ssd_up, pool_w, pool_scale, w_pool_up, w_out, norm_mlp_g, w_ff1, w_ff2,
                   norm_ple_g, w_ple_gate, w_ple_proj):
    D = w_in.shape[0]
    H = d_skip.shape[0]
    C = conv_w.shape[1]
    s_xbc = D + C
    s_dt = s_xbc + 2 * H
    w_dt = w_in[:, s_xbc:s_dt].astype(BF16)
    head_of_lane = jnp.arange(D, dtype=jnp.int32) // (D // H)
    onehot = (jnp.arange(H, dtype=jnp.int32)[:, None] == head_of_lane[None, :]).astype(BF16)
    zeros = jnp.zeros_like(onehot)
    L = SSD_CHUNK
    bias = jnp.concatenate([dt_bias_f, dt_bias_b])
    alog = jnp.concatenate([a_log_f, a_log_b])
    return dict(
        D=D, H=H, C=C,
        norm_mix_g=norm_mix_g,
        w_in=w_in, s_dt=s_dt,
        w_dt=w_dt, w_dtT=w_dt.T,
        conv_w=conv_w, conv_b=conv_b,
        bias=bias.reshape(1, 2 * H), alog=alog.reshape(1, 2 * H),
        biasT=jnp.broadcast_to(bias[:, None], (2 * H, L)), alogT=jnp.broadcast_to(alog[:, None], (2 * H, L)),
        emat=jnp.stack([jnp.concatenate([onehot, zeros, onehot, zeros]),
                        jnp.concatenate([zeros, onehot, zeros, onehot])]),
        d_skip=d_skip, ssd_norm_g=ssd_norm_g,
        w_ssd_up=w_ssd_up, w_pool_up=w_pool_up, w_out=w_out, w_ff1=w_ff1, w_ff2=w_ff2, w_ple_gate=w_ple_gate,
        pool_w=pool_w.astype(BF16), pool_scale=pool_scale,
        norm_mlp_g=norm_mlp_g, norm_ple_g=norm_ple_g, w_ple_proj=w_ple_proj.astype(BF16),
    )


def _layer(x_parts, p_parts, lp, batch, seq):
    D, H = lp["D"], lp["H"]
    PD = lp["w_pool_up"].shape[0]
    off_ga, off_gb = PD, PD + D

    h, dtraw, dtrawT = _norm_dt(x_parts, lp["norm_mix_g"], lp["w_dt"], lp["w_dtT"])
    n_in = lp["w_in"].shape[1]
    zs, (w_rest,) = _matmul(
        h, lp["w_in"], BF16, "in_proj_z", epilogue=lambda acc: acc * _sigmoid(acc), tn=_TN_F32_WEIGHT,
        w_cols=(0, D), casts=((lp["w_in"], (D, n_in - D)),))
    tail, (w_ff1, w_out, w_pool_up) = _matmul(
        h, w_rest, BF16, "in_proj_tail", tn=_TN_WIDE, w_cols=(lp["s_dt"] - D, PD + 2 * D),
        casts=(lp["w_ff1"], lp["w_out"], lp["w_pool_up"]))

    xbc_raw, (w_ple_gate, w_ssd_up) = _matmul(h, w_rest, BF16, "in_proj_xbc", tn=_TN_WIDE, w_cols=(0, lp["C"]),
                                              casts=(lp["w_ple_gate"], lp["w_ssd_up"]))
    xbc = _conv_silu(xbc_raw, 0, lp["conv_w"], lp["conv_b"], seq)
    prev_f, prev_b = _ssd_states(xbc, dtraw, lp["bias"], lp["alog"], lp["emat"], batch, seq, D, H)
    y = _ssd_out(zs, xbc, dtraw, dtrawT, prev_f, prev_b, lp["bias"], lp["alog"], lp["biasT"],
                 lp["alogT"], lp["d_skip"], lp["ssd_norm_g"], lp["emat"], batch, seq, D, H)
    gated_a = _matmul(y, w_ssd_up, BF16, "ssd_up", tn=_TN_WIDE,
                      epilogue=lambda acc, g_ref: _sigmoid(g_ref[...].astype(F32)) * acc,
                      extras=(tail,), col_offsets=(off_ga,),
                      extra_specs=lambda tm, tn: [_tile_spec(off_ga)(tm, tn)])

    pooled = _pool(tail, 0, lp["pool_w"], lp["pool_scale"], seq)
    mixed = _matmul(pooled, w_pool_up, BF16, "pool_up", tn=_TN_WIDE,
                    epilogue=lambda acc, g_ref, a_ref: (_sigmoid(g_ref[...].astype(F32)) * acc
                                                        + a_ref[...].astype(F32)),
                    extras=(tail, gated_a), col_offsets=(off_gb,),
                    extra_specs=lambda tm, tn: [_tile_spec(off_gb)(tm, tn), _tile_spec(0)(tm, tn)])
    x, xg, ssq = _matmul_residual_norm(mixed, w_out, x_parts, lp["norm_mlp_g"], "out_proj")

    ff, (w_ff2,) = _matmul(xg, w_ff1, BF16, "ff1", ssq=ssq, tn=_TN_WIDE, casts=(lp["w_ff2"],),
                           epilogue=lambda acc: jnp.square(jnp.maximum(acc, 0.0)))
    x, xg, ssq = _matmul_kacc_residual_norm(ff, w_ff2, x, lp["norm_ple_g"], "ff2")

    ple = p_parts[0].shape[1]
    tm = _pick(functools.reduce(math.gcd, [p.shape[0] for p in p_parts]), (1024, 512, 256, 128))
    tiles = _part_tiles(p_parts, tm)

    def ple_epilogue(acc, x_ref, *refs):
        p = _part_read(refs[:-1], tiles, pl.program_id(0))
        return x_ref[...] + _sigmoid(acc) * _dot(p.astype(BF16), refs[-1][...])

    return _matmul(xg, w_ple_gate, F32, "ple", ssq=ssq, tm=tm, epilogue=ple_epilogue,
                   extras=(x, *p_parts, lp["w_ple_proj"]),
                   extra_specs=lambda tm, tn: ([_tile_spec(0)(tm, tn)] + _part_specs(p_parts, tm, ple)
                                               + [pl.BlockSpec((ple, tn), lambda i, j: (0, j))]))


def _run_groups(xs, ps, layers, norm_final_g):
    seq, D = xs[0].shape[1:]
    assert seq % SSD_CHUNK == 0
    batch = sum(x.shape[0] for x in xs)
    x_parts = [x.reshape(-1, D) for x in xs]
    for i, lp in enumerate(layers):
        p_parts = [p[i].reshape(-1, p.shape[-1]) for p in ps]
        x_parts = [_layer(x_parts, p_parts, lp, batch, seq)]
    outs = _norm_final(x_parts[0] if len(x_parts) == 1 else jnp.concatenate(x_parts), norm_final_g,
                       [x.shape[0] * seq for x in xs])
    return [o.reshape(x.shape) for o, x in zip(outs, xs)]


def kernel(x_prompt, x_sample, p_prompt, p_sample, norm_mix_g, w_in, conv_w, conv_b, dt_bias_f, dt_bias_b, a_log_f, a_log_b, d_skip, ssd_norm_g, w_ssd_up, pool_w, pool_scale, w_pool_up, w_out, norm_mlp_g, w_ff1, w_ff2, norm_ple_g, w_ple_gate, w_ple_proj, norm_final_g):
    stacked = (norm_mix_g, w_in, conv_w, conv_b, dt_bias_f, dt_bias_b, a_log_f, a_log_b, d_skip,
               ssd_norm_g, w_ssd_up, pool_w, pool_scale, w_pool_up, w_out, norm_mlp_g, w_ff1, w_ff2,
               norm_ple_g, w_ple_gate, w_ple_proj)
    layers = [_prepare_layer(*[w[i] for w in stacked]) for i in range(w_in.shape[0])]
    if x_prompt.shape[1] == x_sample.shape[1]:
        y_prompt, y_sample = _run_groups([x_prompt, x_sample], [p_prompt, p_sample], layers, norm_final_g)
    else:
        (y_prompt,) = _run_groups([x_prompt], [p_prompt], layers, norm_final_g)
        (y_sample,) = _run_groups([x_sample], [p_sample], layers, norm_final_g)
    return (y_prompt, y_sample)
```

```python
import functools
import math

import jax
import jax.numpy as jnp
from jax import lax
from jax.experimental import pallas as pl
from jax.experimental.pallas import tpu as pltpu

F32 = jnp.float32
BF16 = jnp.bfloat16

EPS = 1e-6
LOG2E = 1.4426950408889634
LN2 = 0.6931471805599453
SSD_GROUPS = 8
SSD_STATE = 128
SSD_CHUNK = 128
POOL_WINDOWS = (2, 4, 8, 16)

V7X_VMEM_BYTES = 64 * 1024 * 1024
V7X_VMEM_RESERVE_BYTES = 8 * 1024 * 1024
BF16_SUBLANE_TILE = 16
LANES = 128
HALO = BF16_SUBLANE_TILE
_TN_WIDE = 1024
_TN_F32_WEIGHT = 512
CONV_SUB_ROWS = 128
CAST_BLOCK_BYTES = 2 * 1024 * 1024


def _pick(dim, prefs):
    for p in prefs:
        if dim % p == 0:
            return p
    return dim


def _nbytes(shape, dtype):
    n = 1
    for s in shape:
        n *= s
    return n * jnp.dtype(dtype).itemsize


def _params(semantics, block_bytes, scratch_bytes=0):
    est = 2 * block_bytes + scratch_bytes
    limit = min(2 * est + (4 << 20), V7X_VMEM_BYTES - V7X_VMEM_RESERVE_BYTES)
    return pltpu.CompilerParams(dimension_semantics=semantics, vmem_limit_bytes=int(limit))


def _sigmoid(v):
    return 1.0 / (1.0 + jnp.exp(-v))


def _softplus(v):
    u = jnp.exp(-jnp.abs(v))
    w = 1.0 + u
    return jnp.maximum(v, 0.0) + jnp.where(w == 1.0, u, (jnp.log2(w) * LN2) * (u / (w - 1.0)))


def _split3(v):
    a = v.astype(BF16)
    r = v - a.astype(F32)
    b = r.astype(BF16)
    c = (r - b.astype(F32)).astype(BF16)
    return a, b, c


def _split2(v):
    a = v.astype(BF16)
    b = (v - a.astype(F32)).astype(BF16)
    return a, b


def _dot(a, b):
    return jnp.dot(a, b, preferred_element_type=F32)


def _dot_nt(a, b):
    return lax.dot_general(a, b, (((1,), (1,)), ((), ())), preferred_element_type=F32)


def _dot_tn(a, b):
    return lax.dot_general(a, b, (((0,), (0,)), ((), ())), preferred_element_type=F32)


def _rms(x, g):
    ms = jnp.mean(x * x, axis=-1, keepdims=True)
    return x * lax.rsqrt(ms + EPS) * g


def _part_tiles(parts, tm):
    return [p.shape[0] // tm for p in parts]


def _part_specs(parts, tm, ncols, col_index=None):
    specs, start = [], 0
    for n in _part_tiles(parts, tm):
        def index(*idx, s=start, n=n):
            row = idx[0] - s
            if col_index is None:
                return jnp.clip(row, 0, n - 1), 0
            return jnp.clip(row, 0, n - 1), jnp.where((row >= 0) & (row < n), col_index(*idx), 0)
        specs.append(pl.BlockSpec((tm, ncols), index))
        start += n
    return specs


def _part_read(refs, tiles, i):
    val = refs[-1][...]
    end = sum(tiles[:-1])
    for ref, n in zip(reversed(refs[:-1]), reversed(tiles[:-1])):
        val = jnp.where(i < end, ref[...], val)
        end -= n
    return val


def _norm_dt_kernel(*refs, tiles):
    n = len(tiles)
    x_refs, (g_ref, wdt_ref, wdtT_ref, h_ref, dt_ref, dtT_ref) = refs[:n], refs[n:]
    h = _rms(_part_read(x_refs, tiles, pl.program_id(0)), g_ref[...]).astype(BF16)
    h_ref[...] = h
    dt_ref[...] = _dot(h, wdt_ref[...])
    dtT_ref[...] = _dot_nt(wdtT_ref[...], h)


def _norm_dt(x_parts, g, w_dt, w_dtT):
    D = x_parts[0].shape[1]
    M = sum(p.shape[0] for p in x_parts)
    H2 = w_dt.shape[1]
    tm = _pick(functools.reduce(math.gcd, [p.shape[0] for p in x_parts]), (512, 256, 128))
    blk = (len(x_parts) * _nbytes((tm, D), F32) + _nbytes((tm, D), BF16) + 2 * _nbytes((D, H2), BF16)
           + 2 * _nbytes((tm, H2), F32))
    return pl.pallas_call(
        functools.partial(_norm_dt_kernel, tiles=_part_tiles(x_parts, tm)),
        grid=(M // tm,),
        in_specs=_part_specs(x_parts, tm, D) + [
            pl.BlockSpec((1, D), lambda i: (0, 0)),
            pl.BlockSpec((D, H2), lambda i: (0, 0)),
            pl.BlockSpec((H2, D), lambda i: (0, 0))],
        out_specs=[pl.BlockSpec((tm, D), lambda i: (i, 0)),
                   pl.BlockSpec((tm, H2), lambda i: (i, 0)),
                   pl.BlockSpec((H2, tm), lambda i: (0, i))],
        out_shape=[jax.ShapeDtypeStruct((M, D), BF16),
                   jax.ShapeDtypeStruct((M, H2), F32),
                   jax.ShapeDtypeStruct((H2, M), F32)],
        compiler_params=_params(("arbitrary",), blk),
        name="norm_dt",
    )(*x_parts, g.reshape(1, D), w_dt, w_dtT)


def _norm_final_kernel(x_ref, g_ref, *o_refs, tiles):
    i = pl.program_id(0)
    y = _rms(x_ref[...], g_ref[...])
    start = 0
    for o_ref, n in zip(o_refs, tiles):
        @pl.when((i >= start) & (i < start + n))
        def _(o_ref=o_ref):
            o_ref[...] = y
        start += n


def _norm_final(x, g, part_rows):
    M, D = x.shape
    tm = _pick(functools.reduce(math.gcd, part_rows), (512, 256, 128))
    outs = [jax.ShapeDtypeStruct((r, D), F32) for r in part_rows]
    blk = (1 + len(part_rows)) * _nbytes((tm, D), F32)
    return pl.pallas_call(
        functools.partial(_norm_final_kernel, tiles=_part_tiles(outs, tm)),
        grid=(M // tm,),
        in_specs=[pl.BlockSpec((tm, D), lambda i: (i, 0)),
                  pl.BlockSpec((1, D), lambda i: (0, 0))],
        out_specs=_part_specs(outs, tm, D),
        out_shape=outs,
        compiler_params=_params(("arbitrary",), blk),
        name="norm_final",
    )(x, g.reshape(1, D))


def _row_rsqrt(ssq_ref, d):
    return lax.rsqrt(jnp.sum(ssq_ref[...], axis=-1, keepdims=True) * (1.0 / d) + EPS)


def _lane_partial_sumsq(v):
    sq = v * v
    part = sq[:, 0:LANES]
    for q in range(1, v.shape[1] // LANES):
        part = part + sq[:, q * LANES:(q + 1) * LANES]
    return part


def _mm_kernel(lhs_ref, w_ref, *rest, epilogue, norm_width, n_casts):
    n_in = len(rest) - 1 - 2 * n_casts
    ins, cast_srcs, out_ref, cast_dsts = rest[:n_in], rest[n_in:n_in + n_casts], rest[n_in + n_casts], rest[n_in + n_casts + 1:]
    acc = _dot(lhs_ref[...], w_ref[...].astype(lhs_ref.dtype))
    if norm_width:
        acc = acc * _row_rsqrt(ins[0], norm_width)
        ins = ins[1:]
    out_ref[...] = epilogue(acc, *ins).astype(out_ref.dtype)
    for src_ref, dst_ref in zip(cast_srcs, cast_dsts):
        dst_ref[...] = src_ref[...].astype(dst_ref.dtype)


def _plan_casts(items, steps):
    jobs, rest, used = [], [], 0
    for item in items:
        a, (c0, cw) = item if isinstance(item, tuple) else (item, (0, item.shape[1]))
        cols = _pick(math.gcd(cw, c0) if c0 else cw, (2048, 1024, 512, 256, 128))
        fit = [r for r in (128, 256, 512, 1024, 2048, 4096) if a.shape[0] % r == 0 and cw % cols == 0
               and c0 % cols == 0 and cols % LANES == 0
               and _nbytes((r, cols), F32) <= CAST_BLOCK_BYTES
               and used + (a.shape[0] // r) * (cw // cols) <= steps]
        if fit and not rest:
            jobs.append((a, c0, cw, (fit[0], cols)))
            used += (a.shape[0] // fit[0]) * (cw // cols)
        else:
            rest.append(a[:, c0:c0 + cw])
    return tuple(jobs), rest


def _cast_specs(jobs, n_i, n_j):
    src_specs, dst_specs, start = [], [], 0
    for a, c0, cw, block in jobs:
        n_rb, n_cb = a.shape[0] // block[0], cw // block[1]

        def index(i, j, start=start, n=n_rb * n_cb, n_cb=n_cb, cb0=0):
            b = jnp.clip(i * n_j + j - start, 0, n - 1)
            return b // n_cb, cb0 + b % n_cb
        src_specs.append(pl.BlockSpec(block, functools.partial(index, cb0=c0 // block[1])))
        dst_specs.append(pl.BlockSpec(block, index))
        start += n_rb * n_cb
    assert start <= n_i * n_j, "not enough grid steps to carry the weight casts"
    return src_specs, dst_specs


def _matmul(lhs, w, out_dtype, name, epilogue=None, extras=(), extra_specs=None, ssq=None, tm=None, tn=None,
            col_offsets=(), w_cols=None, casts=()):
    M, K = lhs.shape
    w_off, N = w_cols if w_cols else (0, w.shape[1])
    tm = tm or _pick(M, (1024, 512, 256, 128))
    tn = next(t for t in ((tn,) if tn else ()) + (512, 256, 128)
              if N % t == 0 and all(o % t == 0 for o in col_offsets))
    if epilogue is None:
        epilogue = lambda acc: acc
    specs = extra_specs(tm, tn) if extra_specs else []
    if ssq is not None:
        extras = (ssq,) + tuple(extras)
        specs = [pl.BlockSpec((tm, LANES), lambda i, j: (i, 0))] + specs
    cast_items = tuple(casts)
    jobs, uncast = _plan_casts(cast_items, (M // tm) * (N // tn))
    cast_src_specs, cast_dst_specs = _cast_specs(jobs, M // tm, N // tn)
    blk = (_nbytes((tm, K), lhs.dtype) + _nbytes((K, tn), w.dtype) + _nbytes((tm, tn), out_dtype)
           + sum(_nbytes(s.block_shape, e.dtype) for s, e in zip(specs, extras))
           + sum(_nbytes(job[3], F32) + _nbytes(job[3], BF16) for job in jobs)
           + (_nbytes((K, tn), BF16) if w.dtype != lhs.dtype else 0))
    if w_off % tn == 0:
        w_blk0 = w_off // tn
        w_spec = pl.BlockSpec((K, tn), lambda i, j: (0, w_blk0 + j))
    elif w_off % LANES == 0:
        w_spec = pl.BlockSpec((pl.Element(K), pl.Element(tn)),
                              lambda i, j: (0, pl.multiple_of(w_off + j * tn, LANES)))
    else:
        w = w[:, w_off:w_off + N]
        w_spec = pl.BlockSpec((K, tn), lambda i, j: (0, j))
    outs = pl.pallas_call(
        functools.partial(_mm_kernel, epilogue=epilogue, norm_width=K if ssq is not None else 0,
                          n_casts=len(jobs)),
        grid=(M // tm, N // tn),
        in_specs=[pl.BlockSpec((tm, K), lambda i, j: (i, 0)), w_spec] + specs + cast_src_specs,
        out_specs=[pl.BlockSpec((tm, tn), lambda i, j: (i, j))] + cast_dst_specs,
        out_shape=[jax.ShapeDtypeStruct((M, N), out_dtype)]
        + [jax.ShapeDtypeStruct((job[0].shape[0], job[2]), BF16) for job in jobs],
        compiler_params=_params(("arbitrary", "arbitrary") if jobs else ("parallel", "arbitrary"),
                                blk, _nbytes((tm, tn), F32)),
        name=name,
    )(lhs, w, *extras, *[job[0] for job in jobs])
    if not cast_items:
        return outs[0]
    return outs[0], tuple(outs[1:]) + tuple(a.astype(BF16) for a in uncast)


def _ple_final_kernel(lhs_ref, w_ref, ssq_ref, x_ref, *refs, tiles, n_row, width):
    n = len(tiles)
    p_refs, (wp_ref, g_ref), o_refs, (buf_ref, sq_ref) = refs[:n], refs[n:n + 2], refs[n + 2:2 * n + 2], refs[2 * n + 2:]
    i = pl.program_id(0)
    j = pl.program_id(1)
    slot = i % 2

    @pl.when(i < n_row)
    def _():
        acc = _dot(lhs_ref[...], w_ref[...]) * _row_rsqrt(ssq_ref, lhs_ref.shape[1])
        p = _part_read(p_refs, tiles, i)
        x3 = x_ref[...] + _sigmoid(acc) * _dot(p.astype(BF16), wp_ref[...])
        buf_ref[slot, j] = x3
        part = _lane_partial_sumsq(x3)
        sq_ref[slot] = jnp.where(j == 0, part, sq_ref[slot] + part)

    start = 0
    for o_ref, nt in zip(o_refs, tiles):
        @pl.when((i - 1 >= start) & (i - 1 < start + nt))
        def _(o_ref=o_ref):
            prev = 1 - slot
            r = lax.rsqrt(jnp.sum(sq_ref[prev], axis=-1, keepdims=True) * (1.0 / width) + EPS)
            o_ref[...] = buf_ref[prev, j] * r * g_ref[...]
        start += nt


def _ple_final(xg, w, ssq, x, p_parts, wp, g_final, name):
    M, K = xg.shape
    N = w.shape[1]
    ple = p_parts[0].shape[1]
    tm = _pick(functools.reduce(math.gcd, [p.shape[0] for p in p_parts]), (512, 256, 128))
    tn = _pick(N, (512, 256, 128))
    n_row, n_col = M // tm, N // tn
    tiles = _part_tiles(p_parts, tm)
    last = n_row - 1
    outs = [jax.ShapeDtypeStruct((p.shape[0], N), F32) for p in p_parts]
    out_specs, start = [], 0
    for nt in tiles:
        def index(i, j, s=start, nt=nt):
            row = i - 1 - s
            return jnp.clip(row, 0, nt - 1), jnp.where(row < 0, 0, jnp.where(row >= nt, n_col - 1, j))
        out_specs.append(pl.BlockSpec((tm, tn), index))
        start += nt
    blk = (_nbytes((tm, K), BF16) + _nbytes((K, tn), BF16) + (1 + len(p_parts)) * _nbytes((tm, tn), F32)
           + len(p_parts) * _nbytes((tm, ple), F32) + _nbytes((ple, tn), BF16) + _nbytes((tm, LANES), F32))
    row = lambda i: jnp.minimum(i, last)
    return pl.pallas_call(
        functools.partial(_ple_final_kernel, tiles=tiles, n_row=n_row, width=N),
        grid=(n_row + 1, n_col),
        in_specs=[pl.BlockSpec((tm, K), lambda i, j: (row(i), 0)),
                  pl.BlockSpec((K, tn), lambda i, j: (0, j)),
                  pl.BlockSpec((tm, LANES), lambda i, j: (row(i), 0)),
                  pl.BlockSpec((tm, tn), lambda i, j: (row(i), j))]
        + _part_specs(p_parts, tm, ple)
        + [pl.BlockSpec((ple, tn), lambda i, j: (0, j)),
           pl.BlockSpec((1, tn), lambda i, j: (0, j))],
        out_specs=out_specs,
        out_shape=outs,
        scratch_shapes=[pltpu.VMEM((2, n_col, tm, tn), F32), pltpu.VMEM((2, tm, LANES), F32)],
        compiler_params=_params(("arbitrary", "arbitrary"), blk,
                                _nbytes((2, n_col, tm, tn), F32) + _nbytes((tm, tn), F32)),
        name=name,
    )(xg, w, ssq, x, *p_parts, wp, g_final.reshape(1, N))


def _tile_spec(col_offset):
    def make(tm, tn):
        assert col_offset % tn == 0
        off = col_offset // tn
        return pl.BlockSpec((tm, tn), lambda i, j: (i, off + j))
    return make


def _emit_norm_operand(x_new, g_ref, xg_ref, ssq_ref, first):
    xg_ref[...] = (x_new * g_ref[...]).astype(xg_ref.dtype)
    part = _lane_partial_sumsq(x_new)

    @pl.when(first)
    def _():
        ssq_ref[...] = part

    @pl.when(jnp.logical_not(first))
    def _():
        ssq_ref[...] += part


def _mm_res_norm_kernel(lhs_ref, w_ref, *refs, tiles):
    n = len(tiles)
    res_refs, (g_ref, x_ref, xg_ref, ssq_ref) = refs[:n], refs[n:]
    x_new = _part_read(res_refs, tiles, pl.program_id(0)) + _dot(lhs_ref[...], w_ref[...])
    x_ref[...] = x_new
    _emit_norm_operand(x_new, g_ref, xg_ref, ssq_ref, pl.program_id(1) == 0)


def _matmul_residual_norm(lhs, w, res_parts, g_next, name):
    M, K = lhs.shape
    N = w.shape[1]
    tm = _pick(functools.reduce(math.gcd, [p.shape[0] for p in res_parts]), (1024, 512, 256, 128))
    tn = _pick(N, (512, 256, 128))
    blk = (_nbytes((tm, K), lhs.dtype) + _nbytes((K, tn), w.dtype) + (1 + len(res_parts)) * _nbytes((tm, tn), F32)
           + _nbytes((tm, tn), BF16) + _nbytes((tm, LANES), F32))
    return pl.pallas_call(
        functools.partial(_mm_res_norm_kernel, tiles=_part_tiles(res_parts, tm)),
        grid=(M // tm, N // tn),
        in_specs=[pl.BlockSpec((tm, K), lambda i, j: (i, 0)),
                  pl.BlockSpec((K, tn), lambda i, j: (0, j))]
        + _part_specs(res_parts, tm, tn, col_index=lambda i, j: j)
        + [pl.BlockSpec((1, tn), lambda i, j: (0, j))],
        out_specs=[pl.BlockSpec((tm, tn), lambda i, j: (i, j)),
                   pl.BlockSpec((tm, tn), lambda i, j: (i, j)),
                   pl.BlockSpec((tm, LANES), lambda i, j: (i, 0))],
        out_shape=[jax.ShapeDtypeStruct((M, N), F32),
                   jax.ShapeDtypeStruct((M, N), BF16),
                   jax.ShapeDtypeStruct((M, LANES), F32)],
        compiler_params=_params(("arbitrary", "arbitrary"), blk, _nbytes((tm, tn), F32)),
        name=name,
    )(lhs, w, *res_parts, g_next.reshape(1, N))


def _mm_kacc_kernel(lhs_ref, w_ref, res_ref, g_ref, x_ref, xg_ref, ssq_ref, *, nk):
    k = pl.program_id(2)

    def step(base_ref, finish):
        x_new = base_ref[...] + _dot(lhs_ref[...], w_ref[...])
        x_ref[...] = x_new
        if finish:
            _emit_norm_operand(x_new, g_ref, xg_ref, ssq_ref, pl.program_id(1) == 0)

    if nk == 1:
        step(res_ref, True)
        return
    pl.when(k == 0)(lambda: step(res_ref, False))
    if nk > 2:
        pl.when((k > 0) & (k < nk - 1))(lambda: step(x_ref, False))
    pl.when(k == nk - 1)(lambda: step(x_ref, True))


def _matmul_kacc_residual_norm(lhs, w, res, g_next, name):
    M, K = lhs.shape
    N = w.shape[1]
    tm = _pick(M, (1024, 512, 256, 128))
    tn = _pick(N, (1024, 512, 256, 128))
    tk = _pick(K, (2048, 1024, 512, 256, 128))
    blk = (_nbytes((tm, tk), lhs.dtype) + _nbytes((tk, tn), w.dtype) + 2 * _nbytes((tm, tn), F32)
           + _nbytes((tm, tn), BF16) + _nbytes((tm, LANES), F32))
    return pl.pallas_call(
        functools.partial(_mm_kacc_kernel, nk=K // tk),
        grid=(M // tm, N // tn, K // tk),
        in_specs=[pl.BlockSpec((tm, tk), lambda i, j, k: (i, k)),
                  pl.BlockSpec((tk, tn), lambda i, j, k: (k, j)),
                  pl.BlockSpec((tm, tn), lambda i, j, k: (i, j)),
                  pl.BlockSpec((1, tn), lambda i, j, k: (0, j))],
        out_specs=[pl.BlockSpec((tm, tn), lambda i, j, k: (i, j)),
                   pl.BlockSpec((tm, tn), lambda i, j, k: (i, j)),
                   pl.BlockSpec((tm, LANES), lambda i, j, k: (i, 0))],
        out_shape=[jax.ShapeDtypeStruct((M, N), F32),
                   jax.ShapeDtypeStruct((M, N), BF16),
                   jax.ShapeDtypeStruct((M, LANES), F32)],
        compiler_params=_params(("parallel", "arbitrary", "arbitrary"), blk, _nbytes((tm, tn), F32)),
        name=name,
    )(lhs, w, res, g_next.reshape(1, N))


def _conv_kernel(prev_ref, cur_ref, next_ref, w_ref, b_ref, shift_ref, out_ref, buf_ref, *, tm, seq_tiles, width):
    i = pl.program_id(0)
    first = (i % seq_tiles) == 0
    last = (i % seq_tiles) == seq_tiles - 1
    prev = prev_ref[...]
    nxt = next_ref[...]
    buf_ref[0:HALO, :] = jnp.where(first, jnp.zeros_like(prev), prev)
    buf_ref[HALO:HALO + tm, :] = cur_ref[...]
    buf_ref[HALO + tm:, :] = jnp.where(last, jnp.zeros_like(nxt), nxt)

    half = width // 2
    taps = [k for k in range(width) if k != half]
    sub = shift_ref.shape[0]
    win = sub + 2 * HALO
    shift = shift_ref[...]
    w_taps = [w_ref[k:k + 1, :].astype(BF16) for k in taps]
    for r0 in range(0, tm, sub):
        window = buf_ref[r0:r0 + win, :]
        scaled = jnp.concatenate([window * wk for wk in w_taps], axis=0)
        acc = (b_ref[...] + w_ref[half:half + 1, :] * window[HALO:HALO + sub].astype(F32)
               + _dot(shift, scaled))
        out_ref[r0:r0 + sub, :] = (acc * _sigmoid(acc)).astype(out_ref.dtype)


def _conv_silu(proj, col_offset, conv_w, conv_b, seq):
    M = proj.shape[0]
    width, C = conv_w.shape
    assert width // 2 <= HALO
    tm = _pick(seq, (512, 256, 128))
    tc = _pick(C, (2048, 1024, 512, 256, 128))
    while col_offset % tc:
        tc //= 2
    off = col_offset // tc
    hb = tm // HALO
    n_halo = M // HALO
    half = width // 2
    sub = min(tm, CONV_SUB_ROWS)
    win = sub + 2 * HALO
    rows = jnp.arange(sub, dtype=jnp.int32)[:, None]
    cols = jnp.arange(win, dtype=jnp.int32)[None, :]
    shift = jnp.concatenate([(cols == rows + HALO + (k - half)) for k in range(width) if k != half],
                            axis=1).astype(BF16)
    blk = (2 * _nbytes((tm, tc), BF16) + 2 * _nbytes((HALO, tc), BF16) + _nbytes((width + 1, tc), F32)
           + _nbytes(shift.shape, BF16))
    return pl.pallas_call(
        functools.partial(_conv_kernel, tm=tm, seq_tiles=seq // tm, width=width),
        grid=(M // tm, C // tc),
        in_specs=[pl.BlockSpec((HALO, tc), lambda i, j: (jnp.maximum(i * hb - 1, 0), off + j)),
                  pl.BlockSpec((tm, tc), lambda i, j: (i, off + j)),
                  pl.BlockSpec((HALO, tc), lambda i, j: (jnp.minimum((i + 1) * hb, n_halo - 1), off + j)),
                  pl.BlockSpec((width, tc), lambda i, j: (0, j)),
                  pl.BlockSpec((1, tc), lambda i, j: (0, j)),
                  pl.BlockSpec(shift.shape, lambda i, j: (0, 0))],
        out_specs=pl.BlockSpec((tm, tc), lambda i, j: (i, j)),
        out_shape=jax.ShapeDtypeStruct((M, C), BF16),
        scratch_shapes=[pltpu.VMEM((tm + 2 * HALO, tc), BF16)],
        compiler_params=_params(("parallel", "parallel"), blk,
                                _nbytes((tm + 2 * HALO, tc), BF16) + 8 * _nbytes((CONV_SUB_ROWS, tc), F32)),
        name="conv_silu",
    )(proj, proj, proj, conv_w, conv_b.reshape(1, C), shift)


def _pool_kernel(prev_ref, cur_ref, next_ref, pw_ref, ps_ref, out_ref, *, tm, seq, Wd):
    i = pl.program_id(0)
    t0 = (i * tm) % seq
    has_prev = t0 > 0
    has_next = t0 + tm < seq
    r = lax.broadcasted_iota(jnp.int32, (tm, tm), 0)
    c = lax.broadcasted_iota(jnp.int32, (tm, tm), 1)
    rh = lax.broadcasted_iota(jnp.int32, (HALO, HALO), 0)
    ch = lax.broadcasted_iota(jnp.int32, (HALO, HALO), 1)
    t = lax.broadcasted_iota(jnp.int32, (tm, Wd), 0) + t0
    for g, w in enumerate(POOL_WINDOWS):
        lo_off = w // 2
        hi_off = w - lo_off
        sl = slice(g * Wd, (g + 1) * Wd)
        band = jnp.where((c >= r - lo_off) & (c < r + hi_off), 1.0, 0.0).astype(BF16)
        band_prev = jnp.where(has_prev & (ch - HALO >= rh - lo_off), 1.0, 0.0).astype(BF16)
        band_next = jnp.where(has_next & (ch < rh - HALO + hi_off), 1.0, 0.0).astype(BF16)
        cur = cur_ref[:, sl]
        wsum = _dot(band, cur)
        wsum = jnp.concatenate([wsum[0:HALO] + _dot(band_prev, prev_ref[:, sl]),
                                wsum[HALO:tm - HALO],
                                wsum[tm - HALO:] + _dot(band_next, next_ref[:, sl])], axis=0)
        cnt = jnp.minimum(t + hi_off, seq) - jnp.maximum(t - lo_off, 0)
        pooled = (wsum / cnt.astype(F32) - cur.astype(F32)).astype(BF16)
        out_ref[:, sl] = (_dot(pooled, pw_ref[g]) * ps_ref[:, sl]).astype(out_ref.dtype)


def _pool(proj, col_offset, pool_w, pool_scale, seq):
    M = proj.shape[0]
    PG, Wd, _ = pool_w.shape
    PD = PG * Wd
    assert PG == len(POOL_WINDOWS) and max(POOL_WINDOWS) // 2 <= HALO
    assert col_offset % PD == 0
    off = col_offset // PD
    tm = _pick(seq, (256, 128))
    hb = tm // HALO
    n_halo = M // HALO
    blk = (2 * _nbytes((tm, PD), BF16) + 2 * _nbytes((HALO, PD), BF16) + _nbytes((PG, Wd, Wd), BF16))
    return pl.pallas_call(
        functools.partial(_pool_kernel, tm=tm, seq=seq, Wd=Wd),
        grid=(M // tm,),
        in_specs=[pl.BlockSpec((HALO, PD), lambda i: (jnp.maximum(i * hb - 1, 0), off)),
                  pl.BlockSpec((tm, PD), lambda i: (i, off)),
                  pl.BlockSpec((HALO, PD), lambda i: (jnp.minimum((i + 1) * hb, n_halo - 1), off)),
                  pl.BlockSpec((PG, Wd, Wd), lambda i: (0, 0, 0)),
                  pl.BlockSpec((1, PD), lambda i: (0, 0))],
        out_specs=pl.BlockSpec((tm, PD), lambda i: (i, 0)),
        out_shape=jax.ShapeDtypeStruct((M, PD), BF16),
        compiler_params=_params(("parallel",), blk, 4 * _nbytes((tm, PD), F32)),
        name="pool",
    )(proj, proj, proj, pool_w, pool_scale.reshape(1, PD))


def _cumsum_rows(tri, v):
    a, b, c = _split3(v)
    return _dot(tri, a) + _dot(tri, b) + _dot(tri, c)


def _cumsum_lanes(v, triT):
    a, b, c = _split3(v)
    return _dot(a, triT) + _dot(b, triT) + _dot(c, triT)


def _hilo(v):
    a, b = _split2(v)
    return jnp.concatenate([a, b], axis=1)


def _expand(v_hilo, e2):
    return _dot(v_hilo, e2)


def _ssd_state_kernel(xf_ref, bf_ref, dtf_ref, xb_ref, bb_ref, dtb_ref, bias_ref, alog_ref, e_ref,
                      pf_ref, pb_ref, h_ref, *, H, G, N, Q):
    L = SSD_CHUNK

    @pl.when(pl.program_id(1) == 0)
    def _():
        h_ref[...] = jnp.zeros_like(h_ref)

    row = lax.broadcasted_iota(jnp.int32, (L, L), 0)
    col = lax.broadcasted_iota(jnp.int32, (L, L), 1)
    tri = jnp.where(row >= col, 1.0, 0.0).astype(BF16)
    is_fwd = lax.broadcasted_iota(jnp.int32, (L, 2 * H), 1) < H
    A = -jnp.exp(alog_ref[...])

    def direction(d, x_ref, b_ref, dtraw_ref, p_ref):
        dt = _softplus(dtraw_ref[...] + bias_ref[...])
        a = dt * A
        cum = _cumsum_rows(tri, a)
        total = cum[L - 1:L, :]
        dte = jnp.exp(jnp.where(is_fwd, total - cum, cum - a))
        factors = jnp.concatenate([dt * dte, jnp.broadcast_to(jnp.exp(total), (8, 2 * H))], axis=0)
        expanded = _expand(_hilo(factors), e_ref[d])
        xs = (x_ref[...].astype(F32) * expanded[0:L]).astype(BF16)
        cdexp = expanded[L:L + 1, :]
        bm = b_ref[...]
        for g in range(G):
            st = _dot_tn(bm[:, g * N:(g + 1) * N], xs[:, g * Q:(g + 1) * Q])
            h_old = h_ref[d, g]
            p_ref[0, 0, g] = h_old.astype(BF16)
            h_ref[d, g] = cdexp[:, g * Q:(g + 1) * Q] * h_old + st

    direction(0, xf_ref, bf_ref, dtf_ref, pf_ref)
    direction(1, xb_ref, bb_ref, dtb_ref, pb_ref)


def _ssd_states(xbc, dtraw, bias, alog, emat, batch, seq, D, H):
    L, G, N = SSD_CHUNK, SSD_GROUPS, SSD_STATE
    GN = G * N
    Q = D // G
    NC = seq // L
    assert D % GN == 0 or GN % D == 0
    b_off = D // GN
    H2 = 2 * H
    blk = 2 * (_nbytes((L, D), BF16) + _nbytes((L, GN), BF16) + _nbytes((L, H2), F32)
               + _nbytes((G, N, Q), BF16)) + _nbytes((2, 2 * H2, D), BF16)
    fwd = lambda b, t: b * NC + t
    bwd = lambda b, t: b * NC + (NC - 1 - t)
    out_sds = jax.ShapeDtypeStruct((batch, NC, G, N, Q), BF16)
    return pl.pallas_call(
        functools.partial(_ssd_state_kernel, H=H, G=G, N=N, Q=Q),
        grid=(batch, NC),
        in_specs=[pl.BlockSpec((L, D), lambda b, t: (fwd(b, t), 0)),
                  pl.BlockSpec((L, GN), lambda b, t: (fwd(b, t), b_off)),
                  pl.BlockSpec((L, H2), lambda b, t: (fwd(b, t), 0)),
                  pl.BlockSpec((L, D), lambda b, t: (bwd(b, t), 0)),
                  pl.BlockSpec((L, GN), lambda b, t: (bwd(b, t), b_off)),
                  pl.BlockSpec((L, H2), lambda b, t: (bwd(b, t), 0)),
                  pl.BlockSpec((1, H2), lambda b, t: (0, 0)),
                  pl.BlockSpec((1, H2), lambda b, t: (0, 0)),
                  pl.BlockSpec((2, 2 * H2, D), lambda b, t: (0, 0, 0))],
        out_specs=[pl.BlockSpec((1, 1, G, N, Q), lambda b, t: (b, t, 0, 0, 0)),
                   pl.BlockSpec((1, 1, G, N, Q), lambda b, t: (b, NC - 1 - t, 0, 0, 0))],
        out_shape=[out_sds, out_sds],
        scratch_shapes=[pltpu.VMEM((2, G, N, Q), F32)],
        compiler_params=_params(("parallel", "arbitrary"), blk,
                                _nbytes((2, G, N, Q), F32) + 4 * _nbytes((L, D), F32)),
        name="ssd_states",
    )(xbc, xbc, dtraw, xbc, xbc, dtraw, bias, alog, emat)


def _ssd_out_kernel(dsk_ref, x_ref, b_ref, c_ref, zs_ref, dt_ref, dtT_ref, pf_ref, pb_ref,
                    bias_ref, alog_ref, biasT_ref, alogT_ref, ng_ref, e_ref, out_ref,
                    *, H, G, N, Q, P):
    L = SSD_CHUNK
    E = H // G
    row = lax.broadcasted_iota(jnp.int32, (L, L), 0)
    col = lax.broadcasted_iota(jnp.int32, (L, L), 1)
    lower = row >= col
    eye = row == col
    tri = jnp.where(lower, 1.0, 0.0).astype(BF16)
    triT = jnp.where(row <= col, 1.0, 0.0).astype(BF16)
    is_fwd = lax.broadcasted_iota(jnp.int32, (L, 2 * H), 1) < H
    is_fwdT = lax.broadcasted_iota(jnp.int32, (2 * H, L), 0) < H

    dt = _softplus(dt_ref[...] + bias_ref[...])
    a = dt * (-jnp.exp(alog_ref[...]))
    cum = _cumsum_rows(tri, a)
    total = cum[L - 1:L, :]
    excl = cum - a
    col2 = jnp.where(is_fwd, cum, excl) * LOG2E
    sdec = _hilo(jnp.exp(jnp.where(is_fwd, cum, total - excl)))

    dtT = _softplus(dtT_ref[...] + biasT_ref[...])
    aT = dtT * (-jnp.exp(alogT_ref[...]))
    cumT = _cumsum_lanes(aT, triT)
    lg = jnp.log2(dtT)
    row2 = jnp.where(is_fwdT, cumT * LOG2E - lg, (cumT - aT) * LOG2E + lg)

    lane = lax.broadcasted_iota(jnp.int32, (L, 2 * P), 1)
    keep_left = jnp.where(lane < P, 1.0, 0.0).astype(BF16)
    keep_right = jnp.where(lane < P, 0.0, 1.0).astype(BF16)

    for g in range(G):
        c_g = c_ref[:, g * N:(g + 1) * N]
        cb = _dot_nt(c_g, b_ref[:, g * N:(g + 1) * N])
        cb_diag = jnp.sum(jnp.where(eye, cb, 0.0), axis=0, keepdims=True)
        pieces = []
        for e2 in range(E // 2):
            ws = []
            for h in (g * E + 2 * e2, g * E + 2 * e2 + 1):
                arg = jnp.where(lower,
                                col2[:, h:h + 1] - row2[h:h + 1, :],
                                row2[H + h:H + h + 1, :] - col2[:, H + h:H + h + 1])
                diag = cb_diag * dtT[H + h:H + h + 1, :] + dsk_ref[h]
                ws.append((cb * jnp.exp2(arg) + jnp.where(eye, diag, 0.0)).astype(BF16))
            h0 = g * E + 2 * e2
            xp = x_ref[:, h0 * P:(h0 + 2) * P]
            rhs = jnp.concatenate([xp * keep_left, xp * keep_right], axis=0)
            pieces.append(_dot(jnp.concatenate(ws, axis=1), rhs))
        y = jnp.concatenate(pieces, axis=1) if len(pieces) > 1 else pieces[0]
        sl = slice(g * Q, (g + 1) * Q)
        y = (y + _dot(c_g, pf_ref[0, 0, g]) * _expand(sdec, e_ref[0, :, sl])
             + _dot(c_g, pb_ref[0, 0, g]) * _expand(sdec, e_ref[1, :, sl]))
        y = y * zs_ref[:, sl].astype(F32)
        ms = jnp.mean(y * y, axis=-1, keepdims=True)
        out_ref[:, sl] = (y * lax.rsqrt(ms + EPS) * ng_ref[:, sl]).astype(out_ref.dtype)


def _ssd_out(zs, xbc, dtraw, dtrawT, prev_f, prev_b, bias, alog, biasT, alogT, d_skip, norm_g, emat,
             batch, seq, D, H):
    L, G, N = SSD_CHUNK, SSD_GROUPS, SSD_STATE
    GN = G * N
    Q = D // G
    P = D // H
    NC = seq // L
    M = batch * seq
    H2 = 2 * H
    assert (H // G) % 2 == 0 and 2 * P == 128 and D % GN == 0
    b_off = D // GN
    blk = (2 * _nbytes((L, D), BF16) + 2 * _nbytes((L, GN), BF16) + 2 * _nbytes((L, H2), F32)
           + 2 * _nbytes((G, N, Q), BF16) + _nbytes((2, 2 * H2, D), BF16) + _nbytes((L, D), BF16))
    rowblk = lambda b, c: b * NC + c
    return pl.pallas_call(
        functools.partial(_ssd_out_kernel, H=H, G=G, N=N, Q=Q, P=P),
        grid=(batch, NC),
        in_specs=[pl.BlockSpec(memory_space=pltpu.SMEM),
                  pl.BlockSpec((L, D), lambda b, c: (rowblk(b, c), 0)),
                  pl.BlockSpec((L, GN), lambda b, c: (rowblk(b, c), b_off)),
                  pl.BlockSpec((L, GN), lambda b, c: (rowblk(b, c), b_off + 1)),
                  pl.BlockSpec((L, D), lambda b, c: (rowblk(b, c), 0)),
                  pl.BlockSpec((L, H2), lambda b, c: (rowblk(b, c), 0)),
                  pl.BlockSpec((H2, L), lambda b, c: (0, rowblk(b, c))),
                  pl.BlockSpec((1, 1, G, N, Q), lambda b, c: (b, c, 0, 0, 0)),
                  pl.BlockSpec((1, 1, G, N, Q), lambda b, c: (b, c, 0, 0, 0)),
                  pl.BlockSpec((1, H2), lambda b, c: (0, 0)),
                  pl.BlockSpec((1, H2), lambda b, c: (0, 0)),
                  pl.BlockSpec((H2, L), lambda b, c: (0, 0)),
                  pl.BlockSpec((H2, L), lambda b, c: (0, 0)),
                  pl.BlockSpec((1, D), lambda b, c: (0, 0)),
                  pl.BlockSpec((2, 2 * H2, D), lambda b, c: (0, 0, 0))],
        out_specs=pl.BlockSpec((L, D), lambda b, c: (rowblk(b, c), 0)),
        out_shape=jax.ShapeDtypeStruct((M, D), BF16),
        compiler_params=_params(("parallel", "parallel"), blk, 8 * _nbytes((L, D), F32)),
        name="ssd_out",
    )(d_skip, xbc, xbc, xbc, zs, dtraw, dtrawT, prev_f, prev_b, bias, alog, biasT, alogT,
      norm_g.reshape(1, D), emat)


def _prepare_layer(norm_mix_g, w_in, conv_w, conv_b, dt_bias_f, dt_bias_b, a_log_f, a_log_b, d_skip,
                   ssd_norm_g, w_ssd_up, pool_w, pool_scale, w_pool_up, w_out, norm_mlp_g, w_ff1, w_ff2,
                   norm_ple_g, w_ple_gate, w_ple_proj):
    D = w_in.shape[0]
    H = d_skip.shape[0]
    C = conv_w.shape[1]
    s_xbc = D + C
    s_dt = s_xbc + 2 * H
    w_dt = w_in[:, s_xbc:s_dt].astype(BF16)
    head_of_lane = jnp.arange(D, dtype=jnp.int32) // (D // H)
    onehot = (jnp.arange(H, dtype=jnp.int32)[:, None] == head_of_lane[None, :]).astype(BF16)
    zeros = jnp.zeros_like(onehot)
    L = SSD_CHUNK
    bias = jnp.concatenate([dt_bias_f, dt_bias_b])
    alog = jnp.concatenate([a_log_f, a_log_b])
    return dict(
        D=D, H=H, C=C,
        norm_mix_g=norm_mix_g,
        w_in=w_in, s_dt=s_dt,
        w_dt=w_dt, w_dtT=w_dt.T,
        conv_w=conv_w, conv_b=conv_b,
        bias=bias.reshape(1, 2 * H), alog=alog.reshape(1, 2 * H),
        biasT=jnp.broadcast_to(bias[:, None], (2 * H, L)), alogT=jnp.broadcast_to(alog[:, None], (2 * H, L)),
        emat=jnp.stack([jnp.concatenate([onehot, zeros, onehot, zeros]),
                        jnp.concatenate([zeros, onehot, zeros, onehot])]),
        d_skip=d_skip, ssd_norm_g=ssd_norm_g,
        w_ssd_up=w_ssd_up, w_pool_up=w_pool_up, w_out=w_out, w_ff1=w_ff1, w_ff2=w_ff2, w_ple_gate=w_ple_gate,
        pool_w=pool_w.astype(BF16), pool_scale=pool_scale,
        norm_mlp_g=norm_mlp_g, norm_ple_g=norm_ple_g, w_ple_proj=w_ple_proj.astype(BF16),
    )


def _layer(x_parts, p_parts, lp, batch, seq, final_g=None):
    D, H = lp["D"], lp["H"]
    PD = lp["w_pool_up"].shape[0]
    off_ga, off_gb = PD, PD + D

    h, dtraw, dtrawT = _norm_dt(x_parts, lp["norm_mix_g"], lp["w_dt"], lp["w_dtT"])
    n_in = lp["w_in"].shape[1]
    zs, (w_rest,) = _matmul(
        h, lp["w_in"], BF16, "in_proj_z", epilogue=lambda acc: acc * _sigmoid(acc), tn=_TN_F32_WEIGHT,
        w_cols=(0, D), casts=((lp["w_in"], (D, n_in - D)),))
    tail, (w_ff1, w_out, w_pool_up) = _matmul(
        h, w_rest, BF16, "in_proj_tail", tn=_TN_WIDE, w_cols=(lp["s_dt"] - D, PD + 2 * D),
        casts=(lp["w_ff1"], lp["w_out"], lp["w_pool_up"]))

    xbc_raw, (w_ple_gate, w_ssd_up) = _matmul(h, w_rest, BF16, "in_proj_xbc", tn=_TN_WIDE, w_cols=(0, lp["C"]),
                                              casts=(lp["w_ple_gate"], lp["w_ssd_up"]))
    xbc = _conv_silu(xbc_raw, 0, lp["conv_w"], lp["conv_b"], seq)
    prev_f, prev_b = _ssd_states(xbc, dtraw, lp["bias"], lp["alog"], lp["emat"], batch, seq, D, H)
    y = _ssd_out(zs, xbc, dtraw, dtrawT, prev_f, prev_b, lp["bias"], lp["alog"], lp["biasT"],
                 lp["alogT"], lp["d_skip"], lp["ssd_norm_g"], lp["emat"], batch, seq, D, H)
    gated_a = _matmul(y, w_ssd_up, BF16, "ssd_up", tn=_TN_WIDE,
                      epilogue=lambda acc, g_ref: _sigmoid(g_ref[...].astype(F32)) * acc,
                      extras=(tail,), col_offsets=(off_ga,),
                      extra_specs=lambda tm, tn: [_tile_spec(off_ga)(tm, tn)])

    pooled = _pool(tail, 0, lp["pool_w"], lp["pool_scale"], seq)
    mixed = _matmul(pooled, w_pool_up, BF16, "pool_up", tn=_TN_WIDE,
                    epilogue=lambda acc, g_ref, a_ref: (_sigmoid(g_ref[...].astype(F32)) * acc
                                                        + a_ref[...].astype(F32)),
                    extras=(tail, gated_a), col_offsets=(off_gb,),
                    extra_specs=lambda tm, tn: [_tile_spec(off_gb)(tm, tn), _tile_spec(0)(tm, tn)])
    x, xg, ssq = _matmul_residual_norm(mixed, w_out, x_parts, lp["norm_mlp_g"], "out_proj")

    ff, (w_ff2,) = _matmul(xg, w_ff1, BF16, "ff1", ssq=ssq, tn=_TN_WIDE, casts=(lp["w_ff2"],),
                           epilogue=lambda acc: jnp.square(jnp.maximum(acc, 0.0)))
    x, xg, ssq = _matmul_kacc_residual_norm(ff, w_ff2, x, lp["norm_ple_g"], "ff2")

    if final_g is not None:
        return _ple_final(xg, w_ple_gate, ssq, x, p_parts, lp["w_ple_proj"], final_g, "ple_final")
    ple = p_parts[0].shape[1]
    tm = _pick(functools.reduce(math.gcd, [p.shape[0] for p in p_parts]), (1024, 512, 256, 128))
    tiles = _part_tiles(p_parts, tm)

    def ple_epilogue(acc, x_ref, *refs):
        p = _part_read(refs[:-1], tiles, pl.program_id(0))
        return x_ref[...] + _sigmoid(acc) * _dot(p.astype(BF16), refs[-1][...])

    return _matmul(xg, w_ple_gate, F32, "ple", ssq=ssq, tm=tm, epilogue=ple_epilogue,
                   extras=(x, *p_parts, lp["w_ple_proj"]),
                   extra_specs=lambda tm, tn: ([_tile_spec(0)(tm, tn)] + _part_specs(p_parts, tm, ple)
                                               + [pl.BlockSpec((ple, tn), lambda i, j: (0, j))]))


def _run_groups(xs, ps, layers, norm_final_g):
    seq, D = xs[0].shape[1:]
    assert seq % SSD_CHUNK == 0
    batch = sum(x.shape[0] for x in xs)
    x_parts = [x.reshape(-1, D) for x in xs]
    outs = None
    for i, lp in enumerate(layers):
        p_parts = [p[i].reshape(-1, p.shape[-1]) for p in ps]
        if i == len(layers) - 1:
            outs = _layer(x_parts, p_parts, lp, batch, seq, final_g=norm_final_g)
        else:
            x_parts = [_layer(x_parts, p_parts, lp, batch, seq)]
    if outs is None:
        outs = _norm_final(jnp.concatenate(x_parts), norm_final_g, [x.shape[0] * seq for x in xs])
    return [o.reshape(x.shape) for o, x in zip(outs, xs)]


def kernel(x_prompt, x_sample, p_prompt, p_sample, norm_mix_g, w_in, conv_w, conv_b, dt_bias_f, dt_bias_b, a_log_f, a_log_b, d_skip, ssd_norm_g, w_ssd_up, pool_w, pool_scale, w_pool_up, w_out, norm_mlp_g, w_ff1, w_ff2, norm_ple_g, w_ple_gate, w_ple_proj, norm_final_g):
    stacked = (norm_mix_g, w_in, conv_w, conv_b, dt_bias_f, dt_bias_b, a_log_f, a_log_b, d_skip,
               ssd_norm_g, w_ssd_up, pool_w, pool_scale, w_pool_up, w_out, norm_mlp_g, w_ff1, w_ff2,
               norm_ple_g, w_ple_gate, w_ple_proj)
    layers = [_prepare_layer(*[w[i] for w in stacked]) for i in range(w_in.shape[0])]
    if x_prompt.shape[1] == x_sample.shape[1]:
        y_prompt, y_sample = _run_groups([x_prompt, x_sample], [p_prompt, p_sample], layers, norm_final_g)
    else:
        (y_prompt,) = _run_groups([x_prompt], [p_prompt], layers, norm_final_g)
        (y_sample,) = _run_groups([x_sample], [p_sample], layers, norm_final_g)
    return (y_prompt, y_sample)
```

```python
import functools
import math

import jax
import jax.numpy as jnp
from jax import lax
from jax.experimental import pallas as pl
from jax.experimental.pallas import tpu as pltpu

F32 = jnp.float32
BF16 = jnp.bfloat16

EPS = 1e-6
LOG2E = 1.4426950408889634
LN2 = 0.6931471805599453
SSD_GROUPS = 8
SSD_STATE = 128
SSD_CHUNK = 128
POOL_WINDOWS = (2, 4, 8, 16)

V7X_VMEM_BYTES = 64 * 1024 * 1024
V7X_VMEM_RESERVE_BYTES = 8 * 1024 * 1024
BF16_SUBLANE_TILE = 16
LANES = 128
HALO = BF16_SUBLANE_TILE
_TN_WIDE = 1024
_TN_F32_WEIGHT = 512
CONV_SUB_ROWS = 128
SSD_OUT_CHUNKS_PER_STEP = 2
CAST_BLOCK_BYTES = 2 * 1024 * 1024


def _pick(dim, prefs):
    for p in prefs:
        if dim % p == 0:
            return p
    return dim


def _nbytes(shape, dtype):
    n = 1
    for s in shape:
        n *= s
    return n * jnp.dtype(dtype).itemsize


def _params(semantics, block_bytes, scratch_bytes=0):
    est = 2 * block_bytes + scratch_bytes
    limit = min(2 * est + (4 << 20), V7X_VMEM_BYTES - V7X_VMEM_RESERVE_BYTES)
    return pltpu.CompilerParams(dimension_semantics=semantics, vmem_limit_bytes=int(limit))


def _sigmoid(v):
    return 1.0 / (1.0 + jnp.exp(-v))


def _softplus(v):
    u = jnp.exp(-jnp.abs(v))
    w = 1.0 + u
    return jnp.maximum(v, 0.0) + jnp.where(w == 1.0, u, (jnp.log2(w) * LN2) * (u / (w - 1.0)))


def _split3(v):
    a = v.astype(BF16)
    r = v - a.astype(F32)
    b = r.astype(BF16)
    c = (r - b.astype(F32)).astype(BF16)
    return a, b, c


def _split2(v):
    a = v.astype(BF16)
    b = (v - a.astype(F32)).astype(BF16)
    return a, b


def _dot(a, b):
    return jnp.dot(a, b, preferred_element_type=F32)


def _dot_nt(a, b):
    return lax.dot_general(a, b, (((1,), (1,)), ((), ())), preferred_element_type=F32)


def _dot_tn(a, b):
    return lax.dot_general(a, b, (((0,), (0,)), ((), ())), preferred_element_type=F32)


def _rms(x, g):
    ms = jnp.mean(x * x, axis=-1, keepdims=True)
    return x * lax.rsqrt(ms + EPS) * g


def _part_tiles(parts, tm):
    return [p.shape[0] // tm for p in parts]


def _part_specs(parts, tm, ncols, col_index=None):
    specs, start = [], 0
    for n in _part_tiles(parts, tm):
        def index(*idx, s=start, n=n):
            row = idx[0] - s
            if col_index is None:
                return jnp.clip(row, 0, n - 1), 0
            return jnp.clip(row, 0, n - 1), jnp.where((row >= 0) & (row < n), col_index(*idx), 0)
        specs.append(pl.BlockSpec((tm, ncols), index))
        start += n
    return specs


def _part_read(refs, tiles, i):
    val = refs[-1][...]
    end = sum(tiles[:-1])
    for ref, n in zip(reversed(refs[:-1]), reversed(tiles[:-1])):
        val = jnp.where(i < end, ref[...], val)
        end -= n
    return val


def _norm_dt_kernel(*refs, tiles):
    n = len(tiles)
    x_refs, (g_ref, wdt_ref, wdtT_ref, h_ref, dt_ref, dtT_ref) = refs[:n], refs[n:]
    h = _rms(_part_read(x_refs, tiles, pl.program_id(0)), g_ref[...]).astype(BF16)
    h_ref[...] = h
    dt_ref[...] = _dot(h, wdt_ref[...])
    dtT_ref[...] = _dot_nt(wdtT_ref[...], h)


def _norm_dt(x_parts, g, w_dt, w_dtT):
    D = x_parts[0].shape[1]
    M = sum(p.shape[0] for p in x_parts)
    H2 = w_dt.shape[1]
    tm = _pick(functools.reduce(math.gcd, [p.shape[0] for p in x_parts]), (512, 256, 128))
    blk = (len(x_parts) * _nbytes((tm, D), F32) + _nbytes((tm, D), BF16) + 2 * _nbytes((D, H2), BF16)
           + 2 * _nbytes((tm, H2), F32))
    return pl.pallas_call(
        functools.partial(_norm_dt_kernel, tiles=_part_tiles(x_parts, tm)),
        grid=(M // tm,),
        in_specs=_part_specs(x_parts, tm, D) + [
            pl.BlockSpec((1, D), lambda i: (0, 0)),
            pl.BlockSpec((D, H2), lambda i: (0, 0)),
            pl.BlockSpec((H2, D), lambda i: (0, 0))],
        out_specs=[pl.BlockSpec((tm, D), lambda i: (i, 0)),
                   pl.BlockSpec((tm, H2), lambda i: (i, 0)),
                   pl.BlockSpec((H2, tm), lambda i: (0, i))],
        out_shape=[jax.ShapeDtypeStruct((M, D), BF16),
                   jax.ShapeDtypeStruct((M, H2), F32),
                   jax.ShapeDtypeStruct((H2, M), F32)],
        compiler_params=_params(("arbitrary",), blk),
        name="norm_dt",
    )(*x_parts, g.reshape(1, D), w_dt, w_dtT)


def _norm_final_kernel(x_ref, g_ref, *o_refs, tiles):
    i = pl.program_id(0)
    y = _rms(x_ref[...], g_ref[...])
    start = 0
    for o_ref, n in zip(o_refs, tiles):
        @pl.when((i >= start) & (i < start + n))
        def _(o_ref=o_ref):
            o_ref[...] = y
        start += n


def _norm_final(x, g, part_rows):
    M, D = x.shape
    tm = _pick(functools.reduce(math.gcd, part_rows), (512, 256, 128))
    outs = [jax.ShapeDtypeStruct((r, D), F32) for r in part_rows]
    blk = (1 + len(part_rows)) * _nbytes((tm, D), F32)
    return pl.pallas_call(
        functools.partial(_norm_final_kernel, tiles=_part_tiles(outs, tm)),
        grid=(M // tm,),
        in_specs=[pl.BlockSpec((tm, D), lambda i: (i, 0)),
                  pl.BlockSpec((1, D), lambda i: (0, 0))],
        out_specs=_part_specs(outs, tm, D),
        out_shape=outs,
        compiler_params=_params(("arbitrary",), blk),
        name="norm_final",
    )(x, g.reshape(1, D))


def _row_rsqrt(ssq_ref, d):
    return lax.rsqrt(jnp.sum(ssq_ref[...], axis=-1, keepdims=True) * (1.0 / d) + EPS)


def _lane_partial_sumsq(v):
    sq = v * v
    part = sq[:, 0:LANES]
    for q in range(1, v.shape[1] // LANES):
        part = part + sq[:, q * LANES:(q + 1) * LANES]
    return part


def _mm_kernel(lhs_ref, w_ref, *rest, epilogue, norm_width, n_casts):
    n_in = len(rest) - 1 - 2 * n_casts
    ins, cast_srcs, out_ref, cast_dsts = rest[:n_in], rest[n_in:n_in + n_casts], rest[n_in + n_casts], rest[n_in + n_casts + 1:]
    acc = _dot(lhs_ref[...], w_ref[...].astype(lhs_ref.dtype))
    if norm_width:
        acc = acc * _row_rsqrt(ins[0], norm_width)
        ins = ins[1:]
    out_ref[...] = epilogue(acc, *ins).astype(out_ref.dtype)
    for src_ref, dst_ref in zip(cast_srcs, cast_dsts):
        dst_ref[...] = src_ref[...].astype(dst_ref.dtype)


def _plan_casts(items, steps):
    jobs, rest, used = [], [], 0
    for item in items:
        a, (c0, cw) = item if isinstance(item, tuple) else (item, (0, item.shape[1]))
        cols = _pick(math.gcd(cw, c0) if c0 else cw, (2048, 1024, 512, 256, 128))
        fit = [r for r in (128, 256, 512, 1024, 2048, 4096) if a.shape[0] % r == 0 and cw % cols == 0
               and c0 % cols == 0 and cols % LANES == 0
               and _nbytes((r, cols), F32) <= CAST_BLOCK_BYTES
               and used + (a.shape[0] // r) * (cw // cols) <= steps]
        if fit and not rest:
            jobs.append((a, c0, cw, (fit[0], cols)))
            used += (a.shape[0] // fit[0]) * (cw // cols)
        else:
            rest.append(a[:, c0:c0 + cw])
    return tuple(jobs), rest


def _cast_specs(jobs, n_i, n_j):
    src_specs, dst_specs, start = [], [], 0
    for a, c0, cw, block in jobs:
        n_rb, n_cb = a.shape[0] // block[0], cw // block[1]

        def index(i, j, start=start, n=n_rb * n_cb, n_cb=n_cb, cb0=0):
            b = jnp.clip(i * n_j + j - start, 0, n - 1)
            return b // n_cb, cb0 + b % n_cb
        src_specs.append(pl.BlockSpec(block, functools.partial(index, cb0=c0 // block[1])))
        dst_specs.append(pl.BlockSpec(block, index))
        start += n_rb * n_cb
    assert start <= n_i * n_j, "not enough grid steps to carry the weight casts"
    return src_specs, dst_specs


def _matmul(lhs, w, out_dtype, name, epilogue=None, extras=(), extra_specs=None, ssq=None, tm=None, tn=None,
            col_offsets=(), w_cols=None, casts=()):
    M, K = lhs.shape
    w_off, N = w_cols if w_cols else (0, w.shape[1])
    tm = tm or _pick(M, (1024, 512, 256, 128))
    tn = next(t for t in ((tn,) if tn else ()) + (512, 256, 128)
              if N % t == 0 and all(o % t == 0 for o in col_offsets))
    if epilogue is None:
        epilogue = lambda acc: acc
    specs = extra_specs(tm, tn) if extra_specs else []
    if ssq is not None:
        extras = (ssq,) + tuple(extras)
        specs = [pl.BlockSpec((tm, LANES), lambda i, j: (i, 0))] + specs
    cast_items = tuple(casts)
    jobs, uncast = _plan_casts(cast_items, (M // tm) * (N // tn))
    cast_src_specs, cast_dst_specs = _cast_specs(jobs, M // tm, N // tn)
    blk = (_nbytes((tm, K), lhs.dtype) + _nbytes((K, tn), w.dtype) + _nbytes((tm, tn), out_dtype)
           + sum(_nbytes(s.block_shape, e.dtype) for s, e in zip(specs, extras))
           + sum(_nbytes(job[3], F32) + _nbytes(job[3], BF16) for job in jobs)
           + (_nbytes((K, tn), BF16) if w.dtype != lhs.dtype else 0))
    if w_off % tn == 0:
        w_blk0 = w_off // tn
        w_spec = pl.BlockSpec((K, tn), lambda i, j: (0, w_blk0 + j))
    elif w_off % LANES == 0:
        w_spec = pl.BlockSpec((pl.Element(K), pl.Element(tn)),
                              lambda i, j: (0, pl.multiple_of(w_off + j * tn, LANES)))
    else:
        w = w[:, w_off:w_off + N]
        w_spec = pl.BlockSpec((K, tn), lambda i, j: (0, j))
    outs = pl.pallas_call(
        functools.partial(_mm_kernel, epilogue=epilogue, norm_width=K if ssq is not None else 0,
                          n_casts=len(jobs)),
        grid=(M // tm, N // tn),
        in_specs=[pl.BlockSpec((tm, K), lambda i, j: (i, 0)), w_spec] + specs + cast_src_specs,
        out_specs=[pl.BlockSpec((tm, tn), lambda i, j: (i, j))] + cast_dst_specs,
        out_shape=[jax.ShapeDtypeStruct((M, N), out_dtype)]
        + [jax.ShapeDtypeStruct((job[0].shape[0], job[2]), BF16) for job in jobs],
        compiler_params=_params(("arbitrary", "arbitrary") if jobs else ("parallel", "arbitrary"),
                                blk, _nbytes((tm, tn), F32)),
        name=name,
    )(lhs, w, *extras, *[job[0] for job in jobs])
    if not cast_items:
        return outs[0]
    return outs[0], tuple(outs[1:]) + tuple(a.astype(BF16) for a in uncast)


def _ple_final_kernel(lhs_ref, w_ref, ssq_ref, x_ref, *refs, tiles, n_row, width):
    n = len(tiles)
    p_refs, (wp_ref, g_ref), o_refs, (buf_ref, sq_ref) = refs[:n], refs[n:n + 2], refs[n + 2:2 * n + 2], refs[2 * n + 2:]
    i = pl.program_id(0)
    j = pl.program_id(1)
    slot = i % 2

    @pl.when(i < n_row)
    def _():
        acc = _dot(lhs_ref[...], w_ref[...]) * _row_rsqrt(ssq_ref, lhs_ref.shape[1])
        p = _part_read(p_refs, tiles, i)
        x3 = x_ref[...] + _sigmoid(acc) * _dot(p.astype(BF16), wp_ref[...])
        buf_ref[slot, j] = x3
        part = _lane_partial_sumsq(x3)
        sq_ref[slot] = jnp.where(j == 0, part, sq_ref[slot] + part)

    start = 0
    for o_ref, nt in zip(o_refs, tiles):
        @pl.when((i - 1 >= start) & (i - 1 < start + nt))
        def _(o_ref=o_ref):
            prev = 1 - slot
            r = lax.rsqrt(jnp.sum(sq_ref[prev], axis=-1, keepdims=True) * (1.0 / width) + EPS)
            o_ref[...] = buf_ref[prev, j] * r * g_ref[...]
        start += nt


def _ple_final(xg, w, ssq, x, p_parts, wp, g_final, name):
    M, K = xg.shape
    N = w.shape[1]
    ple = p_parts[0].shape[1]
    tm = _pick(functools.reduce(math.gcd, [p.shape[0] for p in p_parts]), (512, 256, 128))
    tn = _pick(N, (512, 256, 128))
    n_row, n_col = M // tm, N // tn
    tiles = _part_tiles(p_parts, tm)
    last = n_row - 1
    outs = [jax.ShapeDtypeStruct((p.shape[0], N), F32) for p in p_parts]
    out_specs, start = [], 0
    for nt in tiles:
        def index(i, j, s=start, nt=nt):
            row = i - 1 - s
            return jnp.clip(row, 0, nt - 1), jnp.where(row < 0, 0, jnp.where(row >= nt, n_col - 1, j))
        out_specs.append(pl.BlockSpec((tm, tn), index))
        start += nt
    blk = (_nbytes((tm, K), BF16) + _nbytes((K, tn), BF16) + (1 + len(p_parts)) * _nbytes((tm, tn), F32)
           + len(p_parts) * _nbytes((tm, ple), F32) + _nbytes((ple, tn), BF16) + _nbytes((tm, LANES), F32))
    row = lambda i: jnp.minimum(i, last)
    return pl.pallas_call(
        functools.partial(_ple_final_kernel, tiles=tiles, n_row=n_row, width=N),
        grid=(n_row + 1, n_col),
        in_specs=[pl.BlockSpec((tm, K), lambda i, j: (row(i), 0)),
                  pl.BlockSpec((K, tn), lambda i, j: (0, j)),
                  pl.BlockSpec((tm, LANES), lambda i, j: (row(i), 0)),
                  pl.BlockSpec((tm, tn), lambda i, j: (row(i), j))]
        + _part_specs(p_parts, tm, ple)
        + [pl.BlockSpec((ple, tn), lambda i, j: (0, j)),
           pl.BlockSpec((1, tn), lambda i, j: (0, j))],
        out_specs=out_specs,
        out_shape=outs,
        scratch_shapes=[pltpu.VMEM((2, n_col, tm, tn), F32), pltpu.VMEM((2, tm, LANES), F32)],
        compiler_params=_params(("arbitrary", "arbitrary"), blk,
                                _nbytes((2, n_col, tm, tn), F32) + _nbytes((tm, tn), F32)),
        name=name,
    )(xg, w, ssq, x, *p_parts, wp, g_final.reshape(1, N))


def _tile_spec(col_offset):
    def make(tm, tn):
        assert col_offset % tn == 0
        off = col_offset // tn
        return pl.BlockSpec((tm, tn), lambda i, j: (i, off + j))
    return make


def _emit_norm_operand(x_new, g_ref, xg_ref, ssq_ref, first):
    xg_ref[...] = (x_new * g_ref[...]).astype(xg_ref.dtype)
    part = _lane_partial_sumsq(x_new)

    @pl.when(first)
    def _():
        ssq_ref[...] = part

    @pl.when(jnp.logical_not(first))
    def _():
        ssq_ref[...] += part


def _mm_res_norm_kernel(lhs_ref, w_ref, *refs, tiles):
    n = len(tiles)
    res_refs, (g_ref, x_ref, xg_ref, ssq_ref) = refs[:n], refs[n:]
    x_new = _part_read(res_refs, tiles, pl.program_id(0)) + _dot(lhs_ref[...], w_ref[...])
    x_ref[...] = x_new
    _emit_norm_operand(x_new, g_ref, xg_ref, ssq_ref, pl.program_id(1) == 0)


def _matmul_residual_norm(lhs, w, res_parts, g_next, name):
    M, K = lhs.shape
    N = w.shape[1]
    tm = _pick(functools.reduce(math.gcd, [p.shape[0] for p in res_parts]), (1024, 512, 256, 128))
    tn = _pick(N, (512, 256, 128))
    blk = (_nbytes((tm, K), lhs.dtype) + _nbytes((K, tn), w.dtype) + (1 + len(res_parts)) * _nbytes((tm, tn), F32)
           + _nbytes((tm, tn), BF16) + _nbytes((tm, LANES), F32))
    return pl.pallas_call(
        functools.partial(_mm_res_norm_kernel, tiles=_part_tiles(res_parts, tm)),
        grid=(M // tm, N // tn),
        in_specs=[pl.BlockSpec((tm, K), lambda i, j: (i, 0)),
                  pl.BlockSpec((K, tn), lambda i, j: (0, j))]
        + _part_specs(res_parts, tm, tn, col_index=lambda i, j: j)
        + [pl.BlockSpec((1, tn), lambda i, j: (0, j))],
        out_specs=[pl.BlockSpec((tm, tn), lambda i, j: (i, j)),
                   pl.BlockSpec((tm, tn), lambda i, j: (i, j)),
                   pl.BlockSpec((tm, LANES), lambda i, j: (i, 0))],
        out_shape=[jax.ShapeDtypeStruct((M, N), F32),
                   jax.ShapeDtypeStruct((M, N), BF16),
                   jax.ShapeDtypeStruct((M, LANES), F32)],
        compiler_params=_params(("arbitrary", "arbitrary"), blk, _nbytes((tm, tn), F32)),
        name=name,
    )(lhs, w, *res_parts, g_next.reshape(1, N))


def _mm_kacc_kernel(lhs_ref, w_ref, res_ref, g_ref, x_ref, xg_ref, ssq_ref, *, nk):
    k = pl.program_id(2)

    def step(base_ref, finish):
        x_new = base_ref[...] + _dot(lhs_ref[...], w_ref[...])
        x_ref[...] = x_new
        if finish:
            _emit_norm_operand(x_new, g_ref, xg_ref, ssq_ref, pl.program_id(1) == 0)

    if nk == 1:
        step(res_ref, True)
        return
    pl.when(k == 0)(lambda: step(res_ref, False))
    if nk > 2:
        pl.when((k > 0) & (k < nk - 1))(lambda: step(x_ref, False))
    pl.when(k == nk - 1)(lambda: step(x_ref, True))


def _matmul_kacc_residual_norm(lhs, w, res, g_next, name):
    M, K = lhs.shape
    N = w.shape[1]
    tm = _pick(M, (1024, 512, 256, 128))
    tn = _pick(N, (1024, 512, 256, 128))
    tk = _pick(K, (2048, 1024, 512, 256, 128))
    blk = (_nbytes((tm, tk), lhs.dtype) + _nbytes((tk, tn), w.dtype) + 2 * _nbytes((tm, tn), F32)
           + _nbytes((tm, tn), BF16) + _nbytes((tm, LANES), F32))
    return pl.pallas_call(
        functools.partial(_mm_kacc_kernel, nk=K // tk),
        grid=(M // tm, N // tn, K // tk),
        in_specs=[pl.BlockSpec((tm, tk), lambda i, j, k: (i, k)),
                  pl.BlockSpec((tk, tn), lambda i, j, k: (k, j)),
                  pl.BlockSpec((tm, tn), lambda i, j, k: (i, j)),
                  pl.BlockSpec((1, tn), lambda i, j, k: (0, j))],
        out_specs=[pl.BlockSpec((tm, tn), lambda i, j, k: (i, j)),
                   pl.BlockSpec((tm, tn), lambda i, j, k: (i, j)),
                   pl.BlockSpec((tm, LANES), lambda i, j, k: (i, 0))],
        out_shape=[jax.ShapeDtypeStruct((M, N), F32),
                   jax.ShapeDtypeStruct((M, N), BF16),
                   jax.ShapeDtypeStruct((M, LANES), F32)],
        compiler_params=_params(("parallel", "arbitrary", "arbitrary"), blk, _nbytes((tm, tn), F32)),
        name=name,
    )(lhs, w, res, g_next.reshape(1, N))


def _conv_kernel(prev_ref, cur_ref, next_ref, w_ref, b_ref, shift_ref, out_ref, buf_ref, *, tm, seq_tiles, width):
    i = pl.program_id(0)
    first = (i % seq_tiles) == 0
    last = (i % seq_tiles) == seq_tiles - 1
    prev = prev_ref[...]
    nxt = next_ref[...]
    buf_ref[0:HALO, :] = jnp.where(first, jnp.zeros_like(prev), prev)
    buf_ref[HALO:HALO + tm, :] = cur_ref[...]
    buf_ref[HALO + tm:, :] = jnp.where(last, jnp.zeros_like(nxt), nxt)

    half = width // 2
    taps = [k for k in range(width) if k != half]
    sub = shift_ref.shape[0]
    win = sub + 2 * HALO
    shift = shift_ref[...]
    w_taps = [w_ref[k:k + 1, :].astype(BF16) for k in taps]
    for r0 in range(0, tm, sub):
        window = buf_ref[r0:r0 + win, :]
        scaled = jnp.concatenate([window * wk for wk in w_taps], axis=0)
        acc = (b_ref[...] + w_ref[half:half + 1, :] * window[HALO:HALO + sub].astype(F32)
               + _dot(shift, scaled))
        out_ref[r0:r0 + sub, :] = (acc * _sigmoid(acc)).astype(out_ref.dtype)


def _conv_silu(proj, col_offset, conv_w, conv_b, seq):
    M = proj.shape[0]
    width, C = conv_w.shape
    assert width // 2 <= HALO
    tm = _pick(seq, (1024, 512, 256, 128))
    tc = _pick(C, (2048, 1024, 512, 256, 128))
    while col_offset % tc:
        tc //= 2
    off = col_offset // tc
    hb = tm // HALO
    n_halo = M // HALO
    half = width // 2
    sub = min(tm, CONV_SUB_ROWS)
    win = sub + 2 * HALO
    rows = jnp.arange(sub, dtype=jnp.int32)[:, None]
    cols = jnp.arange(win, dtype=jnp.int32)[None, :]
    shift = jnp.concatenate([(cols == rows + HALO + (k - half)) for k in range(width) if k != half],
                            axis=1).astype(BF16)
    blk = (2 * _nbytes((tm, tc), BF16) + 2 * _nbytes((HALO, tc), BF16) + _nbytes((width + 1, tc), F32)
           + _nbytes(shift.shape, BF16))
    return pl.pallas_call(
        functools.partial(_conv_kernel, tm=tm, seq_tiles=seq // tm, width=width),
        grid=(M // tm, C // tc),
        in_specs=[pl.BlockSpec((HALO, tc), lambda i, j: (jnp.maximum(i * hb - 1, 0), off + j)),
                  pl.BlockSpec((tm, tc), lambda i, j: (i, off + j)),
                  pl.BlockSpec((HALO, tc), lambda i, j: (jnp.minimum((i + 1) * hb, n_halo - 1), off + j)),
                  pl.BlockSpec((width, tc), lambda i, j: (0, j)),
                  pl.BlockSpec((1, tc), lambda i, j: (0, j)),
                  pl.BlockSpec(shift.shape, lambda i, j: (0, 0))],
        out_specs=pl.BlockSpec((tm, tc), lambda i, j: (i, j)),
        out_shape=jax.ShapeDtypeStruct((M, C), BF16),
        scratch_shapes=[pltpu.VMEM((tm + 2 * HALO, tc), BF16)],
        compiler_params=_params(("parallel", "parallel"), blk,
                                _nbytes((tm + 2 * HALO, tc), BF16) + 8 * _nbytes((CONV_SUB_ROWS, tc), F32)),
        name="conv_silu",
    )(proj, proj, proj, conv_w, conv_b.reshape(1, C), shift)


def _pool_kernel(prev_ref, cur_ref, next_ref, pw_ref, ps_ref, out_ref, *, tm, seq, Wd):
    i = pl.program_id(0)
    t0 = (i * tm) % seq
    has_prev = t0 > 0
    has_next = t0 + tm < seq
    r = lax.broadcasted_iota(jnp.int32, (tm, tm), 0)
    c = lax.broadcasted_iota(jnp.int32, (tm, tm), 1)
    rh = lax.broadcasted_iota(jnp.int32, (HALO, HALO), 0)
    ch = lax.broadcasted_iota(jnp.int32, (HALO, HALO), 1)
    t = lax.broadcasted_iota(jnp.int32, (tm, Wd), 0) + t0
    for g, w in enumerate(POOL_WINDOWS):
        lo_off = w // 2
        hi_off = w - lo_off
        sl = slice(g * Wd, (g + 1) * Wd)
        band = jnp.where((c >= r - lo_off) & (c < r + hi_off), 1.0, 0.0).astype(BF16)
        band_prev = jnp.where(has_prev & (ch - HALO >= rh - lo_off), 1.0, 0.0).astype(BF16)
        band_next = jnp.where(has_next & (ch < rh - HALO + hi_off), 1.0, 0.0).astype(BF16)
        cur = cur_ref[:, sl]
        wsum = _dot(band, cur)
        wsum = jnp.concatenate([wsum[0:HALO] + _dot(band_prev, prev_ref[:, sl]),
                                wsum[HALO:tm - HALO],
                                wsum[tm - HALO:] + _dot(band_next, next_ref[:, sl])], axis=0)
        cnt = jnp.minimum(t + hi_off, seq) - jnp.maximum(t - lo_off, 0)
        pooled = (wsum / cnt.astype(F32) - cur.astype(F32)).astype(BF16)
        out_ref[:, sl] = (_dot(pooled, pw_ref[g]) * ps_ref[:, sl]).astype(out_ref.dtype)


def _pool(proj, col_offset, pool_w, pool_scale, seq):
    M = proj.shape[0]
    PG, Wd, _ = pool_w.shape
    PD = PG * Wd
    assert PG == len(POOL_WINDOWS) and max(POOL_WINDOWS) // 2 <= HALO
    assert col_offset % PD == 0
    off = col_offset // PD
    tm = _pick(seq, (256, 128))
    hb = tm // HALO
    n_halo = M // HALO
    blk = (2 * _nbytes((tm, PD), BF16) + 2 * _nbytes((HALO, PD), BF16) + _nbytes((PG, Wd, Wd), BF16))
    return pl.pallas_call(
        functools.partial(_pool_kernel, tm=tm, seq=seq, Wd=Wd),
        grid=(M // tm,),
        in_specs=[pl.BlockSpec((HALO, PD), lambda i: (jnp.maximum(i * hb - 1, 0), off)),
                  pl.BlockSpec((tm, PD), lambda i: (i, off)),
                  pl.BlockSpec((HALO, PD), lambda i: (jnp.minimum((i + 1) * hb, n_halo - 1), off)),
                  pl.BlockSpec((PG, Wd, Wd), lambda i: (0, 0, 0)),
                  pl.BlockSpec((1, PD), lambda i: (0, 0))],
        out_specs=pl.BlockSpec((tm, PD), lambda i: (i, 0)),
        out_shape=jax.ShapeDtypeStruct((M, PD), BF16),
        compiler_params=_params(("parallel",), blk, 4 * _nbytes((tm, PD), F32)),
        name="pool",
    )(proj, proj, proj, pool_w, pool_scale.reshape(1, PD))


def _cumsum_rows(tri, v):
    a, b, c = _split3(v)
    return _dot(tri, a) + _dot(tri, b) + _dot(tri, c)


def _cumsum_lanes(v, triT):
    a, b, c = _split3(v)
    return _dot(a, triT) + _dot(b, triT) + _dot(c, triT)


def _hilo(v):
    a, b = _split2(v)
    return jnp.concatenate([a, b], axis=1)


def _expand(v_hilo, e2):
    return _dot(v_hilo, e2)


def _ssd_state_kernel(xf_ref, bf_ref, dtf_ref, xb_ref, bb_ref, dtb_ref, bias_ref, alog_ref, e_ref,
                      pf_ref, pb_ref, h_ref, *, H, G, N, Q):
    L = SSD_CHUNK

    @pl.when(pl.program_id(1) == 0)
    def _():
        h_ref[...] = jnp.zeros_like(h_ref)

    row = lax.broadcasted_iota(jnp.int32, (L, L), 0)
    col = lax.broadcasted_iota(jnp.int32, (L, L), 1)
    tri = jnp.where(row >= col, 1.0, 0.0).astype(BF16)
    is_fwd = lax.broadcasted_iota(jnp.int32, (L, 2 * H), 1) < H
    A = -jnp.exp(alog_ref[...])

    def direction(d, x_ref, b_ref, dtraw_ref, p_ref):
        dt = _softplus(dtraw_ref[...] + bias_ref[...])
        a = dt * A
        cum = _cumsum_rows(tri, a)
        total = cum[L - 1:L, :]
        dte = jnp.exp(jnp.where(is_fwd, total - cum, cum - a))
        factors = jnp.concatenate([dt * dte, jnp.broadcast_to(jnp.exp(total), (8, 2 * H))], axis=0)
        expanded = _expand(_hilo(factors), e_ref[d])
        xs = (x_ref[...].astype(F32) * expanded[0:L]).astype(BF16)
        cdexp = expanded[L:L + 1, :]
        bm = b_ref[...]
        for g in range(G):
            st = _dot_tn(bm[:, g * N:(g + 1) * N], xs[:, g * Q:(g + 1) * Q])
            h_old = h_ref[d, g]
            p_ref[0, 0, g] = h_old.astype(BF16)
            h_ref[d, g] = cdexp[:, g * Q:(g + 1) * Q] * h_old + st

    direction(0, xf_ref, bf_ref, dtf_ref, pf_ref)
    direction(1, xb_ref, bb_ref, dtb_ref, pb_ref)


def _ssd_states(xbc, dtraw, bias, alog, emat, batch, seq, D, H):
    L, G, N = SSD_CHUNK, SSD_GROUPS, SSD_STATE
    GN = G * N
    Q = D // G
    NC = seq // L
    assert D % GN == 0 or GN % D == 0
    b_off = D // GN
    H2 = 2 * H
    blk = 2 * (_nbytes((L, D), BF16) + _nbytes((L, GN), BF16) + _nbytes((L, H2), F32)
               + _nbytes((G, N, Q), BF16)) + _nbytes((2, 2 * H2, D), BF16)
    fwd = lambda b, t: b * NC + t
    bwd = lambda b, t: b * NC + (NC - 1 - t)
    out_sds = jax.ShapeDtypeStruct((batch, NC, G, N, Q), BF16)
    return pl.pallas_call(
        functools.partial(_ssd_state_kernel, H=H, G=G, N=N, Q=Q),
        grid=(batch, NC),
        in_specs=[pl.BlockSpec((L, D), lambda b, t: (fwd(b, t), 0)),
                  pl.BlockSpec((L, GN), lambda b, t: (fwd(b, t), b_off)),
                  pl.BlockSpec((L, H2), lambda b, t: (fwd(b, t), 0)),
                  pl.BlockSpec((L, D), lambda b, t: (bwd(b, t), 0)),
                  pl.BlockSpec((L, GN), lambda b, t: (bwd(b, t), b_off)),
                  pl.BlockSpec((L, H2), lambda b, t: (bwd(b, t), 0)),
                  pl.BlockSpec((1, H2), lambda b, t: (0, 0)),
                  pl.BlockSpec((1, H2), lambda b, t: (0, 0)),
                  pl.BlockSpec((2, 2 * H2, D), lambda b, t: (0, 0, 0))],
        out_specs=[pl.BlockSpec((1, 1, G, N, Q), lambda b, t: (b, t, 0, 0, 0)),
                   pl.BlockSpec((1, 1, G, N, Q), lambda b, t: (b, NC - 1 - t, 0, 0, 0))],
        out_shape=[out_sds, out_sds],
        scratch_shapes=[pltpu.VMEM((2, G, N, Q), F32)],
        compiler_params=_params(("parallel", "arbitrary"), blk,
                                _nbytes((2, G, N, Q), F32) + 4 * _nbytes((L, D), F32)),
        name="ssd_states",
    )(xbc, xbc, dtraw, xbc, xbc, dtraw, bias, alog, emat)


def _ssd_out_kernel(dsk_ref, x_ref, b_ref, c_ref, zs_ref, dt_ref, dtT_ref, pf_ref, pb_ref,
                    bias_ref, alog_ref, biasT_ref, alogT_ref, ng_ref, e_ref, out_ref,
                    *, H, G, N, Q, P):
    L = SSD_CHUNK
    E = H // G
    row = lax.broadcasted_iota(jnp.int32, (L, L), 0)
    col = lax.broadcasted_iota(jnp.int32, (L, L), 1)
    lower = row >= col
    eye = row == col
    tri = jnp.where(lower, 1.0, 0.0).astype(BF16)
    triT = jnp.where(row <= col, 1.0, 0.0).astype(BF16)
    is_fwd = lax.broadcasted_iota(jnp.int32, (L, 2 * H), 1) < H
    is_fwdT = lax.broadcasted_iota(jnp.int32, (2 * H, L), 0) < H
    lane = lax.broadcasted_iota(jnp.int32, (L, 2 * P), 1)
    keep_left = jnp.where(lane < P, 1.0, 0.0).astype(BF16)
    keep_right = jnp.where(lane < P, 0.0, 1.0).astype(BF16)
    masks = (lower, eye, tri, triT, is_fwd, is_fwdT, keep_left, keep_right)
    for sc in range(x_ref.shape[0] // L):
        r = slice(sc * L, (sc + 1) * L)
        _ssd_out_chunk(masks, dsk_ref, x_ref.at[r], b_ref.at[r], c_ref.at[r], zs_ref.at[r], dt_ref.at[r],
                       dtT_ref.at[:, r], pf_ref.at[0, sc], pb_ref.at[0, sc], bias_ref, alog_ref, biasT_ref,
                       alogT_ref, ng_ref, e_ref, out_ref.at[r], H=H, G=G, N=N, Q=Q, P=P)


def _ssd_out_chunk(masks, dsk_ref, x_ref, b_ref, c_ref, zs_ref, dt_ref, dtT_ref, pf_ref, pb_ref,
                   bias_ref, alog_ref, biasT_ref, alogT_ref, ng_ref, e_ref, out_ref, *, H, G, N, Q, P):
    L = SSD_CHUNK
    E = H // G
    lower, eye, tri, triT, is_fwd, is_fwdT, keep_left, keep_right = masks

    dt = _softplus(dt_ref[...] + bias_ref[...])
    a = dt * (-jnp.exp(alog_ref[...]))
    cum = _cumsum_rows(tri, a)
    total = cum[L - 1:L, :]
    excl = cum - a
    col2 = jnp.where(is_fwd, cum, excl) * LOG2E
    sdec = _hilo(jnp.exp(jnp.where(is_fwd, cum, total - excl)))

    dtT = _softplus(dtT_ref[...] + biasT_ref[...])
    aT = dtT * (-jnp.exp(alogT_ref[...]))
    cumT = _cumsum_lanes(aT, triT)
    lg = jnp.log2(dtT)
    row2 = jnp.where(is_fwdT, cumT * LOG2E - lg, (cumT - aT) * LOG2E + lg)

    for g in range(G):
        c_g = c_ref[:, g * N:(g + 1) * N]
        cb = _dot_nt(c_g, b_ref[:, g * N:(g + 1) * N])
        cb_diag = jnp.sum(jnp.where(eye, cb, 0.0), axis=0, keepdims=True)
        pieces = []
        for e2 in range(E // 2):
            ws = []
            for h in (g * E + 2 * e2, g * E + 2 * e2 + 1):
                arg = jnp.where(lower,
                                col2[:, h:h + 1] - row2[h:h + 1, :],
                                row2[H + h:H + h + 1, :] - col2[:, H + h:H + h + 1])
                diag = cb_diag * dtT[H + h:H + h + 1, :] + dsk_ref[h]
                ws.append((cb * jnp.exp2(arg) + jnp.where(eye, diag, 0.0)).astype(BF16))
            h0 = g * E + 2 * e2
            xp = x_ref[:, h0 * P:(h0 + 2) * P]
            rhs = jnp.concatenate([xp * keep_left, xp * keep_right], axis=0)
            pieces.append(_dot(jnp.concatenate(ws, axis=1), rhs))
        y = jnp.concatenate(pieces, axis=1) if len(pieces) > 1 else pieces[0]
        sl = slice(g * Q, (g + 1) * Q)
        y = (y + _dot(c_g, pf_ref[g]) * _expand(sdec, e_ref[0, :, sl])
             + _dot(c_g, pb_ref[g]) * _expand(sdec, e_ref[1, :, sl]))
        y = y * zs_ref[:, sl].astype(F32)
        ms = jnp.mean(y * y, axis=-1, keepdims=True)
        out_ref[:, sl] = (y * lax.rsqrt(ms + EPS) * ng_ref[:, sl]).astype(out_ref.dtype)


def _ssd_out(zs, xbc, dtraw, dtrawT, prev_f, prev_b, bias, alog, biasT, alogT, d_skip, norm_g, emat,
             batch, seq, D, H):
    L, G, N = SSD_CHUNK, SSD_GROUPS, SSD_STATE
    GN = G * N
    Q = D // G
    P = D // H
    NC = seq // L
    M = batch * seq
    H2 = 2 * H
    assert (H // G) % 2 == 0 and 2 * P == 128 and D % GN == 0
    b_off = D // GN
    cps = SSD_OUT_CHUNKS_PER_STEP if NC % SSD_OUT_CHUNKS_PER_STEP == 0 else 1
    R = cps * L
    steps = NC // cps
    blk = (2 * _nbytes((R, D), BF16) + 2 * _nbytes((R, GN), BF16) + 2 * _nbytes((R, H2), F32)
           + 2 * _nbytes((cps, G, N, Q), BF16) + _nbytes((2, 2 * H2, D), BF16) + _nbytes((R, D), BF16))
    rowblk = lambda b, c: b * steps + c
    return pl.pallas_call(
        functools.partial(_ssd_out_kernel, H=H, G=G, N=N, Q=Q, P=P),
        grid=(batch, steps),
        in_specs=[pl.BlockSpec(memory_space=pltpu.SMEM),
                  pl.BlockSpec((R, D), lambda b, c: (rowblk(b, c), 0)),
                  pl.BlockSpec((R, GN), lambda b, c: (rowblk(b, c), b_off)),
                  pl.BlockSpec((R, GN), lambda b, c: (rowblk(b, c), b_off + 1)),
                  pl.BlockSpec((R, D), lambda b, c: (rowblk(b, c), 0)),
                  pl.BlockSpec((R, H2), lambda b, c: (rowblk(b, c), 0)),
                  pl.BlockSpec((H2, R), lambda b, c: (0, rowblk(b, c))),
                  pl.BlockSpec((1, cps, G, N, Q), lambda b, c: (b, c, 0, 0, 0)),
                  pl.BlockSpec((1, cps, G, N, Q), lambda b, c: (b, c, 0, 0, 0)),
                  pl.BlockSpec((1, H2), lambda b, c: (0, 0)),
                  pl.BlockSpec((1, H2), lambda b, c: (0, 0)),
                  pl.BlockSpec((H2, L), lambda b, c: (0, 0)),
                  pl.BlockSpec((H2, L), lambda b, c: (0, 0)),
                  pl.BlockSpec((1, D), lambda b, c: (0, 0)),
                  pl.BlockSpec((2, 2 * H2, D), lambda b, c: (0, 0, 0))],
        out_specs=pl.BlockSpec((R, D), lambda b, c: (rowblk(b, c), 0)),
        out_shape=jax.ShapeDtypeStruct((M, D), BF16),
        compiler_params=_params(("parallel", "parallel"), blk, 8 * _nbytes((L, D), F32)),
        name="ssd_out",
    )(d_skip, xbc, xbc, xbc, zs, dtraw, dtrawT, prev_f, prev_b, bias, alog, biasT, alogT,
      norm_g.reshape(1, D), emat)


def _prepare_layer(norm_mix_g, w_in, conv_w, conv_b, dt_bias_f, dt_bias_b, a_log_f, a_log_b, d_skip,
                   ssd_norm_g, w_ssd_up, pool_w, pool_scale, w_pool_up, w_out, norm_mlp_g, w_ff1, w_ff2,
                   norm_ple_g, w_ple_gate, w_ple_proj):
    D = w_in.shape[0]
    H = d_skip.shape[0]
    C = conv_w.shape[1]
    s_xbc = D + C
    s_dt = s_xbc + 2 * H
    w_dt = w_in[:, s_xbc:s_dt].astype(BF16)
    head_of_lane = jnp.arange(D, dtype=jnp.int32) // (D // H)
    onehot = (jnp.arange(H, dtype=jnp.int32)[:, None] == head_of_lane[None, :]).astype(BF16)
    zeros = jnp.zeros_like(onehot)
    L = SSD_CHUNK
    bias = jnp.concatenate([dt_bias_f, dt_bias_b])
    alog = jnp.concatenate([a_log_f, a_log_b])
    return dict(
        D=D, H=H, C=C,
        norm_mix_g=norm_mix_g,
        w_in=w_in, s_dt=s_dt,
        w_dt=w_dt, w_dtT=w_dt.T,
        conv_w=conv_w, conv_b=conv_b,
        bias=bias.reshape(1, 2 * H), alog=alog.reshape(1, 2 * H),
        biasT=jnp.broadcast_to(bias[:, None], (2 * H, L)), alogT=jnp.broadcast_to(alog[:, None], (2 * H, L)),
        emat=jnp.stack([jnp.concatenate([onehot, zeros, onehot, zeros]),
                        jnp.concatenate([zeros, onehot, zeros, onehot])]),
        d_skip=d_skip, ssd_norm_g=ssd_norm_g,
        w_ssd_up=w_ssd_up, w_pool_up=w_pool_up, w_out=w_out, w_ff1=w_ff1, w_ff2=w_ff2, w_ple_gate=w_ple_gate,
        pool_w=pool_w.astype(BF16), pool_scale=pool_scale,
        norm_mlp_g=norm_mlp_g, norm_ple_g=norm_ple_g, w_ple_proj=w_ple_proj.astype(BF16),
    )


def _layer(x_parts, p_parts, lp, batch, seq, final_g=None):
    D, H = lp["D"], lp["H"]
    PD = lp["w_pool_up"].shape[0]
    off_ga, off_gb = PD, PD + D

    h, dtraw, dtrawT = _norm_dt(x_parts, lp["norm_mix_g"], lp["w_dt"], lp["w_dtT"])
    n_in = lp["w_in"].shape[1]
    zs, (w_rest,) = _matmul(
        h, lp["w_in"], BF16, "in_proj_z", epilogue=lambda acc: acc * _sigmoid(acc), tn=_TN_F32_WEIGHT,
        w_cols=(0, D), casts=((lp["w_in"], (D, n_in - D)),))
    tail, (w_ff1, w_out, w_pool_up) = _matmul(
        h, w_rest, BF16, "in_proj_tail", tn=_TN_WIDE, w_cols=(lp["s_dt"] - D, PD + 2 * D),
        casts=(lp["w_ff1"], lp["w_out"], lp["w_pool_up"]))

    xbc_raw, (w_ple_gate, w_ssd_up) = _matmul(h, w_rest, BF16, "in_proj_xbc", tn=_TN_WIDE, w_cols=(0, lp["C"]),
                                              casts=(lp["w_ple_gate"], lp["w_ssd_up"]))
    xbc = _conv_silu(xbc_raw, 0, lp["conv_w"], lp["conv_b"], seq)
    prev_f, prev_b = _ssd_states(xbc, dtraw, lp["bias"], lp["alog"], lp["emat"], batch, seq, D, H)
    y = _ssd_out(zs, xbc, dtraw, dtrawT, prev_f, prev_b, lp["bias"], lp["alog"], lp["biasT"],
                 lp["alogT"], lp["d_skip"], lp["ssd_norm_g"], lp["emat"], batch, seq, D, H)
    gated_a = _matmul(y, w_ssd_up, BF16, "ssd_up", tn=_TN_WIDE,
                      epilogue=lambda acc, g_ref: _sigmoid(g_ref[...].astype(F32)) * acc,
                      extras=(tail,), col_offsets=(off_ga,),
                      extra_specs=lambda tm, tn: [_tile_spec(off_ga)(tm, tn)])

    pooled = _pool(tail, 0, lp["pool_w"], lp["pool_scale"], seq)
    mixed = _matmul(pooled, w_pool_up, BF16, "pool_up", tn=_TN_WIDE,
                    epilogue=lambda acc, g_ref, a_ref: (_sigmoid(g_ref[...].astype(F32)) * acc
                                                        + a_ref[...].astype(F32)),
                    extras=(tail, gated_a), col_offsets=(off_gb,),
                    extra_specs=lambda tm, tn: [_tile_spec(off_gb)(tm, tn), _tile_spec(0)(tm, tn)])
    x, xg, ssq = _matmul_residual_norm(mixed, w_out, x_parts, lp["norm_mlp_g"], "out_proj")

    ff, (w_ff2,) = _matmul(xg, w_ff1, BF16, "ff1", ssq=ssq, tn=_TN_WIDE, casts=(lp["w_ff2"],),
                           epilogue=lambda acc: jnp.square(jnp.maximum(acc, 0.0)))
    x, xg, ssq = _matmul_kacc_residual_norm(ff, w_ff2, x, lp["norm_ple_g"], "ff2")

    if final_g is not None:
        return _ple_final(xg, w_ple_gate, ssq, x, p_parts, lp["w_ple_proj"], final_g, "ple_final")
    ple = p_parts[0].shape[1]
    tm = _pick(functools.reduce(math.gcd, [p.shape[0] for p in p_parts]), (1024, 512, 256, 128))
    tiles = _part_tiles(p_parts, tm)

    def ple_epilogue(acc, x_ref, *refs):
        p = _part_read(refs[:-1], tiles, pl.program_id(0))
        return x_ref[...] + _sigmoid(acc) * _dot(p.astype(BF16), refs[-1][...])

    return _matmul(xg, w_ple_gate, F32, "ple", ssq=ssq, tm=tm, epilogue=ple_epilogue,
                   extras=(x, *p_parts, lp["w_ple_proj"]),
                   extra_specs=lambda tm, tn: ([_tile_spec(0)(tm, tn)] + _part_specs(p_parts, tm, ple)
                                               + [pl.BlockSpec((ple, tn), lambda i, j: (0, j))]))


def _run_groups(xs, ps, layers, norm_final_g):
    seq, D = xs[0].shape[1:]
    assert seq % SSD_CHUNK == 0
    batch = sum(x.shape[0] for x in xs)
    x_parts = [x.reshape(-1, D) for x in xs]
    outs = None
    for i, lp in enumerate(layers):
        p_parts = [p[i].reshape(-1, p.shape[-1]) for p in ps]
        if i == len(layers) - 1:
            outs = _layer(x_parts, p_parts, lp, batch, seq, final_g=norm_final_g)
        else:
            x_parts = [_layer(x_parts, p_parts, lp, batch, seq)]
    if outs is None:
        outs = _norm_final(jnp.concatenate(x_parts), norm_final_g, [x.shape[0] * seq for x in xs])
    return [o.reshape(x.shape) for o, x in zip(outs, xs)]


def kernel(x_prompt, x_sample, p_prompt, p_sample, norm_mix_g, w_in, conv_w, conv_b, dt_bias_f, dt_bias_b, a_log_f, a_log_b, d_skip, ssd_norm_g, w_ssd_up, pool_w, pool_scale, w_pool_up, w_out, norm_mlp_g, w_ff1, w_ff2, norm_ple_g, w_ple_gate, w_ple_proj, norm_final_g):
    stacked = (norm_mix_g, w_in, conv_w, conv_b, dt_bias_f, dt_bias_b, a_log_f, a_log_b, d_skip,
               ssd_norm_g, w_ssd_up, pool_w, pool_scale, w_pool_up, w_out, norm_mlp_g, w_ff1, w_ff2,
               norm_ple_g, w_ple_gate, w_ple_proj)
    layers = [_prepare_layer(*[w[i] for w in stacked]) for i in range(w_in.shape[0])]
    if x_prompt.shape[1] == x_sample.shape[1]:
        y_prompt, y_sample = _run_groups([x_prompt, x_sample], [p_prompt, p_sample], layers, norm_final_g)
    else:
        (y_prompt,) = _run_groups([x_prompt], [p_prompt], layers, norm_final_g)
        (y_sample,) = _run_groups([x_sample], [p_sample], layers, norm_final_g)
    return (y_prompt, y_sample)
```
